```python
import math
import jax, jax.numpy as jnp
from jax import lax
import numpy as np

D_MODEL = 2048
BATCH = 2
SEQ = 4096
DEPTH = 2

GRID_W = 64
CTX_LEN = 256
HEAD_DIM = 128
BRANCH_W = 512
N_BRANCH = 4
NA_HEADS = 4
NA_WIN_ROWS = 8
NA_WIN_COLS = 16
S5_GROUP = 16
S5_GROUPS = BRANCH_W // S5_GROUP
S5_STATE = 64
WG_Q_HEADS = 4
WG_KV_HEADS = 2
WG_WINDOW = 128
WG_BLOCK = 128
ML_HEADS = 4
ML_CHUNK = 128
FFN_DIM = 5632
ROPE_BASE = 10000.0
EPS = 1e-6
NEG_INF = -1e30

IN_SIZES = (
    NA_HEADS * HEAD_DIM, NA_HEADS * HEAD_DIM, NA_HEADS * HEAD_DIM,
    BRANCH_W,
    WG_Q_HEADS * HEAD_DIM, WG_KV_HEADS * HEAD_DIM, WG_KV_HEADS * HEAD_DIM,
    ML_HEADS * HEAD_DIM, ML_HEADS * HEAD_DIM, ML_HEADS * HEAD_DIM, ML_HEADS * HEAD_DIM,
    4 * ML_HEADS,
    N_BRANCH * D_MODEL,
)
IN_W = sum(IN_SIZES)
IN_SPLITS = tuple(sum(IN_SIZES[:i + 1]) for i in range(len(IN_SIZES) - 1))

kernel_name = "hybrid_flow_backbone_ctx_prefix"

F32 = jnp.float32


def rms_norm(x, g):
    x32 = x.astype(F32)
    y = x32 * lax.rsqrt(jnp.mean(x32 * x32, axis=-1, keepdims=True) + EPS)
    return (y * g.astype(F32)).astype(x.dtype)


def modulate(h, shift, scale):
    return h * (1 + scale) + shift


def split_heads(t, n_heads):
    return t.reshape(t.shape[0], t.shape[1], n_heads, HEAD_DIM)


def axial_rope_tables(n_tok):
    t = jnp.arange(n_tok)
    pos = jnp.stack([t // GRID_W, t % GRID_W], axis=-1).astype(F32)
    n_freq = HEAD_DIM // 4
    inv_freq = ROPE_BASE ** (-jnp.arange(n_freq, dtype=F32) / n_freq)
    ang = pos[:, :, None] * inv_freq
    return jnp.cos(ang), jnp.sin(ang)


def apply_axial_rope(x, cos, sin):
    b, n, h, d = x.shape
    xs = x.reshape(b, n, h, 2, 2, d // 4)
    xa, xb = xs[..., 0, :], xs[..., 1, :]
    cc, ss = cos[None, :, None], sin[None, :, None]
    out = jnp.stack([xa * cc - xb * ss, xb * cc + xa * ss], axis=-2)
    return out.reshape(b, n, h, d).astype(x.dtype)


def context_attention(q, k, v, sink=None):
    b, m, hq, d = q.shape
    hk = k.shape[2]
    g = hq // hk
    qg = q.reshape(b, m, hk, g, d)
    s = jnp.einsum('bqhgd,bkhd->bhgqk', qg, k).astype(F32) * (d ** -0.5)
    if sink is not None:
        s_sink = jnp.broadcast_to(sink.reshape(hk, g)[None, :, :, None, None].astype(F32), s.shape[:-1] + (1,))
        s = jnp.concatenate([s, s_sink], axis=-1)
    p = jax.nn.softmax(s, axis=-1)
    if sink is not None:
        p = p[..., :-1]
    o = jnp.einsum('bhgqk,bkhd->bqhgd', p.astype(v.dtype), v)
    return o.reshape(b, m, hq * d)


def neighbourhood_attention(q, k, v, k_ctx, v_ctx, rpb):
    b, n, h, d = q.shape
    rows = n // GRID_W
    wr = min(NA_WIN_ROWS, rows)
    r = jnp.arange(rows)
    r0 = jnp.clip(r - wr // 2, 0, rows - wr)
    key_rows = r0[:, None] + jnp.arange(wr)[None, :]
    qr = q.reshape(b, rows, GRID_W, h, d)
    kg = k.reshape(b, rows, GRID_W, h, d)[:, key_rows]
    vg = v.reshape(b, rows, GRID_W, h, d)[:, key_rows]
    col = jnp.arange(GRID_W)
    c0 = jnp.clip(col - NA_WIN_COLS // 2, 0, GRID_W - NA_WIN_COLS)
    col_ok = (col[None, :] >= c0[:, None]) & (col[None, :] < c0[:, None] + NA_WIN_COLS)
    dc = jnp.clip(col[None, :] - col[:, None] + NA_WIN_COLS - 1, 0, 2 * NA_WIN_COLS - 2)
    dr = key_rows - r[:, None] + NA_WIN_ROWS - 1
    bias = rpb[:, dr[:, None, :, None], dc[None, :, None, :]].astype(F32)
    scale = d ** -0.5
    s_loc = jnp.einsum('brqhd,brikhd->bhrqik', qr, kg).astype(F32) * scale + bias[None]
    s_loc = jnp.where(col_ok[:, None, :], s_loc, NEG_INF)
    n_loc = wr * GRID_W
    s_loc = s_loc.reshape(b, h, rows, GRID_W, n_loc)
    s_ctx = jnp.einsum('brqhd,bchd->bhrqc', qr, k_ctx).astype(F32) * scale
    p = jax.nn.softmax(jnp.concatenate([s_loc, s_ctx], axis=-1), axis=-1)
    p_loc = p[..., :n_loc].reshape(b, h, rows, GRID_W, wr, GRID_W).astype(v.dtype)
    p_ctx = p[..., n_loc:].astype(v.dtype)
    o = jnp.einsum('bhrqik,brikhd->brqhd', p_loc, vg) + jnp.einsum('bhrqc,bchd->brqhd', p_ctx, v_ctx)
    return o.reshape(b, n, h * d)


def windowed_gqa(q, k, v, k_ctx, v_ctx, sink):
    b, n, hq, d = q.shape
    hk = k.shape[2]
    g = hq // hk
    nb = n // WG_BLOCK
    n_band = 2 * WG_WINDOW // WG_BLOCK + 1
    pad = ((0, 0), (WG_WINDOW, WG_WINDOW), (0, 0), (0, 0))
    kp, vp = jnp.pad(k, pad), jnp.pad(v, pad)

    def band(xp):
        parts = [xp[:, j * WG_BLOCK: j * WG_BLOCK + n].reshape(b, nb, WG_BLOCK, hk, d) for j in range(n_band)]
        return jnp.concatenate(parts, axis=2)

    kb, vb = band(kp), band(vp)
    qb = q.reshape(b, nb, WG_BLOCK, hk, g, d)
    blk = jnp.arange(nb)[:, None] * WG_BLOCK
    qpos = blk + jnp.arange(WG_BLOCK)[None, :]
    kpos = blk - WG_WINDOW + jnp.arange(n_band * WG_BLOCK)[None, :]
    valid = ((jnp.abs(qpos[:, :, None] - kpos[:, None, :]) <= WG_WINDOW)
             & (kpos[:, None, :] >= 0) & (kpos[:, None, :] < n))
    scale = d ** -0.5
    s_loc = jnp.einsum('bnqhgd,bnkhd->bhgnqk', qb, kb).astype(F32) * scale
    s_loc = jnp.where(valid, s_loc, NEG_INF)
    s_ctx = jnp.einsum('bnqhgd,bchd->bhgnqc', qb, k_ctx).astype(F32) * scale
    s_sink = jnp.broadcast_to(sink.reshape(hk, g)[None, :, :, None, None, None].astype(F32), s_ctx.shape[:-1] + (1,))
    p = jax.nn.softmax(jnp.concatenate([s_loc, s_ctx, s_sink], axis=-1), axis=-1)
    nk = n_band * WG_BLOCK
    o = (jnp.einsum('bhgnqk,bnkhd->bnqhgd', p[..., :nk].astype(v.dtype), vb)
         + jnp.einsum('bhgnqc,bchd->bnqhgd', p[..., nk:-1].astype(v.dtype), v_ctx))
    return o.reshape(b, n, hq * d)


def s5_discretise(lam_re, lam_im, log_step, b_re, b_im):
    lam_re = jnp.minimum(lam_re.astype(F32), -1e-4)
    lam_im = lam_im.astype(F32)
    step = jnp.exp(log_step.astype(F32))[:, None]
    mag = jnp.exp(lam_re * step)
    ang = lam_im * step
    a_re, a_im = mag * jnp.cos(ang), mag * jnp.sin(ang)
    den = lam_re * lam_re + lam_im * lam_im
    f_re = ((a_re - 1) * lam_re + a_im * lam_im) / den
    f_im = (a_im * lam_re - (a_re - 1) * lam_im) / den
    br, bi = b_re.astype(F32), b_im.astype(F32)
    bb_re = f_re[..., None] * br - f_im[..., None] * bi
    bb_im = f_re[..., None] * bi + f_im[..., None] * br
    return a_re, a_im, bb_re, bb_im


def complex_diag_scan(a_re, a_im, u_re, u_im, s0_re, s0_im):
    ar = jnp.broadcast_to(a_re, u_re.shape)
    ai = jnp.broadcast_to(a_im, u_re.shape)

    def combine(e1, e2):
        a1r, a1i, b1r, b1i = e1
        a2r, a2i, b2r, b2i = e2
        return (a1r * a2r - a1i * a2i, a1r * a2i + a1i * a2r,
                a2r * b1r - a2i * b1i + b2r, a2r * b1i + a2i * b1r + b2i)

    pr, pi, sr, si = lax.associative_scan(combine, (ar, ai, u_re, u_im), axis=1)
    sr = sr + pr * s0_re[:, None] - pi * s0_im[:, None]
    si = si + pr * s0_im[:, None] + pi * s0_re[:, None]
    return sr, si


def s5_states(u, init_f, init_b, disc):
    u32 = u.astype(F32)
    out = []
    for dr, (uu, init) in enumerate(((u32, init_f), (jnp.flip(u32, axis=1), init_b))):
        a_re, a_im, bb_re, bb_im = disc[dr]
        bu_re = jnp.einsum('bngh,gph->bngp', uu, bb_re)
        bu_im = jnp.einsum('bngh,gph->bngp', uu, bb_im)
        s_re, s_im = complex_diag_scan(a_re, a_im, bu_re, bu_im, init[0], init[1])
        if dr == 1:
            s_re, s_im = jnp.flip(s_re, axis=1), jnp.flip(s_im, axis=1)
        out.append((s_re, s_im))
    return out


def s5_readout(u, states, c_re, c_im, d_skip, w_glu, b_glu):
    b, n = u.shape[:2]
    y = d_skip.reshape(S5_GROUPS, S5_GROUP).astype(F32) * u.astype(F32)
    for dr in range(2):
        s_re, s_im = states[dr]
        y = (y + jnp.einsum('bngp,ghp->bngh', s_re, c_re[dr].astype(F32))
             - jnp.einsum('bngp,ghp->bngh', s_im, c_im[dr].astype(F32)))
    y = jax.nn.gelu(y.reshape(b, n, BRANCH_W))
    return (y * jax.nn.sigmoid(y @ w_glu.astype(F32) + b_glu.astype(F32))).astype(u.dtype)


def mlstm_chunkwise(q, k, v, log_i, log_f, state, want_h):
    b, h, n, d = q.shape
    nc = n // ML_CHUNK

    def chunks(t):
        return jnp.moveaxis(t.reshape(b, h, nc, ML_CHUNK, *t.shape[3:]), 2, 0)

    xs = (chunks(q), chunks(k), chunks(v), chunks(log_i), chunks(log_f))
    tri = jnp.tril(jnp.ones((ML_CHUNK, ML_CHUNK), dtype=bool))

    def step(carry, inp):
        c_prev, n_prev, m_prev = carry
        qc, kc, vc, lic, lfc = inp
        bcum = jnp.cumsum(lfc, axis=-1)
        b_last = bcum[..., -1]
        log_end = b_last[..., None] - bcum + lic
        m_new = jnp.maximum(b_last + m_prev, jnp.max(log_end, axis=-1))
        w_end = jnp.exp(log_end - m_new[..., None])
        decay = jnp.exp(b_last + m_prev - m_new)
        c_new = decay[..., None, None] * c_prev + jnp.einsum('bhs,bhsv,bhsk->bhvk', w_end, vc, kc)
        n_new = decay[..., None] * n_prev + jnp.einsum('bhs,bhsk->bhk', w_end, kc)
        if not want_h:
            return (c_new, n_new, m_new), None
        log_w = jnp.where(tri, bcum[..., :, None] - bcum[..., None, :] + lic[..., None, :], NEG_INF)
        log_inter = bcum + m_prev[..., None]
        m_t = jnp.maximum(log_inter, jnp.max(log_w, axis=-1))
        w = jnp.exp(log_w - m_t[..., None])
        inter = jnp.exp(log_inter - m_t)
        s = jnp.einsum('bhtk,bhsk->bhts', qc, kc) * w
        num = inter[..., None] * jnp.einsum('bhvk,bhtk->bhtv', c_prev, qc) + jnp.einsum('bhts,bhsv->bhtv', s, vc)
        den = inter * jnp.einsum('bhk,bhtk->bht', n_prev, qc) + jnp.sum(s, axis=-1)
        h_t = num / jnp.maximum(jnp.abs(den), jnp.exp(-m_t))[..., None]
        return (c_new, n_new, m_new), h_t

    state, hs = lax.scan(step, state, xs)
    if want_h:
        hs = jnp.moveaxis(hs, 0, 2).reshape(b, h, n, d)
    return state, hs


def mlstm_bidir(q, k, v, gate_pre, init_f, init_b, want_h):
    q, k, v = [jnp.moveaxis(t, 1, 2) for t in (q, k, v)]
    k = k * (HEAD_DIM ** -0.5)
    g = jnp.moveaxis(gate_pre, 1, 3)
    li_f, lf_f = g[:, 0], jax.nn.log_sigmoid(g[:, 1])
    li_b, lf_b = g[:, 2], jax.nn.log_sigmoid(g[:, 3])
    st_f, h_f = mlstm_chunkwise(q, k, v, li_f, lf_f, init_f, want_h)

    def fl(t):
        return jnp.flip(t, axis=2)

    st_b, h_b = mlstm_chunkwise(fl(q), fl(k), fl(v), fl(li_b), fl(lf_b), init_b, want_h)
    h = h_f + fl(h_b) if want_h else None
    return st_f, st_b, h


def mlstm_output(h, o_pre, ml_norm):
    h = jnp.moveaxis(h, 1, 2).astype(F32)
    h = h * lax.rsqrt(jnp.mean(h * h, axis=-1, keepdims=True) + EPS) * ml_norm.reshape(ML_HEADS, HEAD_DIM).astype(F32)
    b, n = h.shape[:2]
    return (h.reshape(b, n, ML_HEADS * HEAD_DIM) * jax.nn.sigmoid(o_pre.astype(F32))).astype(o_pre.dtype)


def merge_branches(branches, gate_pre, w_branch, w_out):
    br = jnp.stack(branches, axis=2)
    proj = jnp.einsum('bnim,imd->bnid', br, w_branch)
    gates = jax.nn.sigmoid(gate_pre.reshape(proj.shape).astype(F32)).astype(proj.dtype)
    return jnp.sum(gates * proj, axis=2) @ w_out


def token_mixing(hc, hx, cos, sin, w_in, na_rpb, wg_sink, s5_lam_re, s5_lam_im, s5_log_step, s5_b_re, s5_b_im,
                 s5_c_re, s5_c_im, s5_d, s5_w_glu, s5_b_glu, ml_gate_bias, ml_norm, w_branch, w_out, ctx_out):
    b, n_ctx = hc.shape[:2]
    n_lat = hx.shape[1]
    (c_naq, c_nak, c_nav, c_s5, c_wgq, c_wgk, c_wgv, c_mlq, c_mlk, c_mlv, c_mlo, c_mlg, c_gate) = \
        jnp.split(hc @ w_in, IN_SPLITS, axis=-1)
    (x_naq, x_nak, x_nav, x_s5, x_wgq, x_wgk, x_wgv, x_mlq, x_mlk, x_mlv, x_mlo, x_mlg, x_gate) = \
        jnp.split(hx @ w_in, IN_SPLITS, axis=-1)
    disc = [s5_discretise(s5_lam_re[dr], s5_lam_im[dr], s5_log_step[dr], s5_b_re[dr], s5_b_im[dr]) for dr in range(2)]

    na_kc, na_vc = split_heads(c_nak, NA_HEADS), split_heads(c_nav, NA_HEADS)
    wg_kc, wg_vc = split_heads(c_wgk, WG_KV_HEADS), split_heads(c_wgv, WG_KV_HEADS)
    u_c = c_s5.reshape(b, n_ctx, S5_GROUPS, S5_GROUP)
    zs = jnp.zeros((b, S5_GROUPS, S5_STATE), F32)
    sc_f, sc_b = s5_states(u_c, (zs, zs), (zs, zs), disc)
    ml_zero = (jnp.zeros((b, ML_HEADS, HEAD_DIM, HEAD_DIM), F32), jnp.zeros((b, ML_HEADS, HEAD_DIM), F32),
               jnp.zeros((b, ML_HEADS), F32))
    gate_c = c_mlg.reshape(b, n_ctx, 4, ML_HEADS).astype(F32) + ml_gate_bias.astype(F32)
    mst_f, mst_b, h_c = mlstm_bidir(split_heads(c_mlq, ML_HEADS), split_heads(c_mlk, ML_HEADS),
                                    split_heads(c_mlv, ML_HEADS), gate_c, ml_zero, ml_zero, ctx_out)
    y_c = None
    if ctx_out:
        na_c = context_attention(split_heads(c_naq, NA_HEADS), na_kc, na_vc)
        s5_c = s5_readout(u_c, (sc_f, sc_b), s5_c_re, s5_c_im, s5_d, s5_w_glu, s5_b_glu)
        wg_c = context_attention(split_heads(c_wgq, WG_Q_HEADS), wg_kc, wg_vc, wg_sink)
        ml_c = mlstm_output(h_c, c_mlo, ml_norm)
        y_c = merge_branches((na_c, s5_c, wg_c, ml_c), c_gate, w_branch, w_out)

    na_x = neighbourhood_attention(split_heads(x_naq, NA_HEADS), split_heads(x_nak, NA_HEADS),
                                   split_heads(x_nav, NA_HEADS), na_kc, na_vc, na_rpb)
    u_x = x_s5.reshape(b, n_lat, S5_GROUPS, S5_GROUP)
    sx = s5_states(u_x, (sc_f[0][:, -1], sc_f[1][:, -1]), (sc_b[0][:, 0], sc_b[1][:, 0]), disc)
    s5_x = s5_readout(u_x, sx, s5_c_re, s5_c_im, s5_d, s5_w_glu, s5_b_glu)
    wg_x = windowed_gqa(apply_axial_rope(split_heads(x_wgq, WG_Q_HEADS), cos, sin),
                        apply_axial_rope(split_heads(x_wgk, WG_KV_HEADS), cos, sin),
                        split_heads(x_wgv, WG_KV_HEADS), wg_kc, wg_vc, wg_sink)
    gate_x = x_mlg.reshape(b, n_lat, 4, ML_HEADS).astype(F32) + ml_gate_bias.astype(F32)
    _, _, h_x = mlstm_bidir(split_heads(x_mlq, ML_HEADS), split_heads(x_mlk, ML_HEADS),
                            split_heads(x_mlv, ML_HEADS), gate_x, mst_f, mst_b, True)
    ml_x = mlstm_output(h_x, x_mlo, ml_norm)
    y_x = merge_branches((na_x, s5_x, wg_x, ml_x), x_gate, w_branch, w_out)
    return y_c, y_x


def conv_ffn(h, w_up, conv_w, conv_b, w_down):
    u = h @ w_up
    u = lax.conv_general_dilated(u, conv_w[:, None, :].astype(u.dtype), window_strides=(1,), padding=((1, 1),),
                                 dimension_numbers=('NWC', 'WIO', 'NWC'), feature_group_count=u.shape[-1]) + conv_b
    a, g = jnp.split(u, 2, axis=-1)
    return (a * jax.nn.silu(g)) @ w_down


def setup_inputs(seed: int = 0) -> dict:
    key = jax.random.key(seed)
    ks = iter(jax.random.split(key, 40))

    def nrm(shape, scale):
        return jax.random.normal(next(ks), shape, F32) * scale

    L, D, G, P, Hg, F = DEPTH, D_MODEL, S5_GROUPS, S5_STATE, S5_GROUP, FFN_DIM
    f_bias = jnp.linspace(3.0, 6.0, ML_HEADS, dtype=F32)
    zero_h = jnp.zeros((ML_HEADS,), F32)
    gate_bias = jnp.stack([zero_h, f_bias, zero_h, f_bias])[None] + nrm((L, 4, ML_HEADS), 0.1)
    return {
        "x": nrm((BATCH, SEQ, D), 1.0),
        "c": nrm((BATCH, D), 1.0),
        "ctx": nrm((BATCH, CTX_LEN, D), 1.0),
        "c_ctx": nrm((D,), 1.0),
        "w_ada": nrm((L, D, 6 * D), D ** -0.5),
        "b_ada": nrm((L, 6 * D), 0.01),
        "g_mix_pre": 1.0 + nrm((L, D), 0.01),
        "g_mix_post": 1.0 + nrm((L, D), 0.01),
        "g_ffn_pre": 1.0 + nrm((L, D), 0.01),
        "g_ffn_post": 1.0 + nrm((L, D), 0.01),
        "w_in": nrm((L, D, IN_W), D ** -0.5),
        "na_rpb": nrm((L, NA_HEADS, 2 * NA_WIN_ROWS - 1, 2 * NA_WIN_COLS - 1), 0.1),
        "wg_sink": nrm((L, WG_Q_HEADS), 0.5),
        "s5_lam_re": -0.5 + nrm((L, 2, G, P), 0.01),
        "s5_lam_im": math.pi * jnp.arange(P, dtype=F32) + nrm((L, 2, G, P), 0.01),
        "s5_log_step": jax.random.uniform(next(ks), (L, 2, G), F32, math.log(1e-3), math.log(1e-1)),
        "s5_b_re": nrm((L, 2, G, P, Hg), (2 * Hg) ** -0.5),
        "s5_b_im": nrm((L, 2, G, P, Hg), (2 * Hg) ** -0.5),
        "s5_c_re": nrm((L, 2, G, Hg, P), P ** -0.5),
        "s5_c_im": nrm((L, 2, G, Hg, P), P ** -0.5),
        "s5_d": nrm((L, BRANCH_W), 1.0),
        "s5_w_glu": nrm((L, BRANCH_W, BRANCH_W), BRANCH_W ** -0.5),
        "s5_b_glu": nrm((L, BRANCH_W), 0.01),
        "ml_gate_bias": gate_bias,
        "ml_norm": 1.0 + nrm((L, ML_HEADS * HEAD_DIM), 0.01),
        "w_branch": nrm((L, N_BRANCH, BRANCH_W, D), BRANCH_W ** -0.5),
        "w_out": nrm((L, D, D), D ** -0.5),
        "w_up": nrm((L, D, 2 * F), D ** -0.5),
        "ffn_conv_w": nrm((L, 3, 2 * F), 3 ** -0.5),
        "ffn_conv_b": nrm((L, 2 * F), 0.01),
        "w_down": nrm((L, F, D), F ** -0.5),
    }


def reference(x, c, ctx, c_ctx, w_ada, b_ada, g_mix_pre, g_mix_post, g_ffn_pre, g_ffn_post, w_in, na_rpb, wg_sink,
              s5_lam_re, s5_lam_im, s5_log_step, s5_b_re, s5_b_im, s5_c_re, s5_c_im, s5_d, s5_w_glu, s5_b_glu,
              ml_gate_bias, ml_norm, w_branch, w_out, w_up, ffn_conv_w, ffn_conv_b, w_down):
    cos, sin = axial_rope_tables(x.shape[1])
    for l in range(DEPTH):
        ctx_out = l < DEPTH - 1
        mod_x = (jax.nn.silu(c) @ w_ada[l] + b_ada[l])[:, None, :]
        mod_c = (jax.nn.silu(c_ctx) @ w_ada[l] + b_ada[l])[None, None, :]
        sh_mx, sc_mx, gt_mx, sh_fx, sc_fx, gt_fx = jnp.split(mod_x, 6, axis=-1)
        sh_mc, sc_mc, gt_mc, sh_fc, sc_fc, gt_fc = jnp.split(mod_c, 6, axis=-1)

        hx = modulate(rms_norm(x, g_mix_pre[l]), sh_mx, sc_mx)
        hc = modulate(rms_norm(ctx, g_mix_pre[l]), sh_mc, sc_mc)
        y_c, y_x = token_mixing(hc, hx, cos, sin, w_in[l], na_rpb[l], wg_sink[l], s5_lam_re[l], s5_lam_im[l],
                                s5_log_step[l], s5_b_re[l], s5_b_im[l], s5_c_re[l], s5_c_im[l], s5_d[l],
                                s5_w_glu[l], s5_b_glu[l], ml_gate_bias[l], ml_norm[l], w_branch[l], w_out[l], ctx_out)
        x = x + gt_mx * rms_norm(y_x, g_mix_post[l])
        hx = modulate(rms_norm(x, g_ffn_pre[l]), sh_fx, sc_fx)
        x = x + gt_fx * rms_norm(conv_ffn(hx, w_up[l], ffn_conv_w[l], ffn_conv_b[l], w_down[l]), g_ffn_post[l])

        if ctx_out:
            ctx = ctx + gt_mc * rms_norm(y_c, g_mix_post[l])
            hc = modulate(rms_norm(ctx, g_ffn_pre[l]), sh_fc, sc_fc)
            ctx = ctx + gt_fc * rms_norm(conv_ffn(hc, w_up[l], ffn_conv_w[l], ffn_conv_b[l], w_down[l]), g_ffn_post[l])
    return x
```

```python
import functools
import math

import jax
import jax.numpy as jnp
from jax import lax
from jax.experimental import pallas as pl
from jax.experimental.pallas import tpu as pltpu

F32 = jnp.float32
BF16 = jnp.bfloat16

D_MODEL = 2048
BATCH = 2
SEQ = 4096
DEPTH = 2
GRID_W = 64
CTX_LEN = 256
HEAD_DIM = 128
BRANCH_W = 512
N_BRANCH = 4
NA_HEADS = 4
NA_WIN_ROWS = 8
NA_WIN_COLS = 16
S5_GROUP = 16
S5_GROUPS = BRANCH_W // S5_GROUP
S5_STATE = 64
S5_CH = S5_GROUPS * S5_STATE
WG_Q_HEADS = 4
WG_KV_HEADS = 2
WG_WINDOW = 128
WG_BLOCK = 128
ML_HEADS = 4
ML_CHUNK = 128
FFN_DIM = 5632
ROPE_BASE = 10000.0
EPS = 1e-6
NEG_INF = -1e30

N_TOK = CTX_LEN + SEQ
T_ALL = BATCH * N_TOK
GRID_ROWS = SEQ // GRID_W
ATT_SCALE = HEAD_DIM ** -0.5

COL_NAQ, COL_NAK, COL_NAV, COL_S5 = 0, 512, 1024, 1536
COL_WGQ, COL_WGK, COL_WGV = 2048, 2560, 2816
COL_MLQ, COL_MLK, COL_MLV, COL_MLO = 3072, 3584, 4096, 4608
PROJ_W = 5120
COL_MLG = PROJ_W
COL_GATE = PROJ_W + 4 * ML_HEADS
GATE_W = N_BRANCH * D_MODEL

MIB = 1024 * 1024
ATT_QBLK = 256
NA_ROWS_PER_STEP = ATT_QBLK // GRID_W
S5_CHUNK = 256
S5_SUB = 8


def _cparams(vmem_mib=None):
    if vmem_mib is None:
        return None
    return pltpu.CompilerParams(vmem_limit_bytes=vmem_mib * MIB)


def _dot(a, b):
    return jnp.dot(a, b, preferred_element_type=F32)


def _dot_nt(a, b):
    return lax.dot_general(a, b, (((1,), (1,)), ((), ())), preferred_element_type=F32)


def _pick_rows(mod_ref, tile, tm):
    start = tile * tm
    b = start // N_TOK
    lat = jnp.where(b == 0, mod_ref[0:1, :], mod_ref[1:2, :])
    ctx = mod_ref[2:3, :]
    pos = start % N_TOK + lax.broadcasted_iota(jnp.int32, (tm, 1), 0)
    return jnp.where(pos < CTX_LEN, ctx, lat)


def _ada_kernel(c_ref, w_ref, b_ref, o_ref):
    c = c_ref[...]
    s = (c * jax.nn.sigmoid(c)).astype(BF16)
    o_ref[...] = _dot(s, w_ref[...].astype(BF16)) + b_ref[...]


def _ada(cs, w_ada, b_ada):
    tn = 1024
    n_out = 6 * D_MODEL
    return pl.pallas_call(
        _ada_kernel,
        out_shape=jax.ShapeDtypeStruct((DEPTH, 8, n_out), F32),
        grid=(DEPTH, n_out // tn),
        in_specs=[
            pl.BlockSpec((8, D_MODEL), lambda l, j: (0, 0)),
            pl.BlockSpec((None, D_MODEL, tn), lambda l, j: (l, 0, j)),
            pl.BlockSpec((None, 1, tn), lambda l, j: (l, 0, j)),
        ],
        out_specs=pl.BlockSpec((None, 8, tn), lambda l, j: (l, 0, j)),
        compiler_params=_cparams(40),
        name="ada",
    )(cs, w_ada, b_ada.reshape(DEPTH, 1, n_out))


def _norm_mod_kernel(x_ref, g_ref, sh_ref, sc_ref, o_ref, *, tm):
    i = pl.program_id(0)
    x = x_ref[...]
    y = x * lax.rsqrt(jnp.mean(x * x, axis=-1, keepdims=True) + EPS) * g_ref[...]
    o_ref[...] = (y * (1.0 + _pick_rows(sc_ref, i, tm)) + _pick_rows(sh_ref, i, tm)).astype(BF16)


def _norm_mod(x, g, mod_l, shift_chunk, scale_chunk):
    tm = 544
    return pl.pallas_call(
        functools.partial(_norm_mod_kernel, tm=tm),
        out_shape=jax.ShapeDtypeStruct((T_ALL, D_MODEL), BF16),
        grid=(T_ALL // tm,),
        in_specs=[
            pl.BlockSpec((tm, D_MODEL), lambda i: (i, 0)),
            pl.BlockSpec((1, D_MODEL), lambda i: (0, 0)),
            pl.BlockSpec((8, D_MODEL), lambda i: (0, shift_chunk)),
            pl.BlockSpec((8, D_MODEL), lambda i: (0, scale_chunk)),
        ],
        out_specs=pl.BlockSpec((tm, D_MODEL), lambda i: (i, 0)),
        name="norm_mod",
    )(x, g.reshape(1, D_MODEL), mod_l, mod_l)


def _mm_kernel(a_ref, w_ref, o_ref, *, act):
    r = _dot(a_ref[...], w_ref[...])
    if act == "sigmoid":
        r = jax.nn.sigmoid(r)
    o_ref[...] = r.astype(o_ref.dtype)


def _matmul(a, w, col_block0, n_cols, tn, out_dtype, act=None, tm=1088):
    m, k = a.shape
    return pl.pallas_call(
        functools.partial(_mm_kernel, act=act),
        out_shape=jax.ShapeDtypeStruct((m, n_cols), out_dtype),
        grid=(m // tm, n_cols // tn),
        in_specs=[
            pl.BlockSpec((tm, k), lambda i, j: (i, 0)),
            pl.BlockSpec((k, tn), lambda i, j: (0, col_block0 + j)),
        ],
        out_specs=pl.BlockSpec((tm, tn), lambda i, j: (i, j)),
        compiler_params=_cparams(40),
        name="matmul",
    )(a, w)


def _na_kernel(q_ref, k_ref, v_ref, tb_ref, o_ref):
    step = pl.program_id(2)
    kc = k_ref[0:CTX_LEN, :].astype(BF16)
    vc = v_ref[0:CTX_LEN, :].astype(BF16)

    @pl.when(step == 0)
    def _context_queries():
        q = q_ref[...].astype(BF16)
        s = _dot_nt(q, kc) * ATT_SCALE
        p = jnp.exp(s - jnp.max(s, axis=-1, keepdims=True))
        o = _dot(p.astype(BF16), vc) / jnp.sum(p, axis=-1, keepdims=True)
        o_ref[...] = o.astype(o_ref.dtype)

    @pl.when(step > 0)
    def _latent_queries():
        n_win = NA_WIN_ROWS * GRID_W
        for rl in range(NA_ROWS_PER_STEP):
            r = (step - 1) * NA_ROWS_PER_STEP + rl
            r0 = jnp.clip(r - NA_WIN_ROWS // 2, 0, GRID_ROWS - NA_WIN_ROWS)
            q = q_ref[rl * GRID_W:(rl + 1) * GRID_W, :].astype(BF16)
            start = pl.multiple_of(CTX_LEN + r0 * GRID_W, GRID_W)
            kw = k_ref[pl.ds(start, n_win), :].astype(BF16)
            vw = v_ref[pl.ds(start, n_win), :].astype(BF16)
            bias = tb_ref[r0 - r + NA_WIN_ROWS - 1]
            s_loc = jnp.where(bias > 0.5 * NEG_INF, _dot_nt(q, kw) * ATT_SCALE + bias, NEG_INF)
            s_ctx = _dot_nt(q, kc) * ATT_SCALE
            m = jnp.maximum(jnp.max(s_loc, axis=-1, keepdims=True), jnp.max(s_ctx, axis=-1, keepdims=True))
            p_loc = jnp.exp(s_loc - m)
            p_ctx = jnp.exp(s_ctx - m)
            den = jnp.sum(p_loc, axis=-1, keepdims=True) + jnp.sum(p_ctx, axis=-1, keepdims=True)
            o = (_dot(p_loc.astype(BF16), vw) + _dot(p_ctx.astype(BF16), vc)) / den
            o_ref[rl * GRID_W:(rl + 1) * GRID_W, :] = o.astype(o_ref.dtype)


def _na_bias_table(rpb):
    col = jnp.arange(GRID_W)
    c0 = jnp.clip(col - NA_WIN_COLS // 2, 0, GRID_W - NA_WIN_COLS)
    col_ok = (col[None, :] >= c0[:, None]) & (col[None, :] < c0[:, None] + NA_WIN_COLS)
    dc = jnp.clip(col[None, :] - col[:, None] + NA_WIN_COLS - 1, 0, 2 * NA_WIN_COLS - 2)
    per_dr = jnp.where(col_ok[None, None], rpb[:, :, dc].astype(F32), NEG_INF)
    dr = jnp.arange(NA_WIN_ROWS)[:, None] + jnp.arange(NA_WIN_ROWS)[None, :]
    tb = per_dr[:, dr]
    return jnp.moveaxis(tb, 2, 3).reshape(NA_HEADS, NA_WIN_ROWS, GRID_W, NA_WIN_ROWS * GRID_W)


def _neighbourhood_attention(proj, rpb):
    tb = _na_bias_table(rpb)
    n_steps = N_TOK // ATT_QBLK
    hb = HEAD_DIM
    return pl.pallas_call(
        _na_kernel,
        out_shape=jax.ShapeDtypeStruct((BATCH, N_TOK, BRANCH_W), BF16),
        grid=(BATCH, NA_HEADS, n_steps),
        in_specs=[
            pl.BlockSpec((None, ATT_QBLK, hb), lambda b, h, s: (b, s, COL_NAQ // hb + h)),
            pl.BlockSpec((None, N_TOK, hb), lambda b, h, s: (b, 0, COL_NAK // hb + h)),
            pl.BlockSpec((None, N_TOK, hb), lambda b, h, s: (b, 0, COL_NAV // hb + h)),
            pl.BlockSpec((None, NA_WIN_ROWS, GRID_W, NA_WIN_ROWS * GRID_W), lambda b, h, s: (h, 0, 0, 0)),
        ],
        out_specs=pl.BlockSpec((None, ATT_QBLK, hb), lambda b, h, s: (b, s, h)),
        name="na_attn",
    )(proj, proj, proj, tb)


def _rope(x, cos, sin_signed):
    lane = lax.broadcasted_iota(jnp.int32, x.shape, 1)
    partner = jnp.where(lane % 64 < 32, pltpu.roll(x, 96, 1), pltpu.roll(x, 32, 1))
    return x * cos + partner * sin_signed


def _wg_kernel(sink_ref, q_ref, k_ref, v_ref, cos_ref, sin_ref, o_ref, kr_ref):
    hk = pl.program_id(1)
    step = pl.program_id(2)
    kc = k_ref[0:CTX_LEN, :].astype(BF16)
    vc = v_ref[0:CTX_LEN, :].astype(BF16)
    group = WG_Q_HEADS // WG_KV_HEADS

    @pl.when(step == 0)
    def _context_queries():
        for g in range(group):
            sink = sink_ref[hk * group + g]
            q = q_ref[:, g * HEAD_DIM:(g + 1) * HEAD_DIM].astype(BF16)
            s = _dot_nt(q, kc) * ATT_SCALE
            m = jnp.maximum(jnp.max(s, axis=-1, keepdims=True), sink)
            p = jnp.exp(s - m)
            den = jnp.sum(p, axis=-1, keepdims=True) + jnp.exp(sink - m)
            o_ref[:, g * HEAD_DIM:(g + 1) * HEAD_DIM] = (_dot(p.astype(BF16), vc) / den).astype(o_ref.dtype)

    @pl.when(step == 1)
    def _rope_keys():
        kr_ref[...] = _rope(k_ref[CTX_LEN:N_TOK, :], cos_ref[...], sin_ref[...]).astype(BF16)

    @pl.when(step > 0)
    def _latent_queries():
        nb = SEQ // WG_BLOCK
        qi = lax.broadcasted_iota(jnp.int32, (WG_BLOCK, WG_BLOCK), 0)
        kj = lax.broadcasted_iota(jnp.int32, (WG_BLOCK, WG_BLOCK), 1)
        for blk in range(ATT_QBLK // WG_BLOCK):
            n = (step - 1) * (ATT_QBLK // WG_BLOCK) + blk
            base = pl.multiple_of(n * WG_BLOCK, WG_BLOCK)
            prev = pl.multiple_of(jnp.maximum(n - 1, 0) * WG_BLOCK, WG_BLOCK)
            nxt = pl.multiple_of(jnp.minimum(n + 1, nb - 1) * WG_BLOCK, WG_BLOCK)
            cos_q = cos_ref[pl.ds(base, WG_BLOCK), :]
            sin_q = sin_ref[pl.ds(base, WG_BLOCK), :]
            k_prev = kr_ref[pl.ds(prev, WG_BLOCK), :]
            k_self = kr_ref[pl.ds(base, WG_BLOCK), :]
            k_next = kr_ref[pl.ds(nxt, WG_BLOCK), :]
            v_prev = v_ref[pl.ds(CTX_LEN + prev, WG_BLOCK), :].astype(BF16)
            v_self = v_ref[pl.ds(CTX_LEN + base, WG_BLOCK), :].astype(BF16)
            v_next = v_ref[pl.ds(CTX_LEN + nxt, WG_BLOCK), :].astype(BF16)
            ok_prev = kj >= qi + jnp.where(n > 0, 0, WG_BLOCK)
            ok_next = kj + jnp.where(n < nb - 1, 0, WG_BLOCK) <= qi
            for g in range(group):
                sink = sink_ref[hk * group + g]
                rows = slice(blk * WG_BLOCK, (blk + 1) * WG_BLOCK)
                cols = slice(g * HEAD_DIM, (g + 1) * HEAD_DIM)
                q = _rope(q_ref[rows, cols], cos_q, sin_q).astype(BF16)
                s_prev = jnp.where(ok_prev, _dot_nt(q, k_prev) * ATT_SCALE, NEG_INF)
                s_self = _dot_nt(q, k_self) * ATT_SCALE
                s_next = jnp.where(ok_next, _dot_nt(q, k_next) * ATT_SCALE, NEG_INF)
                s_ctx = _dot_nt(q, kc) * ATT_SCALE
                m = jnp.maximum(
                    jnp.maximum(jnp.max(s_prev, axis=-1, keepdims=True), jnp.max(s_self, axis=-1, keepdims=True)),
                    jnp.maximum(jnp.max(s_next, axis=-1, keepdims=True), jnp.max(s_ctx, axis=-1, keepdims=True)))
                m = jnp.maximum(m, sink)
                p_prev = jnp.exp(s_prev - m)
                p_self = jnp.exp(s_self - m)
                p_next = jnp.exp(s_next - m)
                p_ctx = jnp.exp(s_ctx - m)
                den = (jnp.sum(p_prev, axis=-1, keepdims=True) + jnp.sum(p_self, axis=-1, keepdims=True)
                       + jnp.sum(p_next, axis=-1, keepdims=True) + jnp.sum(p_ctx, axis=-1, keepdims=True)
                       + jnp.exp(sink - m))
                o = (_dot(p_prev.astype(BF16), v_prev) + _dot(p_self.astype(BF16), v_self)
                     + _dot(p_next.astype(BF16), v_next) + _dot(p_ctx.astype(BF16), vc)) / den
                o_ref[rows, cols] = o.astype(o_ref.dtype)


def _rope_tables():
    t = jnp.arange(SEQ)
    pos = jnp.stack([t // GRID_W, t % GRID_W], axis=-1).astype(F32)
    n_freq = HEAD_DIM // 4
    inv_freq = ROPE_BASE ** (-jnp.arange(n_freq, dtype=F32) / n_freq)
    ang = pos[:, :, None] * inv_freq
    cos, sin = jnp.cos(ang), jnp.sin(ang)
    cos_t = jnp.concatenate([cos[:, 0], cos[:, 0], cos[:, 1], cos[:, 1]], axis=-1)
    sin_t = jnp.concatenate([-sin[:, 0], sin[:, 0], -sin[:, 1], sin[:, 1]], axis=-1)
    return cos_t, sin_t


def _windowed_gqa(proj, sink, cos_t, sin_t):
    n_steps = N_TOK // ATT_QBLK
    qw = (WG_Q_HEADS // WG_KV_HEADS) * HEAD_DIM
    hb = HEAD_DIM
    return pl.pallas_call(
        _wg_kernel,
        out_shape=jax.ShapeDtypeStruct((BATCH, N_TOK, BRANCH_W), BF16),
        grid=(BATCH, WG_KV_HEADS, n_steps),
        in_specs=[
            pl.BlockSpec(memory_space=pltpu.SMEM),
            pl.BlockSpec((None, ATT_QBLK, qw), lambda b, h, s: (b, s, COL_WGQ // qw + h)),
            pl.BlockSpec((None, N_TOK, hb), lambda b, h, s: (b, 0, COL_WGK // hb + h)),
            pl.BlockSpec((None, N_TOK, hb), lambda b, h, s: (b, 0, COL_WGV // hb + h)),
            pl.BlockSpec((SEQ, hb), lambda b, h, s: (0, 0)),
            pl.BlockSpec((SEQ, hb), lambda b, h, s: (0, 0)),
        ],
        out_specs=pl.BlockSpec((None, ATT_QBLK, qw), lambda b, h, s: (b, s, h)),
        scratch_shapes=[pltpu.VMEM((SEQ, hb), BF16)],
        name="wg_attn",
    )(sink, proj, proj, proj, cos_t, sin_t)


def _s5_disc_kernel(lre_ref, lim_ref, lstep_ref, bre_ref, bim_ref, pre_ref, pim_ref, bbre_ref, bbim_ref):
    lre = jnp.minimum(lre_ref[...], -1e-4)
    lim = lim_ref[...]
    step = jnp.exp(lstep_ref[...])
    kk = (lax.broadcasted_iota(jnp.int32, (S5_SUB, 1), 0) + 1).astype(F32)
    mag = jnp.exp(kk * (lre * step))
    ang = kk * (lim * step)
    p_re = mag * jnp.cos(ang)
    p_im = mag * jnp.sin(ang)
    pre_ref[...] = p_re
    pim_ref[...] = p_im
    a_re = p_re[0:1, :]
    a_im = p_im[0:1, :]
    den = lre * lre + lim * lim
    f_re = ((a_re - 1.0) * lre + a_im * lim) / den
    f_im = (a_im * lre - (a_re - 1.0) * lim) / den
    br = bre_ref[...]
    bi = bim_ref[...]
    bbre_ref[...] = f_re * br - f_im * bi
    bbim_ref[...] = f_re * bi + f_im * br


def _s5_discretise(lam_re, lam_im, log_step, b_re, b_im):
    lre = lam_re.reshape(2, 1, S5_CH)
    lim = lam_im.reshape(2, 1, S5_CH)
    lstep = jnp.repeat(log_step, S5_STATE, axis=-1).reshape(2, 1, S5_CH)
    br = jnp.transpose(b_re, (0, 3, 1, 2)).reshape(2, S5_GROUP, S5_CH)
    bi = jnp.transpose(b_im, (0, 3, 1, 2)).reshape(2, S5_GROUP, S5_CH)
    row = lambda n: pl.BlockSpec((None, n, S5_CH), lambda d: (d, 0, 0))
    return pl.pallas_call(
        _s5_disc_kernel,
        out_shape=(jax.ShapeDtypeStruct((2, S5_SUB, S5_CH), F32), jax.ShapeDtypeStruct((2, S5_SUB, S5_CH), F32),
                   jax.ShapeDtypeStruct((2, S5_GROUP, S5_CH), F32), jax.ShapeDtypeStruct((2, S5_GROUP, S5_CH), F32)),
        grid=(2,),
        in_specs=[row(1), row(1), row(1), row(S5_GROUP), row(S5_GROUP)],
        out_specs=(row(S5_SUB), row(S5_SUB), row(S5_GROUP), row(S5_GROUP)),
        name="s5_disc",
    )(lre, lim, lstep, br, bi)


def _s5_scan_kernel(u_ref, bbre_ref, bbim_ref, cre_ref, cim_ref, tab_ref, y_ref, sre_ref, sim_ref, car_ref, *, rev):
    j = pl.program_id(1)

    @pl.when(j == 0)
    def _reset():
        car_ref[...] = jnp.zeros_like(car_ref)

    u = u_ref[...].astype(BF16)
    n_col_tiles = S5_CH // 256
    for c in range(n_col_tiles):
        ub = u[:, 128 * (c // 2):128 * (c // 2) + 128]
        sre_ref[:, 256 * c:256 * (c + 1)] = _dot(ub, bbre_ref[c])
        sim_ref[:, 256 * c:256 * (c + 1)] = _dot(ub, bbim_ref[c])

    n_groups = S5_CHUNK // S5_SUB
    last = 0 if rev else S5_SUB - 1

    def group(gi, carry):
        cr, ci = carry
        g = (n_groups - 1 - gi) if rev else gi
        r0 = pl.multiple_of(g * S5_SUB, S5_SUB)
        xr = sre_ref[pl.ds(r0, S5_SUB), :]
        xi = sim_ref[pl.ds(r0, S5_SUB), :]
        for t, k in enumerate((1, 2, 4)):
            shift = (S5_SUB - k) if rev else k
            ar = tab_ref[2 * t]
            ai = tab_ref[2 * t + 1]
            rr = pltpu.roll(xr, shift, 0)
            ri = pltpu.roll(xi, shift, 0)
            xr, xi = xr + ar * rr - ai * ri, xi + ar * ri + ai * rr
        apr = tab_ref[6]
        api = tab_ref[7]
        xr, xi = xr + apr * cr - api * ci, xi + apr * ci + api * cr
        sre_ref[pl.ds(r0, S5_SUB), :] = xr
        sim_ref[pl.ds(r0, S5_SUB), :] = xi
        return xr[last:last + 1, :], xi[last:last + 1, :]

    cr, ci = lax.fori_loop(0, n_groups, group, (car_ref[0:1, :], car_ref[1:2, :]))
    car_ref[0:1, :] = cr
    car_ref[1:2, :] = ci

    n_out_tiles = BRANCH_W // 128
    kw = S5_CH // n_out_tiles
    for oc in range(n_out_tiles):
        sr = sre_ref[:, kw * oc:kw * (oc + 1)].astype(BF16)
        si = sim_ref[:, kw * oc:kw * (oc + 1)].astype(BF16)
        y_ref[:, 128 * oc:128 * (oc + 1)] = _dot(sr, cre_ref[oc]) + _dot(si, cim_ref[oc])


def _chunk_index(j, n_chunks, n_ctx_chunks, rev):
    if not rev:
        return j
    return jnp.where(j < n_ctx_chunks, n_ctx_chunks - 1 - j, n_chunks - 1 - (j - n_ctx_chunks))


def _s5_tables(p_re, p_im, bb_re, bb_im, c_re, c_im, rev):
    row = jnp.arange(S5_SUB)[:, None]
    tabs = []
    for k in (1, 2, 4):
        keep = (row <= S5_SUB - 1 - k) if rev else (row >= k)
        tabs += [jnp.where(keep, p_re[k - 1][None, :], 0.0), jnp.where(keep, p_im[k - 1][None, :], 0.0)]
    tabs += [p_re[::-1], p_im[::-1]] if rev else [p_re, p_im]
    tab = jnp.stack(tabs)
    eye = jnp.eye(S5_GROUPS, dtype=F32)

    def in_tiles(bb):
        bbg = bb.reshape(S5_GROUP, S5_GROUPS, S5_STATE)
        full = (eye[:, None, :, None] * jnp.transpose(bbg, (1, 0, 2))[:, :, None, :]).reshape(BRANCH_W, S5_CH)
        return jnp.stack([full[128 * (c // 2):128 * (c // 2) + 128, 256 * c:256 * (c + 1)]
                          for c in range(S5_CH // 256)]).astype(BF16)

    def out_tiles(cc):
        full = (eye[:, None, :, None] * jnp.transpose(cc, (0, 2, 1))[:, :, None, :]).reshape(S5_CH, BRANCH_W)
        kw = S5_CH // 4
        return jnp.stack([full[kw * oc:kw * (oc + 1), 128 * oc:128 * (oc + 1)] for oc in range(4)]).astype(BF16)

    return tab, in_tiles(bb_re), in_tiles(bb_im), out_tiles(c_re.astype(F32)), out_tiles(-c_im.astype(F32))


def _s5_scan(proj, tab, bbre_t, bbim_t, cre_t, cim_t, rev):
    n_chunks = N_TOK // S5_CHUNK
    n_ctx_chunks = CTX_LEN // S5_CHUNK
    cidx = functools.partial(_chunk_index, n_chunks=n_chunks, n_ctx_chunks=n_ctx_chunks, rev=rev)
    full = lambda shape: pl.BlockSpec(shape, lambda b, j: (0,) * len(shape))
    return pl.pallas_call(
        functools.partial(_s5_scan_kernel, rev=rev),
        out_shape=jax.ShapeDtypeStruct((BATCH, N_TOK, BRANCH_W), F32),
        grid=(BATCH, n_chunks),
        in_specs=[
            pl.BlockSpec((None, S5_CHUNK, BRANCH_W), lambda b, j: (b, cidx(j), COL_S5 // BRANCH_W)),
            full(bbre_t.shape), full(bbim_t.shape), full(cre_t.shape), full(cim_t.shape), full(tab.shape),
        ],
        out_specs=pl.BlockSpec((None, S5_CHUNK, BRANCH_W), lambda b, j: (b, cidx(j), 0)),
        scratch_shapes=[pltpu.VMEM((S5_CHUNK, S5_CH), F32), pltpu.VMEM((S5_CHUNK, S5_CH), F32),
                        pltpu.VMEM((8, S5_CH), F32)],
        name="s5_scan_bwd" if rev else "s5_scan_fwd",
    )(proj, bbre_t, bbim_t, cre_t, cim_t, tab)


def _s5_out_kernel(u_ref, yf_ref, yb_ref, d_ref, w_ref, b_ref, o_ref):
    y = d_ref[...] * u_ref[...] + yf_ref[...] + yb_ref[...]
    y = jax.nn.gelu(y)
    z = _dot(y.astype(BF16), w_ref[...]) + b_ref[...]
    o_ref[...] = (y * jax.nn.sigmoid(z)).astype(o_ref.dtype)


def _s5_out(proj2d, y_f, y_b, d_skip, w_glu, b_glu):
    tm = 544
    tok = lambda cb: pl.BlockSpec((tm, BRANCH_W), lambda i: (i, cb))
    vec = pl.BlockSpec((1, BRANCH_W), lambda i: (0, 0))
    return pl.pallas_call(
        _s5_out_kernel,
        out_shape=jax.ShapeDtypeStruct((T_ALL, BRANCH_W), BF16),
        grid=(T_ALL // tm,),
        in_specs=[tok(COL_S5 // BRANCH_W), tok(0), tok(0), vec,
                  pl.BlockSpec((BRANCH_W, BRANCH_W), lambda i: (0, 0)), vec],
        out_specs=tok(0),
        name="s5_out",
    )(proj2d, y_f, y_b, d_skip.reshape(1, BRANCH_W), w_glu, b_glu.reshape(1, BRANCH_W))


def _mlstm_kernel(q_ref, k_ref, v_ref, g_ref, bias_ref, h_ref, c_ref, n_ref, m_ref, *, rev):
    j = pl.program_id(1)

    @pl.when(j == 0)
    def _reset():
        c_ref[...] = jnp.zeros_like(c_ref)
        n_ref[...] = jnp.zeros_like(n_ref)
        m_ref[...] = jnp.zeros_like(m_ref)

    L = ML_CHUNK
    t_idx = lax.broadcasted_iota(jnp.int32, (L, L), 0)
    s_idx = lax.broadcasted_iota(jnp.int32, (L, L), 1)
    tri = (s_idx >= t_idx) if rev else (s_idx <= t_idx)
    tri_bf = jnp.where(tri, 1.0, 0.0).astype(BF16)

    gates = g_ref[...] + bias_ref[...]
    log_f = jax.nn.log_sigmoid(gates)
    hi = log_f.astype(BF16)
    r1 = log_f - hi.astype(F32)
    mid = r1.astype(BF16)
    lo = (r1 - mid.astype(F32)).astype(BF16)
    bcum = _dot(tri_bf, hi) + _dot(tri_bf, mid) + _dot(tri_bf, lo)
    gates_t = gates.T
    bcum_t = bcum.T
    last = 0 if rev else L - 1
    d = 1 if rev else 0

    for h in range(ML_HEADS):
        ii = (2 * d) * ML_HEADS + h
        fi = (2 * d + 1) * ML_HEADS + h
        li_c = gates[:, ii:ii + 1]
        li_r = gates_t[ii:ii + 1, :]
        bc_c = bcum[:, fi:fi + 1]
        bc_r = bcum_t[fi:fi + 1, :]
        b_last = bcum_t[fi:fi + 1, last:last + 1]
        m_prev = m_ref[h][0:1, 0:1]
        c_prev = c_ref[h]
        n_prev = n_ref[h][0:1, :]
        cols = slice(h * HEAD_DIM, (h + 1) * HEAD_DIM)
        q = q_ref[:, cols]
        k = k_ref[:, cols] * ATT_SCALE
        v = v_ref[:, cols]
        qb, kb, vb = q.astype(BF16), k.astype(BF16), v.astype(BF16)

        log_end = b_last - bc_c + li_c
        m_new = jnp.maximum(b_last + m_prev, jnp.max(log_end, axis=0, keepdims=True))
        w_end = jnp.exp(log_end - m_new)
        decay = jnp.exp(b_last + m_prev - m_new)

        log_w = jnp.where(tri, bc_c - bc_r + li_r, NEG_INF)
        log_inter = bc_c + m_prev
        m_t = jnp.maximum(log_inter, jnp.max(log_w, axis=-1, keepdims=True))
        w = jnp.exp(log_w - m_t)
        inter = jnp.exp(log_inter - m_t)
        s = _dot_nt(qb, kb) * w
        num = inter * _dot_nt(qb, c_prev.astype(BF16)) + _dot(s.astype(BF16), vb)
        den = inter * jnp.sum(q * n_prev, axis=-1, keepdims=True) + jnp.sum(s, axis=-1, keepdims=True)
        h_ref[:, cols] = num / jnp.maximum(jnp.abs(den), jnp.exp(-m_t))

        vw_t = (v * w_end).T.astype(BF16)
        c_ref[h] = decay * c_prev + _dot(vw_t, kb)
        n_new = decay * n_prev + jnp.sum(k * w_end, axis=0, keepdims=True)
        n_ref[h] = jnp.broadcast_to(n_new, (8, HEAD_DIM))
        m_ref[h] = jnp.broadcast_to(m_new, (8, HEAD_DIM))


def _mlstm(proj, mlg, gate_bias, rev):
    n_chunks = N_TOK // ML_CHUNK
    n_ctx_chunks = CTX_LEN // ML_CHUNK
    cidx = functools.partial(_chunk_index, n_chunks=n_chunks, n_ctx_chunks=n_ctx_chunks, rev=rev)
    w = ML_HEADS * HEAD_DIM
    tok = lambda cb: pl.BlockSpec((None, ML_CHUNK, w), lambda b, j: (b, cidx(j), cb))
    return pl.pallas_call(
        functools.partial(_mlstm_kernel, rev=rev),
        out_shape=jax.ShapeDtypeStruct((BATCH, N_TOK, w), F32),
        grid=(BATCH, n_chunks),
        in_specs=[
            tok(COL_MLQ // w), tok(COL_MLK // w), tok(COL_MLV // w),
            pl.BlockSpec((None, ML_CHUNK, 128), lambda b, j: (b, cidx(j), 0)),
            pl.BlockSpec((1, 128), lambda b, j: (0, 0)),
        ],
        out_specs=tok(0),
        scratch_shapes=[pltpu.VMEM((ML_HEADS, HEAD_DIM, HEAD_DIM), F32), pltpu.VMEM((ML_HEADS, 8, HEAD_DIM), F32),
                        pltpu.VMEM((ML_HEADS, 8, HEAD_DIM), F32)],
        name="mlstm_bwd" if rev else "mlstm_fwd",
    )(proj, proj, proj, mlg, gate_bias)


def _ml_out_kernel(hf_ref, hb_ref, o_ref, nrm_ref, out_ref):
    for h in range(ML_HEADS):
        cols = slice(h * HEAD_DIM, (h + 1) * HEAD_DIM)
        x = hf_ref[:, cols] + hb_ref[:, cols]
        y = x * lax.rsqrt(jnp.mean(x * x, axis=-1, keepdims=True) + EPS) * nrm_ref[:, cols]
        out_ref[:, cols] = (y * jax.nn.sigmoid(o_ref[:, cols])).astype(out_ref.dtype)


def _ml_out(h_f, h_b, proj2d, ml_norm):
    tm = 544
    w = ML_HEADS * HEAD_DIM
    tok = lambda cb: pl.BlockSpec((tm, w), lambda i: (i, cb))
    return pl.pallas_call(
        _ml_out_kernel,
        out_shape=jax.ShapeDtypeStruct((T_ALL, w), BF16),
        grid=(T_ALL // tm,),
        in_specs=[tok(0), tok(0), tok(COL_MLO // w), pl.BlockSpec((1, w), lambda i: (0, 0))],
        out_specs=tok(0),
        name="ml_out",
    )(h_f, h_b, proj2d, ml_norm.reshape(1, w))


def _merge_kernel(b0_ref, b1_ref, b2_ref, b3_ref, w_ref, g0_ref, g1_ref, g2_ref, g3_ref, z_ref):
    acc = g0_ref[...] * _dot(b0_ref[...], w_ref[0])
    acc += g1_ref[...] * _dot(b1_ref[...], w_ref[1])
    acc += g2_ref[...] * _dot(b2_ref[...], w_ref[2])
    acc += g3_ref[...] * _dot(b3_ref[...], w_ref[3])
    z_ref[...] = acc.astype(z_ref.dtype)


def _merge(branches, w_branch, gates):
    tm, tn = 1088, 512
    nj = D_MODEL // tn
    br = pl.BlockSpec((tm, BRANCH_W), lambda i, j: (i, 0))
    gate = lambda g: pl.BlockSpec((tm, tn), lambda i, j: (i, g * nj + j))
    return pl.pallas_call(
        _merge_kernel,
        out_shape=jax.ShapeDtypeStruct((T_ALL, D_MODEL), BF16),
        grid=(T_ALL // tm, nj),
        in_specs=[br, br, br, br, pl.BlockSpec((N_BRANCH, BRANCH_W, tn), lambda i, j: (0, 0, j)),
                  gate(0), gate(1), gate(2), gate(3)],
        out_specs=pl.BlockSpec((tm, tn), lambda i, j: (i, j)),
        compiler_params=_cparams(40),
        name="merge",
    )(*branches, w_branch, gates, gates, gates, gates)


def _mm_norm_res_kernel(a_ref, w_ref, x_ref, g_ref, gt_ref, o_ref, *, nk, tm):
    i = pl.program_id(0)
    kk = pl.program_id(1)
    p = _dot(a_ref[...], w_ref[...])

    @pl.when(kk == 0)
    def _first():
        o_ref[...] = p

    @pl.when(kk > 0)
    def _accumulate():
        o_ref[...] += p

    @pl.when(kk == nk - 1)
    def _finish():
        y = o_ref[...]
        yn = y * lax.rsqrt(jnp.mean(y * y, axis=-1, keepdims=True) + EPS) * g_ref[...]
        o_ref[...] = x_ref[...] + _pick_rows(gt_ref, i, tm) * yn


def _matmul_norm_residual(a, w, x, g, mod_l, gate_chunk):
    tm, tk = 544, 512
    k_dim = a.shape[1]
    nk = k_dim // tk
    return pl.pallas_call(
        functools.partial(_mm_norm_res_kernel, nk=nk, tm=tm),
        out_shape=jax.ShapeDtypeStruct((T_ALL, D_MODEL), F32),
        grid=(T_ALL // tm, nk),
        in_specs=[
            pl.BlockSpec((tm, tk), lambda i, k: (i, k)),
            pl.BlockSpec((tk, D_MODEL), lambda i, k: (k, 0)),
            pl.BlockSpec((tm, D_MODEL), lambda i, k: (i, 0)),
            pl.BlockSpec((1, D_MODEL), lambda i, k: (0, 0)),
            pl.BlockSpec((8, D_MODEL), lambda i, k: (0, gate_chunk)),
        ],
        out_specs=pl.BlockSpec((tm, D_MODEL), lambda i, k: (i, 0)),
        compiler_params=_cparams(40),
        name="matmul_norm_res",
    )(a, w, x, g.reshape(1, D_MODEL), mod_l)


FFN_TM = 1088
FFN_TF = 512
FFN_HALO = 16


def _ffn_up_kernel(hp_ref, hm_ref, hn_ref, wa_ref, wg_ref, cwa_ref, cwg_ref, cba_ref, cbg_ref, o_ref,
                   hext_ref, ua_ref, ug_ref):
    i = pl.program_id(0)
    j = pl.program_id(1)
    tm, halo = FFN_TM, FFN_HALO

    @pl.when(j == 0)
    def _assemble_rows():
        hext_ref[0:halo, :] = hp_ref[...]
        hext_ref[halo:halo + tm, :] = hm_ref[...]
        hext_ref[halo + tm:halo + tm + halo, :] = hn_ref[...]

    hext = hext_ref[...]
    ua_ref[...] = _dot(hext, wa_ref[...])
    ug_ref[...] = _dot(hext, wg_ref[...])

    pos = (i * tm) % N_TOK + lax.broadcasted_iota(jnp.int32, (tm, 1), 0)
    has_prev = jnp.where((pos != 0) & (pos != CTX_LEN), 1.0, 0.0)
    has_next = jnp.where((pos != CTX_LEN - 1) & (pos != N_TOK - 1), 1.0, 0.0)

    def conv(u_ref, cw_ref, cb_ref):
        prev = u_ref[pl.ds(halo - 1, tm), :] * has_prev
        mid = u_ref[pl.ds(halo, tm), :]
        nxt = u_ref[pl.ds(halo + 1, tm), :] * has_next
        return prev * cw_ref[0:1, :] + mid * cw_ref[1:2, :] + nxt * cw_ref[2:3, :] + cb_ref[...]

    a = conv(ua_ref, cwa_ref, cba_ref)
    g = conv(ug_ref, cwg_ref, cbg_ref)
    o_ref[...] = (a * (g * jax.nn.sigmoid(g))).astype(o_ref.dtype)


def _ffn_up(h, w_up, conv_w, conv_b):
    tm, tf, halo = FFN_TM, FFN_TF, FFN_HALO
    nf = FFN_DIM // tf
    hb = tm // halo
    n_halo_blocks = T_ALL // halo
    conv_b = conv_b.reshape(1, 2 * FFN_DIM)
    return pl.pallas_call(
        _ffn_up_kernel,
        out_shape=jax.ShapeDtypeStruct((T_ALL, FFN_DIM), BF16),
        grid=(T_ALL // tm, nf),
        in_specs=[
            pl.BlockSpec((halo, D_MODEL), lambda i, j: (jnp.maximum(i * hb - 1, 0), 0)),
            pl.BlockSpec((tm, D_MODEL), lambda i, j: (i, 0)),
            pl.BlockSpec((halo, D_MODEL), lambda i, j: (jnp.minimum((i + 1) * hb, n_halo_blocks - 1), 0)),
            pl.BlockSpec((D_MODEL, tf), lambda i, j: (0, j)),
            pl.BlockSpec((D_MODEL, tf), lambda i, j: (0, nf + j)),
            pl.BlockSpec((3, tf), lambda i, j: (0, j)),
            pl.BlockSpec((3, tf), lambda i, j: (0, nf + j)),
            pl.BlockSpec((1, tf), lambda i, j: (0, j)),
            pl.BlockSpec((1, tf), lambda i, j: (0, nf + j)),
        ],
        out_specs=pl.BlockSpec((tm, tf), lambda i, j: (i, j)),
        scratch_shapes=[pltpu.VMEM((tm + 2 * halo, D_MODEL), BF16), pltpu.VMEM((tm + 2 * halo, tf), F32),
                        pltpu.VMEM((tm + 2 * halo, tf), F32)],
        compiler_params=_cparams(48),
        name="ffn_up",
    )(h, h, h, w_up, w_up, conv_w, conv_w, conv_b, conv_b)


def kernel(x, c, ctx, c_ctx, w_ada, b_ada, g_mix_pre, g_mix_post, g_ffn_pre, g_ffn_post, w_in, na_rpb, wg_sink,
           s5_lam_re, s5_lam_im, s5_log_step, s5_b_re, s5_b_im, s5_c_re, s5_c_im, s5_d, s5_w_glu, s5_b_glu,
           ml_gate_bias, ml_norm, w_branch, w_out, w_up, ffn_conv_w, ffn_conv_b, w_down):
    assert x.shape == (BATCH, SEQ, D_MODEL) and ctx.shape == (BATCH, CTX_LEN, D_MODEL)
    xs = jnp.concatenate([ctx, x], axis=1).reshape(T_ALL, D_MODEL)
    cs = jnp.concatenate([c, c_ctx[None, :], jnp.zeros((8 - BATCH - 1, D_MODEL), F32)], axis=0)
    mod = _ada(cs, w_ada, b_ada)
    cos_t, sin_t = _rope_tables()

    for l in range(DEPTH):
        mod_l = mod[l]
        w_main = w_in[l, :, :PROJ_W].astype(BF16)
        w_mlg = jnp.pad(w_in[l, :, COL_MLG:COL_GATE], ((0, 0), (0, 128 - 4 * ML_HEADS))).astype(BF16)
        w_gate = w_in[l, :, COL_GATE:].astype(BF16)

        h = _norm_mod(xs, g_mix_pre[l], mod_l, 0, 1)
        proj2d = _matmul(h, w_main, 0, PROJ_W, 512, F32)
        mlg = _matmul(h, w_mlg, 0, 128, 128, F32)
        gates = _matmul(h, w_gate, 0, GATE_W, 512, F32, act="sigmoid")
        proj = proj2d.reshape(BATCH, N_TOK, PROJ_W)
        mlg3 = mlg.reshape(BATCH, N_TOK, 128)

        br_na = _neighbourhood_attention(proj, na_rpb[l]).reshape(T_ALL, BRANCH_W)
        br_wg = _windowed_gqa(proj, wg_sink[l], cos_t, sin_t).reshape(T_ALL, BRANCH_W)

        p_re, p_im, bb_re, bb_im = _s5_discretise(s5_lam_re[l], s5_lam_im[l], s5_log_step[l], s5_b_re[l], s5_b_im[l])
        ys = []
        for dr in range(2):
            tabs = _s5_tables(p_re[dr], p_im[dr], bb_re[dr], bb_im[dr], s5_c_re[l, dr], s5_c_im[l, dr], rev=dr == 1)
            tab, bbre_t, bbim_t, cre_t, cim_t = tabs
            ys.append(_s5_scan(proj, tab, bbre_t, bbim_t, cre_t, cim_t, rev=dr == 1).reshape(T_ALL, BRANCH_W))
        br_s5 = _s5_out(proj2d, ys[0], ys[1], s5_d[l], s5_w_glu[l].astype(BF16), s5_b_glu[l])

        gate_bias = jnp.pad(ml_gate_bias[l].reshape(1, 4 * ML_HEADS), ((0, 0), (0, 128 - 4 * ML_HEADS)))
        h_f = _mlstm(proj, mlg3, gate_bias, rev=False).reshape(T_ALL, ML_HEADS * HEAD_DIM)
        h_b = _mlstm(proj, mlg3, gate_bias, rev=True).reshape(T_ALL, ML_HEADS * HEAD_DIM)
        br_ml = _ml_out(h_f, h_b, proj2d, ml_norm[l])

        z = _merge((br_na, br_s5, br_wg, br_ml), w_branch[l].astype(BF16), gates)
        xs = _matmul_norm_residual(z, w_out[l].astype(BF16), xs, g_mix_post[l], mod_l, 2)

        h2 = _norm_mod(xs, g_ffn_pre[l], mod_l, 3, 4)
        act = _ffn_up(h2, w_up[l].astype(BF16), ffn_conv_w[l], ffn_conv_b[l])
        xs = _matmul_norm_residual(act, w_down[l].astype(BF16), xs, g_ffn_post[l], mod_l, 5)

    return xs.reshape(BATCH, N_TOK, D_MODEL)[:, CTX_LEN:, :]
```

```python
import functools

import jax
import jax.numpy as jnp
from jax import lax
from jax.experimental import pallas as pl
from jax.experimental.pallas import tpu as pltpu

F32 = jnp.float32
BF16 = jnp.bfloat16

D_MODEL = 2048
BATCH = 2
SEQ = 4096
DEPTH = 2
GRID_W = 64
CTX_LEN = 256
HEAD_DIM = 128
BRANCH_W = 512
N_BRANCH = 4
NA_HEADS = 4
NA_WIN_ROWS = 8
NA_WIN_COLS = 16
S5_GROUP = 16
S5_GROUPS = BRANCH_W // S5_GROUP
S5_STATE = 64
S5_CH = S5_GROUPS * S5_STATE
WG_Q_HEADS = 4
WG_KV_HEADS = 2
WG_WINDOW = 128
WG_BLOCK = 128
ML_HEADS = 4
ML_CHUNK = 128
FFN_DIM = 5632
ROPE_BASE = 10000.0
EPS = 1e-6
NEG_INF = -1e30

T_LAT = BATCH * SEQ
T_CTX = BATCH * CTX_LEN
T_ALL = T_LAT + T_CTX
GRID_ROWS = SEQ // GRID_W
ATT_SCALE = HEAD_DIM ** -0.5

COL_NAQ, COL_NAK, COL_NAV, COL_S5 = 0, 512, 1024, 1536
COL_WGQ, COL_WGK, COL_WGV = 2048, 2560, 2816
COL_MLQ, COL_MLK, COL_MLV, COL_MLO = 3072, 3584, 4096, 4608
PROJ_W = 5120
COL_MLG = PROJ_W
COL_GATE = PROJ_W + 4 * ML_HEADS
GATE_W = N_BRANCH * D_MODEL

MIB = 1024 * 1024
TOK_TM = 512
N_LAT_TILES = T_LAT // TOK_TM
N_ALL_TILES = T_ALL // TOK_TM
ATT_QBLK = 256
NA_ROWS_PER_STEP = ATT_QBLK // GRID_W
S5_CHUNK = 256
S5_SUB = 8


def _cparams(vmem_mib=None):
    if vmem_mib is None:
        return None
    return pltpu.CompilerParams(vmem_limit_bytes=vmem_mib * MIB)


def _dot(a, b):
    return jnp.dot(a, b, preferred_element_type=F32)


def _dot_nt(a, b):
    return lax.dot_general(a, b, (((1,), (1,)), ((), ())), preferred_element_type=F32)


def _mod_row(i):
    return jnp.where(i >= N_LAT_TILES, BATCH, i // (N_LAT_TILES // BATCH))


def _mod_spec(layer, chunk):
    return pl.BlockSpec((None, None, 1, D_MODEL), lambda i, *_: (layer, _mod_row(i), 0, chunk))


def _seq_block(b, j, blk, rev):
    n_ctx = CTX_LEN // blk
    n_lat = SEQ // blk
    if rev:
        ctx = T_LAT // blk + b * n_ctx + (n_ctx - 1 - j)
        lat = b * n_lat + (n_lat - 1 - (j - n_ctx))
    else:
        ctx = T_LAT // blk + b * n_ctx + j
        lat = b * n_lat + (j - n_ctx)
    return jnp.where(j < n_ctx, ctx, lat)


def _ada_kernel(c_ref, w_ref, b_ref, o_ref):
    c = c_ref[...]
    s = (c * jax.nn.sigmoid(c)).astype(BF16)
    o_ref[...] = _dot(s, w_ref[...].astype(BF16)) + b_ref[...]


def _ada(cs, w_ada, b_ada):
    tn = 1024
    n_out = 6 * D_MODEL
    return pl.pallas_call(
        _ada_kernel,
        out_shape=jax.ShapeDtypeStruct((DEPTH, 8, n_out), F32),
        grid=(DEPTH, n_out // tn),
        in_specs=[
            pl.BlockSpec((8, D_MODEL), lambda l, j: (0, 0)),
            pl.BlockSpec((None, D_MODEL, tn), lambda l, j: (l, 0, j)),
            pl.BlockSpec((None, 1, tn), lambda l, j: (l, 0, j)),
        ],
        out_specs=pl.BlockSpec((None, 8, tn), lambda l, j: (l, 0, j)),
        compiler_params=_cparams(40),
        name="ada",
    )(cs, w_ada, b_ada.reshape(DEPTH, 1, n_out))


def _x_specs(x_pair, n_tiles):
    if isinstance(x_pair, tuple):
        lat, ctx = x_pair
        last = N_LAT_TILES - 1
        specs = [pl.BlockSpec((TOK_TM, D_MODEL), lambda i, *_: (jnp.minimum(i, last), 0)),
                 pl.BlockSpec((TOK_TM, D_MODEL), lambda i, *_: (0, 0))]
        return specs, [lat, ctx]
    return [pl.BlockSpec((TOK_TM, D_MODEL), lambda i, *_: (i, 0))], [x_pair]


def _for_tile_source(x_refs, fn):
    if len(x_refs) == 1:
        fn(x_refs[0])
        return
    i = pl.program_id(0)
    pl.when(i < N_LAT_TILES)(lambda: fn(x_refs[0]))
    pl.when(i >= N_LAT_TILES)(lambda: fn(x_refs[1]))


def _norm_mod_kernel(*refs):
    *x_refs, g_ref, sh_ref, sc_ref, o_ref = refs

    def body(x_ref):
        x = x_ref[...]
        y = x * lax.rsqrt(jnp.mean(x * x, axis=-1, keepdims=True) + EPS) * g_ref[...]
        o_ref[...] = (y * (1.0 + sc_ref[...]) + sh_ref[...]).astype(BF16)

    _for_tile_source(x_refs, body)


def _norm_mod(x_pair, g, mod4, layer, shift_chunk, scale_chunk, n_tiles):
    x_specs, x_args = _x_specs(x_pair, n_tiles)
    return pl.pallas_call(
        _norm_mod_kernel,
        out_shape=jax.ShapeDtypeStruct((n_tiles * TOK_TM, D_MODEL), BF16),
        grid=(n_tiles,),
        in_specs=x_specs + [pl.BlockSpec((1, D_MODEL), lambda i: (0, 0)),
                            _mod_spec(layer, shift_chunk), _mod_spec(layer, scale_chunk)],
        out_specs=pl.BlockSpec((TOK_TM, D_MODEL), lambda i: (i, 0)),
        name="norm_mod",
    )(*x_args, g.reshape(1, D_MODEL), mod4, mod4)


def _mm_kernel(a_ref, w_ref, o_ref, *, act):
    r = _dot(a_ref[...], w_ref[...].astype(BF16))
    if act == "sigmoid":
        r = jax.nn.sigmoid(r)
    o_ref[...] = r.astype(o_ref.dtype)


def _matmul(a, w, n_cols, tn, out_dtype, *, tm, rows, layer=None, act=None):
    k = a.shape[1]
    if layer is None:
        w_spec = pl.BlockSpec((k, tn), lambda i, j: (0, j))
    else:
        w_spec = pl.BlockSpec((None, k, tn), lambda i, j: (layer, 0, j))
    return pl.pallas_call(
        functools.partial(_mm_kernel, act=act),
        out_shape=jax.ShapeDtypeStruct((rows, n_cols), out_dtype),
        grid=(rows // tm, n_cols // tn),
        in_specs=[pl.BlockSpec((tm, k), lambda i, j: (i, 0)), w_spec],
        out_specs=pl.BlockSpec((tm, tn), lambda i, j: (i, j)),
        compiler_params=_cparams(52),
        name="matmul",
    )(a, w)


def _att_row_block(b, s):
    n_lat = SEQ // ATT_QBLK
    return jnp.where(s == 0, T_LAT // ATT_QBLK + b, b * n_lat + s - 1)


def _na_kernel(q_ref, kl_ref, vl_ref, kc_ref, vc_ref, tb_ref, o_ref):
    step = pl.program_id(2)
    kc = kc_ref[...].astype(BF16)
    vc = vc_ref[...].astype(BF16)

    @pl.when(step == 0)
    def _context_queries():
        q = q_ref[...].astype(BF16)
        s = _dot_nt(q, kc) * ATT_SCALE
        p = jnp.exp(s - jnp.max(s, axis=-1, keepdims=True))
        o = _dot(p.astype(BF16), vc) / jnp.sum(p, axis=-1, keepdims=True)
        o_ref[...] = o.astype(o_ref.dtype)

    @pl.when(step > 0)
    def _latent_queries():
        n_win = NA_WIN_ROWS * GRID_W
        for rl in range(NA_ROWS_PER_STEP):
            r = (step - 1) * NA_ROWS_PER_STEP + rl
            r0 = jnp.clip(r - NA_WIN_ROWS // 2, 0, GRID_ROWS - NA_WIN_ROWS)
            q = q_ref[rl * GRID_W:(rl + 1) * GRID_W, :].astype(BF16)
            start = pl.multiple_of(r0 * GRID_W, GRID_W)
            kw = kl_ref[pl.ds(start, n_win), :].astype(BF16)
            vw = vl_ref[pl.ds(start, n_win), :].astype(BF16)
            bias = tb_ref[r0 - r + NA_WIN_ROWS - 1]
            s_loc = jnp.where(bias > 0.5 * NEG_INF, _dot_nt(q, kw) * ATT_SCALE + bias, NEG_INF)
            s_ctx = _dot_nt(q, kc) * ATT_SCALE
            m = jnp.maximum(jnp.max(s_loc, axis=-1, keepdims=True), jnp.max(s_ctx, axis=-1, keepdims=True))
            p_loc = jnp.exp(s_loc - m)
            p_ctx = jnp.exp(s_ctx - m)
            den = jnp.sum(p_loc, axis=-1, keepdims=True) + jnp.sum(p_ctx, axis=-1, keepdims=True)
            o = (_dot(p_loc.astype(BF16), vw) + _dot(p_ctx.astype(BF16), vc)) / den
            o_ref[rl * GRID_W:(rl + 1) * GRID_W, :] = o.astype(o_ref.dtype)


def _na_bias_table(rpb):
    col = jnp.arange(GRID_W)
    c0 = jnp.clip(col - NA_WIN_COLS // 2, 0, GRID_W - NA_WIN_COLS)
    col_ok = (col[None, :] >= c0[:, None]) & (col[None, :] < c0[:, None] + NA_WIN_COLS)
    dc = jnp.clip(col[None, :] - col[:, None] + NA_WIN_COLS - 1, 0, 2 * NA_WIN_COLS - 2)
    per_dr = jnp.where(col_ok[None, None], rpb[:, :, dc].astype(F32), NEG_INF)
    dr = jnp.arange(NA_WIN_ROWS)[:, None] + jnp.arange(NA_WIN_ROWS)[None, :]
    tb = per_dr[:, dr]
    return jnp.moveaxis(tb, 2, 3).reshape(NA_HEADS, NA_WIN_ROWS, GRID_W, NA_WIN_ROWS * GRID_W)


def _neighbourhood_attention(proj, rpb):
    tb = _na_bias_table(rpb)
    n_steps = 1 + SEQ // ATT_QBLK
    hb = HEAD_DIM
    ctx_blk = T_LAT // CTX_LEN
    return pl.pallas_call(
        _na_kernel,
        out_shape=jax.ShapeDtypeStruct((T_ALL, BRANCH_W), BF16),
        grid=(BATCH, NA_HEADS, n_steps),
        in_specs=[
            pl.BlockSpec((ATT_QBLK, hb), lambda b, h, s: (_att_row_block(b, s), COL_NAQ // hb + h)),
            pl.BlockSpec((SEQ, hb), lambda b, h, s: (b, COL_NAK // hb + h)),
            pl.BlockSpec((SEQ, hb), lambda b, h, s: (b, COL_NAV // hb + h)),
            pl.BlockSpec((CTX_LEN, hb), lambda b, h, s: (ctx_blk + b, COL_NAK // hb + h)),
            pl.BlockSpec((CTX_LEN, hb), lambda b, h, s: (ctx_blk + b, COL_NAV // hb + h)),
            pl.BlockSpec((None, NA_WIN_ROWS, GRID_W, NA_WIN_ROWS * GRID_W), lambda b, h, s: (h, 0, 0, 0)),
        ],
        out_specs=pl.BlockSpec((ATT_QBLK, hb), lambda b, h, s: (_att_row_block(b, s), h)),
        name="na_attn",
    )(proj, proj, proj, proj, proj, tb)


def _rope(x, cos, sin_signed):
    lane = lax.broadcasted_iota(jnp.int32, x.shape, 1)
    partner = jnp.where(lane % 64 < 32, pltpu.roll(x, 96, 1), pltpu.roll(x, 32, 1))
    return x * cos + partner * sin_signed


def _wg_kernel(sink_ref, q_ref, kl_ref, vl_ref, kc_ref, vc_ref, cos_ref, sin_ref, o_ref, kr_ref):
    hk = pl.program_id(1)
    step = pl.program_id(2)
    kc = kc_ref[...].astype(BF16)
    vc = vc_ref[...].astype(BF16)
    group = WG_Q_HEADS // WG_KV_HEADS

    @pl.when(step == 0)
    def _context_queries():
        for g in range(group):
            sink = sink_ref[hk * group + g]
            q = q_ref[:, g * HEAD_DIM:(g + 1) * HEAD_DIM].astype(BF16)
            s = _dot_nt(q, kc) * ATT_SCALE
            m = jnp.maximum(jnp.max(s, axis=-1, keepdims=True), sink)
            p = jnp.exp(s - m)
            den = jnp.sum(p, axis=-1, keepdims=True) + jnp.exp(sink - m)
            o_ref[:, g * HEAD_DIM:(g + 1) * HEAD_DIM] = (_dot(p.astype(BF16), vc) / den).astype(o_ref.dtype)

    @pl.when(step == 1)
    def _rope_keys():
        kr_ref[...] = _rope(kl_ref[...], cos_ref[...], sin_ref[...]).astype(BF16)

    @pl.when(step > 0)
    def _latent_queries():
        nb = SEQ // WG_BLOCK
        qi = lax.broadcasted_iota(jnp.int32, (WG_BLOCK, WG_BLOCK), 0)
        kj = lax.broadcasted_iota(jnp.int32, (WG_BLOCK, WG_BLOCK), 1)
        for blk in range(ATT_QBLK // WG_BLOCK):
            n = (step - 1) * (ATT_QBLK // WG_BLOCK) + blk
            base = pl.multiple_of(n * WG_BLOCK, WG_BLOCK)
            prev = pl.multiple_of(jnp.maximum(n - 1, 0) * WG_BLOCK, WG_BLOCK)
            nxt = pl.multiple_of(jnp.minimum(n + 1, nb - 1) * WG_BLOCK, WG_BLOCK)
            cos_q = cos_ref[pl.ds(base, WG_BLOCK), :]
            sin_q = sin_ref[pl.ds(base, WG_BLOCK), :]
            k_prev = kr_ref[pl.ds(prev, WG_BLOCK), :]
            k_self = kr_ref[pl.ds(base, WG_BLOCK), :]
            k_next = kr_ref[pl.ds(nxt, WG_BLOCK), :]
            v_prev = vl_ref[pl.ds(prev, WG_BLOCK), :].astype(BF16)
            v_self = vl_ref[pl.ds(base, WG_BLOCK), :].astype(BF16)
            v_next = vl_ref[pl.ds(nxt, WG_BLOCK), :].astype(BF16)
            ok_prev = kj >= qi + jnp.where(n > 0, 0, WG_BLOCK)
            ok_next = kj + jnp.where(n < nb - 1, 0, WG_BLOCK) <= qi
            for g in range(group):
                sink = sink_ref[hk * group + g]
                rows = slice(blk * WG_BLOCK, (blk + 1) * WG_BLOCK)
                cols = slice(g * HEAD_DIM, (g + 1) * HEAD_DIM)
                q = _rope(q_ref[rows, cols], cos_q, sin_q).astype(BF16)
                s_prev = jnp.where(ok_prev, _dot_nt(q, k_prev) * ATT_SCALE, NEG_INF)
                s_self = _dot_nt(q, k_self) * ATT_SCALE
                s_next = jnp.where(ok_next, _dot_nt(q, k_next) * ATT_SCALE, NEG_INF)
                s_ctx = _dot_nt(q, kc) * ATT_SCALE
                m = jnp.maximum(
                    jnp.maximum(jnp.max(s_prev, axis=-1, keepdims=True), jnp.max(s_self, axis=-1, keepdims=True)),
                    jnp.maximum(jnp.max(s_next, axis=-1, keepdims=True), jnp.max(s_ctx, axis=-1, keepdims=True)))
                m = jnp.maximum(m, sink)
                p_prev = jnp.exp(s_prev - m)
                p_self = jnp.exp(s_self - m)
                p_next = jnp.exp(s_next - m)
                p_ctx = jnp.exp(s_ctx - m)
                den = (jnp.sum(p_prev, axis=-1, keepdims=True) + jnp.sum(p_self, axis=-1, keepdims=True)
                       + jnp.sum(p_next, axis=-1, keepdims=True) + jnp.sum(p_ctx, axis=-1, keepdims=True)
                       + jnp.exp(sink - m))
                o = (_dot(p_prev.astype(BF16), v_prev) + _dot(p_self.astype(BF16), v_self)
                     + _dot(p_next.astype(BF16), v_next) + _dot(p_ctx.astype(BF16), vc)) / den
                o_ref[rows, cols] = o.astype(o_ref.dtype)


def _rope_tables():
    t = jnp.arange(SEQ)
    pos = jnp.stack([t // GRID_W, t % GRID_W], axis=-1).astype(F32)
    n_freq = HEAD_DIM // 4
    inv_freq = ROPE_BASE ** (-jnp.arange(n_freq, dtype=F32) / n_freq)
    ang = pos[:, :, None] * inv_freq
    cos, sin = jnp.cos(ang), jnp.sin(ang)
    cos_t = jnp.concatenate([cos[:, 0], cos[:, 0], cos[:, 1], cos[:, 1]], axis=-1)
    sin_t = jnp.concatenate([-sin[:, 0], sin[:, 0], -sin[:, 1], sin[:, 1]], axis=-1)
    return cos_t, sin_t


def _windowed_gqa(proj, sink, cos_t, sin_t):
    n_steps = 1 + SEQ // ATT_QBLK
    qw = (WG_Q_HEADS // WG_KV_HEADS) * HEAD_DIM
    hb = HEAD_DIM
    ctx_blk = T_LAT // CTX_LEN
    return pl.pallas_call(
        _wg_kernel,
        out_shape=jax.ShapeDtypeStruct((T_ALL, BRANCH_W), BF16),
        grid=(BATCH, WG_KV_HEADS, n_steps),
        in_specs=[
            pl.BlockSpec(memory_space=pltpu.SMEM),
            pl.BlockSpec((ATT_QBLK, qw), lambda b, h, s: (_att_row_block(b, s), COL_WGQ // qw + h)),
            pl.BlockSpec((SEQ, hb), lambda b, h, s: (b, COL_WGK // hb + h)),
            pl.BlockSpec((SEQ, hb), lambda b, h, s: (b, COL_WGV // hb + h)),
            pl.BlockSpec((CTX_LEN, hb), lambda b, h, s: (ctx_blk + b, COL_WGK // hb + h)),
            pl.BlockSpec((CTX_LEN, hb), lambda b, h, s: (ctx_blk + b, COL_WGV // hb + h)),
            pl.BlockSpec((SEQ, hb), lambda b, h, s: (0, 0)),
            pl.BlockSpec((SEQ, hb), lambda b, h, s: (0, 0)),
        ],
        out_specs=pl.BlockSpec((ATT_QBLK, qw), lambda b, h, s: (_att_row_block(b, s), h)),
        scratch_shapes=[pltpu.VMEM((SEQ, hb), BF16)],
        name="wg_attn",
    )(sink, proj, proj, proj, proj, proj, cos_t, sin_t)


def _s5_disc_kernel(lre_ref, lim_ref, lstep_ref, bre_ref, bim_ref, pre_ref, pim_ref, bbre_ref, bbim_ref):
    lre = jnp.minimum(lre_ref[...], -1e-4)
    lim = lim_ref[...]
    step = jnp.exp(lstep_ref[...])
    kk = (lax.broadcasted_iota(jnp.int32, (S5_SUB, 1), 0) + 1).astype(F32)
    mag = jnp.exp(kk * (lre * step))
    ang = kk * (lim * step)
    p_re = mag * jnp.cos(ang)
    p_im = mag * jnp.sin(ang)
    pre_ref[...] = p_re
    pim_ref[...] = p_im
    a_re = p_re[0:1, :]
    a_im = p_im[0:1, :]
    den = lre * lre + lim * lim
    f_re = ((a_re - 1.0) * lre + a_im * lim) / den
    f_im = (a_im * lre - (a_re - 1.0) * lim) / den
    br = bre_ref[...]
    bi = bim_ref[...]
    bbre_ref[...] = f_re * br - f_im * bi
    bbim_ref[...] = f_re * bi + f_im * br


def _s5_discretise(lam_re, lam_im, log_step, b_re, b_im):
    lre = lam_re.reshape(2, 1, S5_CH)
    lim = lam_im.reshape(2, 1, S5_CH)
    lstep = jnp.repeat(log_step, S5_STATE, axis=-1).reshape(2, 1, S5_CH)
    br = jnp.transpose(b_re, (0, 3, 1, 2)).reshape(2, S5_GROUP, S5_CH)
    bi = jnp.transpose(b_im, (0, 3, 1, 2)).reshape(2, S5_GROUP, S5_CH)
    row = lambda n: pl.BlockSpec((None, n, S5_CH), lambda d: (d, 0, 0))
    return pl.pallas_call(
        _s5_disc_kernel,
        out_shape=(jax.ShapeDtypeStruct((2, S5_SUB, S5_CH), F32), jax.ShapeDtypeStruct((2, S5_SUB, S5_CH), F32),
                   jax.ShapeDtypeStruct((2, S5_GROUP, S5_CH), F32), jax.ShapeDtypeStruct((2, S5_GROUP, S5_CH), F32)),
        grid=(2,),
        in_specs=[row(1), row(1), row(1), row(S5_GROUP), row(S5_GROUP)],
        out_specs=(row(S5_SUB), row(S5_SUB), row(S5_GROUP), row(S5_GROUP)),
        name="s5_disc",
    )(lre, lim, lstep, br, bi)


def _s5_scan_kernel(u_ref, bbre_ref, bbim_ref, cre_ref, cim_ref, tab_ref, y_ref, sre_ref, sim_ref, car_ref, *, rev):
    j = pl.program_id(1)

    @pl.when(j == 0)
    def _reset():
        car_ref[...] = jnp.zeros_like(car_ref)

    u = u_ref[...].astype(BF16)
    n_col_tiles = S5_CH // 256
    for c in range(n_col_tiles):
        ub = u[:, 128 * (c // 2):128 * (c // 2) + 128]
        sre_ref[:, 256 * c:256 * (c + 1)] = _dot(ub, bbre_ref[c])
        sim_ref[:, 256 * c:256 * (c + 1)] = _dot(ub, bbim_ref[c])

    n_groups = S5_CHUNK // S5_SUB
    last = 0 if rev else S5_SUB - 1

    def group(gi, carry):
        cr, ci = carry
        g = (n_groups - 1 - gi) if rev else gi
        r0 = pl.multiple_of(g * S5_SUB, S5_SUB)
        xr = sre_ref[pl.ds(r0, S5_SUB), :]
        xi = sim_ref[pl.ds(r0, S5_SUB), :]
        for t, k in enumerate((1, 2, 4)):
            shift = (S5_SUB - k) if rev else k
            ar = tab_ref[2 * t]
            ai = tab_ref[2 * t + 1]
            rr = pltpu.roll(xr, shift, 0)
            ri = pltpu.roll(xi, shift, 0)
            xr, xi = xr + ar * rr - ai * ri, xi + ar * ri + ai * rr
        apr = tab_ref[6]
        api = tab_ref[7]
        xr, xi = xr + apr * cr - api * ci, xi + apr * ci + api * cr
        sre_ref[pl.ds(r0, S5_SUB), :] = xr
        sim_ref[pl.ds(r0, S5_SUB), :] = xi
        return xr[last:last + 1, :], xi[last:last + 1, :]

    cr, ci = lax.fori_loop(0, n_groups, group, (car_ref[0:1, :], car_ref[1:2, :]))
    car_ref[0:1, :] = cr
    car_ref[1:2, :] = ci

    n_out_tiles = BRANCH_W // 128
    kw = S5_CH // n_out_tiles
    for oc in range(n_out_tiles):
        sr = sre_ref[:, kw * oc:kw * (oc + 1)].astype(BF16)
        si = sim_ref[:, kw * oc:kw * (oc + 1)].astype(BF16)
        y_ref[:, 128 * oc:128 * (oc + 1)] = _dot(sr, cre_ref[oc]) + _dot(si, cim_ref[oc])


def _s5_tables(p_re, p_im, bb_re, bb_im, c_re, c_im, rev):
    row = jnp.arange(S5_SUB)[:, None]
    tabs = []
    for k in (1, 2, 4):
        keep = (row <= S5_SUB - 1 - k) if rev else (row >= k)
        tabs += [jnp.where(keep, p_re[k - 1][None, :], 0.0), jnp.where(keep, p_im[k - 1][None, :], 0.0)]
    tabs += [p_re[::-1], p_im[::-1]] if rev else [p_re, p_im]
    tab = jnp.stack(tabs)
    eye = jnp.eye(S5_GROUPS, dtype=F32)

    def in_tiles(bb):
        bbg = bb.reshape(S5_GROUP, S5_GROUPS, S5_STATE)
        full = (eye[:, None, :, None] * jnp.transpose(bbg, (1, 0, 2))[:, :, None, :]).reshape(BRANCH_W, S5_CH)
        return jnp.stack([full[128 * (c // 2):128 * (c // 2) + 128, 256 * c:256 * (c + 1)]
                          for c in range(S5_CH // 256)]).astype(BF16)

    def out_tiles(cc):
        full = (eye[:, None, :, None] * jnp.transpose(cc, (0, 2, 1))[:, :, None, :]).reshape(S5_CH, BRANCH_W)
        kw = S5_CH // 4
        return jnp.stack([full[kw * oc:kw * (oc + 1), 128 * oc:128 * (oc + 1)] for oc in range(4)]).astype(BF16)

    return tab, in_tiles(bb_re), in_tiles(bb_im), out_tiles(c_re.astype(F32)), out_tiles(-c_im.astype(F32))


def _s5_scan(proj, tab, bbre_t, bbim_t, cre_t, cim_t, rev):
    n_chunks = (CTX_LEN + SEQ) // S5_CHUNK
    blk = functools.partial(_seq_block, blk=S5_CHUNK, rev=rev)
    full = lambda shape: pl.BlockSpec(shape, lambda b, j: (0,) * len(shape))
    return pl.pallas_call(
        functools.partial(_s5_scan_kernel, rev=rev),
        out_shape=jax.ShapeDtypeStruct((T_ALL, BRANCH_W), F32),
        grid=(BATCH, n_chunks),
        in_specs=[
            pl.BlockSpec((S5_CHUNK, BRANCH_W), lambda b, j: (blk(b, j), COL_S5 // BRANCH_W)),
            full(bbre_t.shape), full(bbim_t.shape), full(cre_t.shape), full(cim_t.shape), full(tab.shape),
        ],
        out_specs=pl.BlockSpec((S5_CHUNK, BRANCH_W), lambda b, j: (blk(b, j), 0)),
        scratch_shapes=[pltpu.VMEM((S5_CHUNK, S5_CH), F32), pltpu.VMEM((S5_CHUNK, S5_CH), F32),
                        pltpu.VMEM((8, S5_CH), F32)],
        name="s5_scan_bwd" if rev else "s5_scan_fwd",
    )(proj, bbre_t, bbim_t, cre_t, cim_t, tab)


def _s5_out_kernel(u_ref, yf_ref, yb_ref, d_ref, w_ref, b_ref, o_ref):
    y = d_ref[...] * u_ref[...] + yf_ref[...] + yb_ref[...]
    y = jax.nn.gelu(y)
    z = _dot(y.astype(BF16), w_ref[...]) + b_ref[...]
    o_ref[...] = (y * jax.nn.sigmoid(z)).astype(o_ref.dtype)


def _s5_out(proj, y_f, y_b, d_skip, w_glu, b_glu):
    tm = TOK_TM
    tok = lambda cb: pl.BlockSpec((tm, BRANCH_W), lambda i: (i, cb))
    vec = pl.BlockSpec((1, BRANCH_W), lambda i: (0, 0))
    return pl.pallas_call(
        _s5_out_kernel,
        out_shape=jax.ShapeDtypeStruct((T_ALL, BRANCH_W), BF16),
        grid=(T_ALL // tm,),
        in_specs=[tok(COL_S5 // BRANCH_W), tok(0), tok(0), vec,
                  pl.BlockSpec((BRANCH_W, BRANCH_W), lambda i: (0, 0)), vec],
        out_specs=tok(0),
        name="s5_out",
    )(proj, y_f, y_b, d_skip.reshape(1, BRANCH_W), w_glu, b_glu.reshape(1, BRANCH_W))


def _mlstm_kernel(q_ref, k_ref, v_ref, g_ref, bias_ref, h_ref, c_ref, n_ref, m_ref, *, rev):
    j = pl.program_id(1)

    @pl.when(j == 0)
    def _reset():
        c_ref[...] = jnp.zeros_like(c_ref)
        n_ref[...] = jnp.zeros_like(n_ref)
        m_ref[...] = jnp.zeros_like(m_ref)

    L = ML_CHUNK
    t_idx = lax.broadcasted_iota(jnp.int32, (L, L), 0)
    s_idx = lax.broadcasted_iota(jnp.int32, (L, L), 1)
    tri = (s_idx >= t_idx) if rev else (s_idx <= t_idx)
    tri_bf = jnp.where(tri, 1.0, 0.0).astype(BF16)

    gates = g_ref[...] + bias_ref[...]
    log_f = jax.nn.log_sigmoid(gates)
    hi = log_f.astype(BF16)
    r1 = log_f - hi.astype(F32)
    mid = r1.astype(BF16)
    lo = (r1 - mid.astype(F32)).astype(BF16)
    bcum = _dot(tri_bf, hi) + _dot(tri_bf, mid) + _dot(tri_bf, lo)
    gates_t = gates.T
    bcum_t = bcum.T
    last = 0 if rev else L - 1
    d = 1 if rev else 0

    for h in range(ML_HEADS):
        ii = (2 * d) * ML_HEADS + h
        fi = (2 * d + 1) * ML_HEADS + h
        li_c = gates[:, ii:ii + 1]
        li_r = gates_t[ii:ii + 1, :]
        bc_c = bcum[:, fi:fi + 1]
        bc_r = bcum_t[fi:fi + 1, :]
        b_last = bcum_t[fi:fi + 1, last:last + 1]
        m_prev = m_ref[h][0:1, 0:1]
        c_prev = c_ref[h]
        n_prev = n_ref[h][0:1, :]
        cols = slice(h * HEAD_DIM, (h + 1) * HEAD_DIM)
        q = q_ref[:, cols]
        k = k_ref[:, cols] * ATT_SCALE
        v = v_ref[:, cols]
        qb, kb, vb = q.astype(BF16), k.astype(BF16), v.astype(BF16)

        log_end = b_last - bc_c + li_c
        m_new = jnp.maximum(b_last + m_prev, jnp.max(log_end, axis=0, keepdims=True))
        w_end = jnp.exp(log_end - m_new)
        decay = jnp.exp(b_last + m_prev - m_new)

        log_w = jnp.where(tri, bc_c - bc_r + li_r, NEG_INF)
        log_inter = bc_c + m_prev
        m_t = jnp.maximum(log_inter, jnp.max(log_w, axis=-1, keepdims=True))
        w = jnp.exp(log_w - m_t)
        inter = jnp.exp(log_inter - m_t)
        s = _dot_nt(qb, kb) * w
        num = inter * _dot_nt(qb, c_prev.astype(BF16)) + _dot(s.astype(BF16), vb)
        den = inter * jnp.sum(q * n_prev, axis=-1, keepdims=True) + jnp.sum(s, axis=-1, keepdims=True)
        h_ref[:, cols] = num / jnp.maximum(jnp.abs(den), jnp.exp(-m_t))

        vw_t = (v * w_end).T.astype(BF16)
        c_ref[h] = decay * c_prev + _dot(vw_t, kb)
        n_new = decay * n_prev + jnp.sum(k * w_end, axis=0, keepdims=True)
        n_ref[h] = jnp.broadcast_to(n_new, (8, HEAD_DIM))
        m_ref[h] = jnp.broadcast_to(m_new, (8, HEAD_DIM))


def _mlstm(proj, mlg, gate_bias, rev):
    n_chunks = (CTX_LEN + SEQ) // ML_CHUNK
    blk = functools.partial(_seq_block, blk=ML_CHUNK, rev=rev)
    w = ML_HEADS * HEAD_DIM
    tok = lambda cb: pl.BlockSpec((ML_CHUNK, w), lambda b, j: (blk(b, j), cb))
    return pl.pallas_call(
        functools.partial(_mlstm_kernel, rev=rev),
        out_shape=jax.ShapeDtypeStruct((T_ALL, w), F32),
        grid=(BATCH, n_chunks),
        in_specs=[
            tok(COL_MLQ // w), tok(COL_MLK // w), tok(COL_MLV // w),
            pl.BlockSpec((ML_CHUNK, 128), lambda b, j: (blk(b, j), 0)),
            pl.BlockSpec((1, 128), lambda b, j: (0, 0)),
        ],
        out_specs=tok(0),
        scratch_shapes=[pltpu.VMEM((ML_HEADS, HEAD_DIM, HEAD_DIM), F32), pltpu.VMEM((ML_HEADS, 8, HEAD_DIM), F32),
                        pltpu.VMEM((ML_HEADS, 8, HEAD_DIM), F32)],
        name="mlstm_bwd" if rev else "mlstm_fwd",
    )(proj, proj, proj, mlg, gate_bias)


def _ml_out_kernel(hf_ref, hb_ref, o_ref, nrm_ref, out_ref):
    for h in range(ML_HEADS):
        cols = slice(h * HEAD_DIM, (h + 1) * HEAD_DIM)
        x = hf_ref[:, cols] + hb_ref[:, cols]
        y = x * lax.rsqrt(jnp.mean(x * x, axis=-1, keepdims=True) + EPS) * nrm_ref[:, cols]
        out_ref[:, cols] = (y * jax.nn.sigmoid(o_ref[:, cols])).astype(out_ref.dtype)


def _ml_out(h_f, h_b, proj, ml_norm):
    tm = TOK_TM
    w = ML_HEADS * HEAD_DIM
    tok = lambda cb: pl.BlockSpec((tm, w), lambda i: (i, cb))
    return pl.pallas_call(
        _ml_out_kernel,
        out_shape=jax.ShapeDtypeStruct((T_ALL, w), BF16),
        grid=(T_ALL // tm,),
        in_specs=[tok(0), tok(0), tok(COL_MLO // w), pl.BlockSpec((1, w), lambda i: (0, 0))],
        out_specs=tok(0),
        name="ml_out",
    )(h_f, h_b, proj, ml_norm.reshape(1, w))


def _merge_kernel(b0_ref, b1_ref, b2_ref, b3_ref, w_ref, g0_ref, g1_ref, g2_ref, g3_ref, z_ref):
    acc = g0_ref[...].astype(F32) * _dot(b0_ref[...], w_ref[0])
    acc += g1_ref[...].astype(F32) * _dot(b1_ref[...], w_ref[1])
    acc += g2_ref[...].astype(F32) * _dot(b2_ref[...], w_ref[2])
    acc += g3_ref[...].astype(F32) * _dot(b3_ref[...], w_ref[3])
    z_ref[...] = acc.astype(z_ref.dtype)


def _merge(branches, w_branch, gates, rows, tm):
    tn = 512
    nj = D_MODEL // tn
    br = pl.BlockSpec((tm, BRANCH_W), lambda i, j: (i, 0))
    gate = lambda g: pl.BlockSpec((tm, tn), lambda i, j: (i, g * nj + j))
    return pl.pallas_call(
        _merge_kernel,
        out_shape=jax.ShapeDtypeStruct((rows, D_MODEL), BF16),
        grid=(rows // tm, nj),
        in_specs=[br, br, br, br, pl.BlockSpec((N_BRANCH, BRANCH_W, tn), lambda i, j: (0, 0, j)),
                  gate(0), gate(1), gate(2), gate(3)],
        out_specs=pl.BlockSpec((tm, tn), lambda i, j: (i, j)),
        compiler_params=_cparams(40),
        name="merge",
    )(*branches, w_branch, gates, gates, gates, gates)


def _mm_norm_res_kernel(*refs, nj, tn):
    a_ref, w_ref, *x_refs, g_ref, gt_ref, o_ref, y_ref = refs
    j = pl.program_id(1)
    y_ref[j] = _dot(a_ref[...], w_ref[...])

    @pl.when(j == nj - 1)
    def _finish():
        ss = jnp.sum(y_ref[0] * y_ref[0], axis=-1, keepdims=True)
        for jj in range(1, nj):
            ss += jnp.sum(y_ref[jj] * y_ref[jj], axis=-1, keepdims=True)
        rs = lax.rsqrt(ss * (1.0 / D_MODEL) + EPS)

        def emit(x_ref):
            for jj in range(nj):
                cols = slice(jj * tn, (jj + 1) * tn)
                o_ref[:, cols] = x_ref[:, cols] + gt_ref[:, cols] * (y_ref[jj] * rs * g_ref[:, cols])

        _for_tile_source(x_refs, emit)


def _matmul_norm_residual(a, w, x_pair, g, mod4, layer, gate_chunk, n_tiles, tn):
    k_dim = a.shape[1]
    nj = D_MODEL // tn
    x_specs, x_args = _x_specs(x_pair, n_tiles)
    return pl.pallas_call(
        functools.partial(_mm_norm_res_kernel, nj=nj, tn=tn),
        out_shape=jax.ShapeDtypeStruct((n_tiles * TOK_TM, D_MODEL), F32),
        grid=(n_tiles, nj),
        in_specs=[pl.BlockSpec((TOK_TM, k_dim), lambda i, j: (i, 0)),
                  pl.BlockSpec((k_dim, tn), lambda i, j: (0, j))]
                 + x_specs
                 + [pl.BlockSpec((1, D_MODEL), lambda i, j: (0, 0)), _mod_spec(layer, gate_chunk)],
        out_specs=pl.BlockSpec((TOK_TM, D_MODEL), lambda i, j: (i, 0)),
        scratch_shapes=[pltpu.VMEM((nj, TOK_TM, tn), F32)],
        compiler_params=_cparams(52),
        name="matmul_norm_res",
    )(a, w, *x_args, g.reshape(1, D_MODEL), mod4)


FFN_TF = 512
FFN_HALO = 16
FFN_ROW_BLOCKS = 4
FFN_ALIGN = 16
FFN_LAG = 2 * FFN_ALIGN


def _ffn_up_kernel(hp_ref, hm_ref, hn_ref, wa_ref, wg_ref, cwa_ref, cwg_ref, cba_ref, cbg_ref, o_ref,
                   hext_ref, ua_ref, ug_ref, *, tm):
    i = pl.program_id(0)
    j = pl.program_id(1)
    halo = FFN_HALO
    ext = tm + 2 * halo
    rb = -(-ext // (FFN_ROW_BLOCKS * FFN_ALIGN)) * FFN_ALIGN

    @pl.when(j == 0)
    def _assemble_rows():
        hext_ref[0:halo, :] = hp_ref[...]
        hext_ref[halo:halo + tm, :] = hm_ref[...]
        hext_ref[halo + tm:ext, :] = hn_ref[...]

    wa = wa_ref[...].astype(BF16)
    wg = wg_ref[...].astype(BF16)

    def conv(u_ref, cw_ref, cb_ref, lo, n, has_prev, has_next):
        prev = u_ref[pl.ds(lo + halo - 1, n), :] * has_prev
        mid = u_ref[pl.ds(lo + halo, n), :]
        nxt = u_ref[pl.ds(lo + halo + 1, n), :] * has_next
        return prev * cw_ref[0:1, :] + mid * cw_ref[1:2, :] + nxt * cw_ref[2:3, :] + cb_ref[...]

    lo = 0
    for blk in range(FFN_ROW_BLOCKS):
        rows = slice(blk * rb, min((blk + 1) * rb, ext))
        hx = hext_ref[rows, :]
        ua_ref[rows, :] = _dot(hx, wa)
        ug_ref[rows, :] = _dot(hx, wg)
        hi = tm if blk == FFN_ROW_BLOCKS - 1 else (blk + 1) * rb - FFN_LAG
        n = hi - lo
        row = i * tm + lo + lax.broadcasted_iota(jnp.int32, (n, 1), 0)
        first = (row == 0) | (row == SEQ) | (row == T_LAT) | (row == T_LAT + CTX_LEN)
        final = (row == SEQ - 1) | (row == T_LAT - 1) | (row == T_LAT + CTX_LEN - 1) | (row == T_ALL - 1)
        has_prev = jnp.where(first, 0.0, 1.0)
        has_next = jnp.where(final, 0.0, 1.0)
        a = conv(ua_ref, cwa_ref, cba_ref, lo, n, has_prev, has_next)
        g = conv(ug_ref, cwg_ref, cbg_ref, lo, n, has_prev, has_next)
        o_ref[lo:hi, :] = (a * (g * jax.nn.sigmoid(g))).astype(o_ref.dtype)
        lo = hi


def _ffn_up(h, w_up, layer, conv_w, conv_b, rows, tm):
    tf, halo = FFN_TF, FFN_HALO
    nf = FFN_DIM // tf
    hb = tm // halo
    n_halo_blocks = rows // halo
    ext = tm + 2 * halo
    assert ext % FFN_ALIGN == 0 and rows % tm == 0
    conv_b = conv_b.reshape(1, 2 * FFN_DIM)
    return pl.pallas_call(
        functools.partial(_ffn_up_kernel, tm=tm),
        out_shape=jax.ShapeDtypeStruct((rows, FFN_DIM), BF16),
        grid=(rows // tm, nf),
        in_specs=[
            pl.BlockSpec((halo, D_MODEL), lambda i, j: (jnp.maximum(i * hb - 1, 0), 0)),
            pl.BlockSpec((tm, D_MODEL), lambda i, j: (i, 0)),
            pl.BlockSpec((halo, D_MODEL), lambda i, j: (jnp.minimum((i + 1) * hb, n_halo_blocks - 1), 0)),
            pl.BlockSpec((None, D_MODEL, tf), lambda i, j: (layer, 0, j)),
            pl.BlockSpec((None, D_MODEL, tf), lambda i, j: (layer, 0, nf + j)),
            pl.BlockSpec((3, tf), lambda i, j: (0, j)),
            pl.BlockSpec((3, tf), lambda i, j: (0, nf + j)),
            pl.BlockSpec((1, tf), lambda i, j: (0, j)),
            pl.BlockSpec((1, tf), lambda i, j: (0, nf + j)),
        ],
        out_specs=pl.BlockSpec((tm, tf), lambda i, j: (i, j)),
        scratch_shapes=[pltpu.VMEM((ext, D_MODEL), BF16), pltpu.VMEM((ext, tf), F32), pltpu.VMEM((ext, tf), F32)],
        compiler_params=_cparams(52),
        name="ffn_up",
    )(h, h, h, w_up, w_up, conv_w, conv_w, conv_b, conv_b)


def kernel(x, c, ctx, c_ctx, w_ada, b_ada, g_mix_pre, g_mix_post, g_ffn_pre, g_ffn_post, w_in, na_rpb, wg_sink,
           s5_lam_re, s5_lam_im, s5_log_step, s5_b_re, s5_b_im, s5_c_re, s5_c_im, s5_d, s5_w_glu, s5_b_glu,
           ml_gate_bias, ml_norm, w_branch, w_out, w_up, ffn_conv_w, ffn_conv_b, w_down):
    assert x.shape == (BATCH, SEQ, D_MODEL) and ctx.shape == (BATCH, CTX_LEN, D_MODEL)
    xs = (x.reshape(T_LAT, D_MODEL), ctx.reshape(T_CTX, D_MODEL))
    cs = jnp.concatenate([c, c_ctx[None, :], jnp.zeros((8 - BATCH - 1, D_MODEL), F32)], axis=0)
    mod4 = _ada(cs, w_ada, b_ada).reshape(DEPTH, 8, 1, 6 * D_MODEL)
    cos_t, sin_t = _rope_tables()

    for l in range(DEPTH):
        ctx_out = l < DEPTH - 1
        n_tiles = N_ALL_TILES if ctx_out else N_LAT_TILES
        rows = n_tiles * TOK_TM
        big_tm = rows // 4
        w_mlg = jnp.pad(w_in[l, :, COL_MLG:COL_GATE], ((0, 0), (0, 128 - 4 * ML_HEADS))).astype(BF16)
        w_gate = w_in[l, :, COL_GATE:].astype(BF16)

        h = _norm_mod(xs, g_mix_pre[l], mod4, l, 0, 1, N_ALL_TILES)
        proj = _matmul(h, w_in, PROJ_W, 512, F32, tm=T_ALL // 4, rows=T_ALL, layer=l)
        mlg = _matmul(h, w_mlg, 128, 128, F32, tm=T_ALL // 4, rows=T_ALL)
        gates = _matmul(h, w_gate, GATE_W, 512, BF16, tm=big_tm, rows=rows, act="sigmoid")

        br_na = _neighbourhood_attention(proj, na_rpb[l])
        br_wg = _windowed_gqa(proj, wg_sink[l], cos_t, sin_t)

        p_re, p_im, bb_re, bb_im = _s5_discretise(s5_lam_re[l], s5_lam_im[l], s5_log_step[l], s5_b_re[l], s5_b_im[l])
        ys = []
        for dr in range(2):
            tabs = _s5_tables(p_re[dr], p_im[dr], bb_re[dr], bb_im[dr], s5_c_re[l, dr], s5_c_im[l, dr], rev=dr == 1)
            tab, bbre_t, bbim_t, cre_t, cim_t = tabs
            ys.append(_s5_scan(proj, tab, bbre_t, bbim_t, cre_t, cim_t, rev=dr == 1))
        br_s5 = _s5_out(proj, ys[0], ys[1], s5_d[l], s5_w_glu[l].astype(BF16), s5_b_glu[l])

        gate_bias = jnp.pad(ml_gate_bias[l].reshape(1, 4 * ML_HEADS), ((0, 0), (0, 128 - 4 * ML_HEADS)))
        h_f = _mlstm(proj, mlg, gate_bias, rev=False)
        h_b = _mlstm(proj, mlg, gate_bias, rev=True)
        br_ml = _ml_out(h_f, h_b, proj, ml_norm[l])

        z = _merge((br_na, br_s5, br_wg, br_ml), w_branch[l].astype(BF16), gates, rows, rows // 8)
        xs = _matmul_norm_residual(z, w_out[l].astype(BF16), xs, g_mix_post[l], mod4, l, 2, n_tiles, 1024)

        h2 = _norm_mod(xs, g_ffn_pre[l], mod4, l, 3, 4, n_tiles)
        act = _ffn_up(h2, w_up, l, ffn_conv_w[l], ffn_conv_b[l], rows, rows // 8)
        xs = _matmul_norm_residual(act, w_down[l].astype(BF16), xs, g_ffn_post[l], mod4, l, 5, n_tiles, 512)

    return xs.reshape(BATCH, SEQ, D_MODEL)
```

```python
import functools

import jax
import jax.numpy as jnp
from jax import lax
from jax.experimental import pallas as pl
from jax.experimental.pallas import tpu as pltpu

F32 = jnp.float32
BF16 = jnp.bfloat16

D_MODEL = 2048
BATCH = 2
SEQ = 4096
DEPTH = 2
GRID_W = 64
CTX_LEN = 256
HEAD_DIM = 128
BRANCH_W = 512
N_BRANCH = 4
NA_HEADS = 4
NA_WIN_ROWS = 8
NA_WIN_COLS = 16
S5_GROUP = 16
S5_GROUPS = BRANCH_W // S5_GROUP
S5_STATE = 64
S5_CH = S5_GROUPS * S5_STATE
WG_Q_HEADS = 4
WG_KV_HEADS = 2
WG_WINDOW = 128
WG_BLOCK = 128
ML_HEADS = 4
ML_CHUNK = 128
FFN_DIM = 5632
ROPE_BASE = 10000.0
EPS = 1e-6
NEG_INF = -1e30

T_LAT = BATCH * SEQ
T_CTX = BATCH * CTX_LEN
T_ALL = T_LAT + T_CTX
GRID_ROWS = SEQ // GRID_W
ATT_SCALE = HEAD_DIM ** -0.5

COL_NAQ, COL_NAK, COL_NAV, COL_S5 = 0, 512, 1024, 1536
COL_WGQ, COL_WGK, COL_WGV = 2048, 2560, 2816
COL_MLQ, COL_MLK, COL_MLV, COL_MLO = 3072, 3584, 4096, 4608
PROJ_W = 5120
COL_MLG = PROJ_W
COL_GATE = PROJ_W + 4 * ML_HEADS
GATE_W = N_BRANCH * D_MODEL

MIB = 1024 * 1024
TOK_TM = 512
N_LAT_TILES = T_LAT // TOK_TM
N_ALL_TILES = T_ALL // TOK_TM
ATT_QBLK = 256
NA_ROWS_PER_STEP = ATT_QBLK // GRID_W
S5_CHUNK = 256
S5_SUB = 8


def _cparams(vmem_mib=None):
    if vmem_mib is None:
        return None
    return pltpu.CompilerParams(vmem_limit_bytes=vmem_mib * MIB)


def _dot(a, b):
    return jnp.dot(a, b, preferred_element_type=F32)


def _dot_nt(a, b):
    return lax.dot_general(a, b, (((1,), (1,)), ((), ())), preferred_element_type=F32)


def _mod_row(i):
    return jnp.where(i >= N_LAT_TILES, BATCH, i // (N_LAT_TILES // BATCH))


def _mod_spec(layer, chunk):
    return pl.BlockSpec((None, None, 1, D_MODEL), lambda i, *_: (layer, _mod_row(i), 0, chunk))


def _seq_block(b, j, blk, rev):
    n_ctx = CTX_LEN // blk
    n_lat = SEQ // blk
    if rev:
        ctx = T_LAT // blk + b * n_ctx + (n_ctx - 1 - j)
        lat = b * n_lat + (n_lat - 1 - (j - n_ctx))
    else:
        ctx = T_LAT // blk + b * n_ctx + j
        lat = b * n_lat + (j - n_ctx)
    return jnp.where(j < n_ctx, ctx, lat)


def _ada_kernel(c_ref, w_ref, b_ref, o_ref):
    c = c_ref[...]
    s = (c * jax.nn.sigmoid(c)).astype(BF16)
    o_ref[...] = _dot(s, w_ref[...].astype(BF16)) + b_ref[...]


def _ada(cs, w_ada, b_ada):
    tn = 1024
    n_out = 6 * D_MODEL
    return pl.pallas_call(
        _ada_kernel,
        out_shape=jax.ShapeDtypeStruct((DEPTH, 8, n_out), F32),
        grid=(DEPTH, n_out // tn),
        in_specs=[
            pl.BlockSpec((8, D_MODEL), lambda l, j: (0, 0)),
            pl.BlockSpec((None, D_MODEL, tn), lambda l, j: (l, 0, j)),
            pl.BlockSpec((None, 1, tn), lambda l, j: (l, 0, j)),
        ],
        out_specs=pl.BlockSpec((None, 8, tn), lambda l, j: (l, 0, j)),
        compiler_params=_cparams(40),
        name="ada",
    )(cs, w_ada, b_ada.reshape(DEPTH, 1, n_out))


def _x_specs(x_pair, n_tiles):
    if isinstance(x_pair, tuple):
        lat, ctx = x_pair
        last = N_LAT_TILES - 1
        specs = [pl.BlockSpec((TOK_TM, D_MODEL), lambda i, *_: (jnp.minimum(i, last), 0)),
                 pl.BlockSpec((TOK_TM, D_MODEL), lambda i, *_: (0, 0))]
        return specs, [lat, ctx]
    return [pl.BlockSpec((TOK_TM, D_MODEL), lambda i, *_: (i, 0))], [x_pair]


def _for_tile_source(x_refs, fn):
    if len(x_refs) == 1:
        fn(x_refs[0])
        return
    i = pl.program_id(0)
    pl.when(i < N_LAT_TILES)(lambda: fn(x_refs[0]))
    pl.when(i >= N_LAT_TILES)(lambda: fn(x_refs[1]))


def _norm_mod_kernel(*refs):
    *x_refs, g_ref, sh_ref, sc_ref, o_ref = refs

    def body(x_ref):
        x = x_ref[...]
        y = x * lax.rsqrt(jnp.mean(x * x, axis=-1, keepdims=True) + EPS) * g_ref[...]
        o_ref[...] = (y * (1.0 + sc_ref[...]) + sh_ref[...]).astype(BF16)

    _for_tile_source(x_refs, body)


def _norm_mod(x_pair, g, mod4, layer, shift_chunk, scale_chunk, n_tiles):
    x_specs, x_args = _x_specs(x_pair, n_tiles)
    return pl.pallas_call(
        _norm_mod_kernel,
        out_shape=jax.ShapeDtypeStruct((n_tiles * TOK_TM, D_MODEL), BF16),
        grid=(n_tiles,),
        in_specs=x_specs + [pl.BlockSpec((1, D_MODEL), lambda i: (0, 0)),
                            _mod_spec(layer, shift_chunk), _mod_spec(layer, scale_chunk)],
        out_specs=pl.BlockSpec((TOK_TM, D_MODEL), lambda i: (i, 0)),
        name="norm_mod",
    )(*x_args, g.reshape(1, D_MODEL), mod4, mod4)


def _mm_kernel(a_ref, w_ref, o_ref, *, act):
    r = _dot(a_ref[...], w_ref[...].astype(BF16))
    if act == "sigmoid":
        r = jax.nn.sigmoid(r)
    o_ref[...] = r.astype(o_ref.dtype)


def _matmul(a, w, n_cols, tn, out_dtype, *, tm, rows, layer=None, act=None):
    k = a.shape[1]
    if layer is None:
        w_spec = pl.BlockSpec((k, tn), lambda i, j: (0, j))
    else:
        w_spec = pl.BlockSpec((None, k, tn), lambda i, j: (layer, 0, j))
    return pl.pallas_call(
        functools.partial(_mm_kernel, act=act),
        out_shape=jax.ShapeDtypeStruct((rows, n_cols), out_dtype),
        grid=(rows // tm, n_cols // tn),
        in_specs=[pl.BlockSpec((tm, k), lambda i, j: (i, 0)), w_spec],
        out_specs=pl.BlockSpec((tm, tn), lambda i, j: (i, j)),
        compiler_params=_cparams(52),
        name="matmul",
    )(a, w)


def _att_row_block(b, s):
    n_lat = SEQ // ATT_QBLK
    return jnp.where(s == 0, T_LAT // ATT_QBLK + b, b * n_lat + s - 1)


NA_UNION_ROWS = NA_ROWS_PER_STEP + NA_WIN_ROWS - 1


def _na_kernel(q_ref, kl_ref, vl_ref, kc_ref, vc_ref, tb_ref, o_ref):
    step = pl.program_id(2)
    kc = kc_ref[...].astype(BF16)
    vc = vc_ref[...].astype(BF16)
    q = q_ref[...].astype(BF16)
    s_ctx = _dot_nt(q, kc) * ATT_SCALE

    @pl.when(step == 0)
    def _context_queries():
        p = jnp.exp(s_ctx - jnp.max(s_ctx, axis=-1, keepdims=True))
        o = _dot(p.astype(BF16), vc) / jnp.sum(p, axis=-1, keepdims=True)
        o_ref[...] = o.astype(o_ref.dtype)

    @pl.when(step > 0)
    def _latent_queries():
        rq0 = (step - 1) * NA_ROWS_PER_STEP
        u0 = jnp.clip(rq0 - NA_WIN_ROWS // 2, 0, GRID_ROWS - NA_UNION_ROWS)
        pattern = jnp.where(rq0 == 0, 0, jnp.where(rq0 == GRID_ROWS - NA_ROWS_PER_STEP, 2, 1))
        start = pl.multiple_of(u0 * GRID_W, GRID_W)
        n_win = NA_UNION_ROWS * GRID_W
        kw = kl_ref[pl.ds(start, n_win), :].astype(BF16)
        vw = vl_ref[pl.ds(start, n_win), :].astype(BF16)
        bias = tb_ref[pattern]
        s_loc = jnp.where(bias > 0.5 * NEG_INF, _dot_nt(q, kw) * ATT_SCALE + bias, NEG_INF)
        m = jnp.maximum(jnp.max(s_loc, axis=-1, keepdims=True), jnp.max(s_ctx, axis=-1, keepdims=True))
        p_loc = jnp.exp(s_loc - m)
        p_ctx = jnp.exp(s_ctx - m)
        den = jnp.sum(p_loc, axis=-1, keepdims=True) + jnp.sum(p_ctx, axis=-1, keepdims=True)
        o = (_dot(p_loc.astype(BF16), vw) + _dot(p_ctx.astype(BF16), vc)) / den
        o_ref[...] = o.astype(o_ref.dtype)


def _na_bias_table(rpb):
    assert NA_ROWS_PER_STEP == NA_WIN_ROWS // 2 and GRID_ROWS % NA_ROWS_PER_STEP == 0
    n_r, n_u = NA_ROWS_PER_STEP, NA_UNION_ROWS
    col = jnp.arange(GRID_W)
    c0 = jnp.clip(col - NA_WIN_COLS // 2, 0, GRID_W - NA_WIN_COLS)
    col_ok = (col[None, :] >= c0[:, None]) & (col[None, :] < c0[:, None] + NA_WIN_COLS)
    dc = jnp.clip(col[None, :] - col[:, None] + NA_WIN_COLS - 1, 0, 2 * NA_WIN_COLS - 2)
    per_dr = jnp.where(col_ok[None, None], rpb[:, :, dc].astype(F32), NEG_INF)
    d = jnp.arange(n_r)[:, None]
    i = jnp.arange(n_u)[None, :]
    zero = jnp.zeros_like(d)
    tabs = []
    for u_off, i0 in ((0, zero), (-(NA_WIN_ROWS // 2), d), (n_r - n_u, zero + (n_u - NA_WIN_ROWS))):
        dr = u_off + i - d + NA_WIN_ROWS - 1
        visible = (i >= i0) & (i < i0 + NA_WIN_ROWS)
        t = per_dr[:, jnp.clip(dr, 0, 2 * NA_WIN_ROWS - 2)]
        t = jnp.where(visible[None, :, :, None, None], t, NEG_INF)
        tabs.append(jnp.transpose(t, (0, 1, 3, 2, 4)).reshape(NA_HEADS, n_r * GRID_W, n_u * GRID_W))
    return jnp.stack(tabs, axis=1)


def _neighbourhood_attention(proj, rpb):
    tb = _na_bias_table(rpb)
    n_steps = 1 + SEQ // ATT_QBLK
    hb = HEAD_DIM
    ctx_blk = T_LAT // CTX_LEN
    return pl.pallas_call(
        _na_kernel,
        out_shape=jax.ShapeDtypeStruct((T_ALL, BRANCH_W), BF16),
        grid=(BATCH, NA_HEADS, n_steps),
        in_specs=[
            pl.BlockSpec((ATT_QBLK, hb), lambda b, h, s: (_att_row_block(b, s), COL_NAQ // hb + h)),
            pl.BlockSpec((SEQ, hb), lambda b, h, s: (b, COL_NAK // hb + h)),
            pl.BlockSpec((SEQ, hb), lambda b, h, s: (b, COL_NAV // hb + h)),
            pl.BlockSpec((CTX_LEN, hb), lambda b, h, s: (ctx_blk + b, COL_NAK // hb + h)),
            pl.BlockSpec((CTX_LEN, hb), lambda b, h, s: (ctx_blk + b, COL_NAV // hb + h)),
            pl.BlockSpec((None, 3, ATT_QBLK, NA_UNION_ROWS * GRID_W), lambda b, h, s: (h, 0, 0, 0)),
        ],
        out_specs=pl.BlockSpec((ATT_QBLK, hb), lambda b, h, s: (_att_row_block(b, s), h)),
        name="na_attn",
    )(proj, proj, proj, proj, proj, tb)


def _rope(x, cos, sin_signed):
    lane = lax.broadcasted_iota(jnp.int32, x.shape, 1)
    partner = jnp.where(lane % 64 < 32, pltpu.roll(x, 96, 1), pltpu.roll(x, 32, 1))
    return x * cos + partner * sin_signed


def _wg_kernel(sink_ref, q_ref, kl_ref, vl_ref, kc_ref, vc_ref, cos_ref, sin_ref, o_ref, kr_ref):
    hk = pl.program_id(1)
    step = pl.program_id(2)
    kc = kc_ref[...].astype(BF16)
    vc = vc_ref[...].astype(BF16)
    group = WG_Q_HEADS // WG_KV_HEADS

    @pl.when(step == 0)
    def _context_queries():
        for g in range(group):
            sink = sink_ref[hk * group + g]
            q = q_ref[:, g * HEAD_DIM:(g + 1) * HEAD_DIM].astype(BF16)
            s = _dot_nt(q, kc) * ATT_SCALE
            m = jnp.maximum(jnp.max(s, axis=-1, keepdims=True), sink)
            p = jnp.exp(s - m)
            den = jnp.sum(p, axis=-1, keepdims=True) + jnp.exp(sink - m)
            o_ref[:, g * HEAD_DIM:(g + 1) * HEAD_DIM] = (_dot(p.astype(BF16), vc) / den).astype(o_ref.dtype)

    @pl.when(step == 1)
    def _rope_keys():
        kr_ref[...] = _rope(kl_ref[...], cos_ref[...], sin_ref[...]).astype(BF16)

    @pl.when(step > 0)
    def _latent_queries():
        nb = SEQ // WG_BLOCK
        qi = lax.broadcasted_iota(jnp.int32, (WG_BLOCK, WG_BLOCK), 0)
        kj = lax.broadcasted_iota(jnp.int32, (WG_BLOCK, WG_BLOCK), 1)
        for blk in range(ATT_QBLK // WG_BLOCK):
            n = (step - 1) * (ATT_QBLK // WG_BLOCK) + blk
            base = pl.multiple_of(n * WG_BLOCK, WG_BLOCK)
            prev = pl.multiple_of(jnp.maximum(n - 1, 0) * WG_BLOCK, WG_BLOCK)
            nxt = pl.multiple_of(jnp.minimum(n + 1, nb - 1) * WG_BLOCK, WG_BLOCK)
            cos_q = cos_ref[pl.ds(base, WG_BLOCK), :]
            sin_q = sin_ref[pl.ds(base, WG_BLOCK), :]
            k_prev = kr_ref[pl.ds(prev, WG_BLOCK), :]
            k_self = kr_ref[pl.ds(base, WG_BLOCK), :]
            k_next = kr_ref[pl.ds(nxt, WG_BLOCK), :]
            v_prev = vl_ref[pl.ds(prev, WG_BLOCK), :].astype(BF16)
            v_self = vl_ref[pl.ds(base, WG_BLOCK), :].astype(BF16)
            v_next = vl_ref[pl.ds(nxt, WG_BLOCK), :].astype(BF16)
            ok_prev = kj >= qi + jnp.where(n > 0, 0, WG_BLOCK)
            ok_next = kj + jnp.where(n < nb - 1, 0, WG_BLOCK) <= qi
            for g in range(group):
                sink = sink_ref[hk * group + g]
                rows = slice(blk * WG_BLOCK, (blk + 1) * WG_BLOCK)
                cols = slice(g * HEAD_DIM, (g + 1) * HEAD_DIM)
                q = _rope(q_ref[rows, cols], cos_q, sin_q).astype(BF16)
                s_prev = jnp.where(ok_prev, _dot_nt(q, k_prev) * ATT_SCALE, NEG_INF)
                s_self = _dot_nt(q, k_self) * ATT_SCALE
                s_next = jnp.where(ok_next, _dot_nt(q, k_next) * ATT_SCALE, NEG_INF)
                s_ctx = _dot_nt(q, kc) * ATT_SCALE
                m = jnp.maximum(
                    jnp.maximum(jnp.max(s_prev, axis=-1, keepdims=True), jnp.max(s_self, axis=-1, keepdims=True)),
                    jnp.maximum(jnp.max(s_next, axis=-1, keepdims=True), jnp.max(s_ctx, axis=-1, keepdims=True)))
                m = jnp.maximum(m, sink)
                p_prev = jnp.exp(s_prev - m)
                p_self = jnp.exp(s_self - m)
                p_next = jnp.exp(s_next - m)
                p_ctx = jnp.exp(s_ctx - m)
                den = (jnp.sum(p_prev, axis=-1, keepdims=True) + jnp.sum(p_self, axis=-1, keepdims=True)
                       + jnp.sum(p_next, axis=-1, keepdims=True) + jnp.sum(p_ctx, axis=-1, keepdims=True)
                       + jnp.exp(sink - m))
                o = (_dot(p_prev.astype(BF16), v_prev) + _dot(p_self.astype(BF16), v_self)
                     + _dot(p_next.astype(BF16), v_next) + _dot(p_ctx.astype(BF16), vc)) / den
                o_ref[rows, cols] = o.astype(o_ref.dtype)


def _rope_tables():
    t = jnp.arange(SEQ)
    pos = jnp.stack([t // GRID_W, t % GRID_W], axis=-1).astype(F32)
    n_freq = HEAD_DIM // 4
    inv_freq = ROPE_BASE ** (-jnp.arange(n_freq, dtype=F32) / n_freq)
    ang = pos[:, :, None] * inv_freq
    cos, sin = jnp.cos(ang), jnp.sin(ang)
    cos_t = jnp.concatenate([cos[:, 0], cos[:, 0], cos[:, 1], cos[:, 1]], axis=-1)
    sin_t = jnp.concatenate([-sin[:, 0], sin[:, 0], -sin[:, 1], sin[:, 1]], axis=-1)
    return cos_t, sin_t


def _windowed_gqa(proj, sink, cos_t, sin_t):
    n_steps = 1 + SEQ // ATT_QBLK
    qw = (WG_Q_HEADS // WG_KV_HEADS) * HEAD_DIM
    hb = HEAD_DIM
    ctx_blk = T_LAT // CTX_LEN
    return pl.pallas_call(
        _wg_kernel,
        out_shape=jax.ShapeDtypeStruct((T_ALL, BRANCH_W), BF16),
        grid=(BATCH, WG_KV_HEADS, n_steps),
        in_specs=[
            pl.BlockSpec(memory_space=pltpu.SMEM),
            pl.BlockSpec((ATT_QBLK, qw), lambda b, h, s: (_att_row_block(b, s), COL_WGQ // qw + h)),
            pl.BlockSpec((SEQ, hb), lambda b, h, s: (b, COL_WGK // hb + h)),
            pl.BlockSpec((SEQ, hb), lambda b, h, s: (b, COL_WGV // hb + h)),
            pl.BlockSpec((CTX_LEN, hb), lambda b, h, s: (ctx_blk + b, COL_WGK // hb + h)),
            pl.BlockSpec((CTX_LEN, hb), lambda b, h, s: (ctx_blk + b, COL_WGV // hb + h)),
            pl.BlockSpec((SEQ, hb), lambda b, h, s: (0, 0)),
            pl.BlockSpec((SEQ, hb), lambda b, h, s: (0, 0)),
        ],
        out_specs=pl.BlockSpec((ATT_QBLK, qw), lambda b, h, s: (_att_row_block(b, s), h)),
        scratch_shapes=[pltpu.VMEM((SEQ, hb), BF16)],
        name="wg_attn",
    )(sink, proj, proj, proj, proj, proj, cos_t, sin_t)


def _s5_disc_kernel(lre_ref, lim_ref, lstep_ref, bre_ref, bim_ref, pre_ref, pim_ref, bbre_ref, bbim_ref):
    lre = jnp.minimum(lre_ref[...], -1e-4)
    lim = lim_ref[...]
    step = jnp.exp(lstep_ref[...])
    kk = (lax.broadcasted_iota(jnp.int32, (S5_SUB, 1), 0) + 1).astype(F32)
    mag = jnp.exp(kk * (lre * step))
    ang = kk * (lim * step)
    p_re = mag * jnp.cos(ang)
    p_im = mag * jnp.sin(ang)
    pre_ref[...] = p_re
    pim_ref[...] = p_im
    a_re = p_re[0:1, :]
    a_im = p_im[0:1, :]
    den = lre * lre + lim * lim
    f_re = ((a_re - 1.0) * lre + a_im * lim) / den
    f_im = (a_im * lre - (a_re - 1.0) * lim) / den
    br = bre_ref[...]
    bi = bim_ref[...]
    bbre_ref[...] = f_re * br - f_im * bi
    bbim_ref[...] = f_re * bi + f_im * br


def _s5_discretise(lam_re, lam_im, log_step, b_re, b_im):
    lre = lam_re.reshape(2, 1, S5_CH)
    lim = lam_im.reshape(2, 1, S5_CH)
    lstep = jnp.repeat(log_step, S5_STATE, axis=-1).reshape(2, 1, S5_CH)
    br = jnp.transpose(b_re, (0, 3, 1, 2)).reshape(2, S5_GROUP, S5_CH)
    bi = jnp.transpose(b_im, (0, 3, 1, 2)).reshape(2, S5_GROUP, S5_CH)
    row = lambda n: pl.BlockSpec((None, n, S5_CH), lambda d: (d, 0, 0))
    return pl.pallas_call(
        _s5_disc_kernel,
        out_shape=(jax.ShapeDtypeStruct((2, S5_SUB, S5_CH), F32), jax.ShapeDtypeStruct((2, S5_SUB, S5_CH), F32),
                   jax.ShapeDtypeStruct((2, S5_GROUP, S5_CH), F32), jax.ShapeDtypeStruct((2, S5_GROUP, S5_CH), F32)),
        grid=(2,),
        in_specs=[row(1), row(1), row(1), row(S5_GROUP), row(S5_GROUP)],
        out_specs=(row(S5_SUB), row(S5_SUB), row(S5_GROUP), row(S5_GROUP)),
        name="s5_disc",
    )(lre, lim, lstep, br, bi)


def _s5_scan_kernel(u_ref, bbre_ref, bbim_ref, cre_ref, cim_ref, tab_ref, y_ref, sre_ref, sim_ref, car_ref, *, rev):
    j = pl.program_id(1)

    @pl.when(j == 0)
    def _reset():
        car_ref[...] = jnp.zeros_like(car_ref)

    u = u_ref[...].astype(BF16)
    n_col_tiles = S5_CH // 256
    for c in range(n_col_tiles):
        ub = u[:, 128 * (c // 2):128 * (c // 2) + 128]
        sre_ref[:, 256 * c:256 * (c + 1)] = _dot(ub, bbre_ref[c])
        sim_ref[:, 256 * c:256 * (c + 1)] = _dot(ub, bbim_ref[c])

    n_groups = S5_CHUNK // S5_SUB
    last = 0 if rev else S5_SUB - 1

    def group(gi, carry):
        cr, ci = carry
        g = (n_groups - 1 - gi) if rev else gi
        r0 = pl.multiple_of(g * S5_SUB, S5_SUB)
        xr = sre_ref[pl.ds(r0, S5_SUB), :]
        xi = sim_ref[pl.ds(r0, S5_SUB), :]
        for t, k in enumerate((1, 2, 4)):
            shift = (S5_SUB - k) if rev else k
            ar = tab_ref[2 * t]
            ai = tab_ref[2 * t + 1]
            rr = pltpu.roll(xr, shift, 0)
            ri = pltpu.roll(xi, shift, 0)
            xr, xi = xr + ar * rr - ai * ri, xi + ar * ri + ai * rr
        apr = tab_ref[6]
        api = tab_ref[7]
        xr, xi = xr + apr * cr - api * ci, xi + apr * ci + api * cr
        sre_ref[pl.ds(r0, S5_SUB), :] = xr
        sim_ref[pl.ds(r0, S5_SUB), :] = xi
        return xr[last:last + 1, :], xi[last:last + 1, :]

    cr, ci = lax.fori_loop(0, n_groups, group, (car_ref[0:1, :], car_ref[1:2, :]))
    car_ref[0:1, :] = cr
    car_ref[1:2, :] = ci

    n_out_tiles = BRANCH_W // 128
    kw = S5_CH // n_out_tiles
    for oc in range(n_out_tiles):
        sr = sre_ref[:, kw * oc:kw * (oc + 1)].astype(BF16)
        si = sim_ref[:, kw * oc:kw * (oc + 1)].astype(BF16)
        y_ref[:, 128 * oc:128 * (oc + 1)] = _dot(sr, cre_ref[oc]) + _dot(si, cim_ref[oc])


def _s5_tables(p_re, p_im, bb_re, bb_im, c_re, c_im, rev):
    row = jnp.arange(S5_SUB)[:, None]
    tabs = []
    for k in (1, 2, 4):
        keep = (row <= S5_SUB - 1 - k) if rev else (row >= k)
        tabs += [jnp.where(keep, p_re[k - 1][None, :], 0.0), jnp.where(keep, p_im[k - 1][None, :], 0.0)]
    tabs += [p_re[::-1], p_im[::-1]] if rev else [p_re, p_im]
    tab = jnp.stack(tabs)
    eye = jnp.eye(S5_GROUPS, dtype=F32)

    def in_tiles(bb):
        bbg = bb.reshape(S5_GROUP, S5_GROUPS, S5_STATE)
        full = (eye[:, None, :, None] * jnp.transpose(bbg, (1, 0, 2))[:, :, None, :]).reshape(BRANCH_W, S5_CH)
        return jnp.stack([full[128 * (c // 2):128 * (c // 2) + 128, 256 * c:256 * (c + 1)]
                          for c in range(S5_CH // 256)]).astype(BF16)

    def out_tiles(cc):
        full = (eye[:, None, :, None] * jnp.transpose(cc, (0, 2, 1))[:, :, None, :]).reshape(S5_CH, BRANCH_W)
        kw = S5_CH // 4
        return jnp.stack([full[kw * oc:kw * (oc + 1), 128 * oc:128 * (oc + 1)] for oc in range(4)]).astype(BF16)

    return tab, in_tiles(bb_re), in_tiles(bb_im), out_tiles(c_re.astype(F32)), out_tiles(-c_im.astype(F32))


def _s5_scan(proj, tab, bbre_t, bbim_t, cre_t, cim_t, rev):
    n_chunks = (CTX_LEN + SEQ) // S5_CHUNK
    blk = functools.partial(_seq_block, blk=S5_CHUNK, rev=rev)
    full = lambda shape: pl.BlockSpec(shape, lambda b, j: (0,) * len(shape))
    return pl.pallas_call(
        functools.partial(_s5_scan_kernel, rev=rev),
        out_shape=jax.ShapeDtypeStruct((T_ALL, BRANCH_W), F32),
        grid=(BATCH, n_chunks),
        in_specs=[
            pl.BlockSpec((S5_CHUNK, BRANCH_W), lambda b, j: (blk(b, j), COL_S5 // BRANCH_W)),
            full(bbre_t.shape), full(bbim_t.shape), full(cre_t.shape), full(cim_t.shape), full(tab.shape),
        ],
        out_specs=pl.BlockSpec((S5_CHUNK, BRANCH_W), lambda b, j: (blk(b, j), 0)),
        scratch_shapes=[pltpu.VMEM((S5_CHUNK, S5_CH), F32), pltpu.VMEM((S5_CHUNK, S5_CH), F32),
                        pltpu.VMEM((8, S5_CH), F32)],
        name="s5_scan_bwd" if rev else "s5_scan_fwd",
    )(proj, bbre_t, bbim_t, cre_t, cim_t, tab)


def _s5_out_kernel(u_ref, yf_ref, yb_ref, d_ref, w_ref, b_ref, o_ref):
    y = d_ref[...] * u_ref[...] + yf_ref[...] + yb_ref[...]
    y = jax.nn.gelu(y)
    z = _dot(y.astype(BF16), w_ref[...]) + b_ref[...]
    o_ref[...] = (y * jax.nn.sigmoid(z)).astype(o_ref.dtype)


def _s5_out(proj, y_f, y_b, d_skip, w_glu, layer, b_glu):
    tm = TOK_TM
    tok = lambda cb: pl.BlockSpec((tm, BRANCH_W), lambda i: (i, cb))
    vec = pl.BlockSpec((1, BRANCH_W), lambda i: (0, 0))
    return pl.pallas_call(
        _s5_out_kernel,
        out_shape=jax.ShapeDtypeStruct((T_ALL, BRANCH_W), BF16),
        grid=(T_ALL // tm,),
        in_specs=[tok(COL_S5 // BRANCH_W), tok(0), tok(0), vec,
                  pl.BlockSpec((None, BRANCH_W, BRANCH_W), lambda i: (layer, 0, 0)), vec],
        out_specs=tok(0),
        name="s5_out",
    )(proj, y_f, y_b, d_skip.reshape(1, BRANCH_W), w_glu, b_glu.reshape(1, BRANCH_W))


def _mlstm_kernel(qf_ref, kf_ref, vf_ref, gf_ref, qb_ref, kb_ref, vb_ref, gb_ref, bias_ref, hf_ref, hb_ref,
                  cf_ref, nf_ref, mf_ref, cb_ref, nb_ref, mb_ref):
    j = pl.program_id(1)

    @pl.when(j == 0)
    def _reset():
        for ref in (cf_ref, nf_ref, mf_ref, cb_ref, nb_ref, mb_ref):
            ref[...] = jnp.zeros_like(ref)

    _mlstm_chunk(qf_ref, kf_ref, vf_ref, gf_ref, bias_ref, hf_ref, cf_ref, nf_ref, mf_ref, rev=False)
    _mlstm_chunk(qb_ref, kb_ref, vb_ref, gb_ref, bias_ref, hb_ref, cb_ref, nb_ref, mb_ref, rev=True)


def _mlstm_chunk(q_ref, k_ref, v_ref, g_ref, bias_ref, h_ref, c_ref, n_ref, m_ref, *, rev):
    L = ML_CHUNK
    state = [(c_ref[h], n_ref[h][0:1, :], m_ref[h][0:1, 0:1]) for h in range(ML_HEADS)]
    t_idx = lax.broadcasted_iota(jnp.int32, (L, L), 0)
    s_idx = lax.broadcasted_iota(jnp.int32, (L, L), 1)
    tri = (s_idx >= t_idx) if rev else (s_idx <= t_idx)
    tri_bf = jnp.where(tri, 1.0, 0.0).astype(BF16)
    ones_bf = jnp.ones((L, HEAD_DIM), BF16)

    gates = g_ref[...] + bias_ref[...]
    log_f = jax.nn.log_sigmoid(gates)
    hi = log_f.astype(BF16)
    r1 = log_f - hi.astype(F32)
    mid = r1.astype(BF16)
    lo = (r1 - mid.astype(F32)).astype(BF16)
    bcum = _dot(tri_bf, hi) + _dot(tri_bf, mid) + _dot(tri_bf, lo)
    gates_t = gates.T
    bcum_t = bcum.T
    last = 0 if rev else L - 1
    d = 1 if rev else 0

    for h in range(ML_HEADS):
        ii = (2 * d) * ML_HEADS + h
        fi = (2 * d + 1) * ML_HEADS + h
        li_r = gates_t[ii:ii + 1, :]
        bc_c = bcum[:, fi:fi + 1]
        bc_r = bcum_t[fi:fi + 1, :]
        b_last = bcum_t[fi:fi + 1, last:last + 1]
        c_prev, n_prev, m_prev = state[h]
        cols = slice(h * HEAD_DIM, (h + 1) * HEAD_DIM)
        q = q_ref[:, cols]
        k = k_ref[:, cols] * ATT_SCALE
        v = v_ref[:, cols]
        qb, kb, vb = q.astype(BF16), k.astype(BF16), v.astype(BF16)

        log_end = b_last - bc_r + li_r
        m_new = jnp.maximum(b_last + m_prev, jnp.max(log_end, axis=-1, keepdims=True))
        w_end = jnp.exp(log_end - m_new)
        decay = jnp.exp(b_last + m_prev - m_new)

        bc_full = jnp.broadcast_to(bc_c, (L, L))
        log_w = jnp.where(tri, bc_full - bc_r + li_r, NEG_INF)
        log_inter = bc_full + m_prev
        m_t = jnp.maximum(log_inter, jnp.broadcast_to(jnp.max(log_w, axis=-1, keepdims=True), (L, L)))
        w = jnp.exp(log_w - m_t)
        inter = jnp.exp(log_inter - m_t)
        sb = (_dot_nt(qb, kb) * w).astype(BF16)
        num = inter * _dot_nt(qb, c_prev.astype(BF16)) + _dot(sb, vb)
        n_rows = jnp.broadcast_to(n_prev, (L, HEAD_DIM)).astype(BF16)
        den = inter * _dot_nt(qb, n_rows) + _dot(sb, ones_bf)
        h_ref[:, cols] = num / jnp.maximum(jnp.abs(den), jnp.exp(-m_t))

        vw_t = (v.T * w_end).astype(BF16)
        c_ref[h] = decay * c_prev + _dot(vw_t, kb)
        n_new = decay * n_prev + _dot(jnp.broadcast_to(w_end, (8, L)).astype(BF16), kb)[0:1, :]
        n_ref[h] = jnp.broadcast_to(n_new, (8, HEAD_DIM))
        m_ref[h] = jnp.broadcast_to(m_new, (8, HEAD_DIM))


def _mlstm(proj, mlg, gate_bias):
    n_chunks = (CTX_LEN + SEQ) // ML_CHUNK
    w = ML_HEADS * HEAD_DIM

    def specs(rev):
        blk = functools.partial(_seq_block, blk=ML_CHUNK, rev=rev)
        tok = lambda cb: pl.BlockSpec((ML_CHUNK, w), lambda b, j: (blk(b, j), cb))
        return [tok(COL_MLQ // w), tok(COL_MLK // w), tok(COL_MLV // w),
                pl.BlockSpec((ML_CHUNK, 128), lambda b, j: (blk(b, j), 0))], tok(0)

    in_f, out_f = specs(False)
    in_b, out_b = specs(True)
    out = jax.ShapeDtypeStruct((T_ALL, w), F32)
    return pl.pallas_call(
        _mlstm_kernel,
        out_shape=(out, out),
        grid=(BATCH, n_chunks),
        in_specs=in_f + in_b + [pl.BlockSpec((1, 128), lambda b, j: (0, 0))],
        out_specs=(out_f, out_b),
        scratch_shapes=[pltpu.VMEM((ML_HEADS, HEAD_DIM, HEAD_DIM), F32), pltpu.VMEM((ML_HEADS, 8, HEAD_DIM), F32),
                        pltpu.VMEM((ML_HEADS, 8, HEAD_DIM), F32)] * 2,
        name="mlstm",
    )(proj, proj, proj, mlg, proj, proj, proj, mlg, gate_bias)


def _ml_out_kernel(hf_ref, hb_ref, o_ref, nrm_ref, out_ref):
    for h in range(ML_HEADS):
        cols = slice(h * HEAD_DIM, (h + 1) * HEAD_DIM)
        x = hf_ref[:, cols] + hb_ref[:, cols]
        y = x * lax.rsqrt(jnp.mean(x * x, axis=-1, keepdims=True) + EPS) * nrm_ref[:, cols]
        out_ref[:, cols] = (y * jax.nn.sigmoid(o_ref[:, cols])).astype(out_ref.dtype)


def _ml_out(h_f, h_b, proj, ml_norm):
    tm = TOK_TM
    w = ML_HEADS * HEAD_DIM
    tok = lambda cb: pl.BlockSpec((tm, w), lambda i: (i, cb))
    return pl.pallas_call(
        _ml_out_kernel,
        out_shape=jax.ShapeDtypeStruct((T_ALL, w), BF16),
        grid=(T_ALL // tm,),
        in_specs=[tok(0), tok(0), tok(COL_MLO // w), pl.BlockSpec((1, w), lambda i: (0, 0))],
        out_specs=tok(0),
        name="ml_out",
    )(h_f, h_b, proj, ml_norm.reshape(1, w))


def _merge_kernel(b0_ref, b1_ref, b2_ref, b3_ref, w_ref, g0_ref, g1_ref, g2_ref, g3_ref, z_ref):
    acc = g0_ref[...].astype(F32) * _dot(b0_ref[...], w_ref[0])
    acc += g1_ref[...].astype(F32) * _dot(b1_ref[...], w_ref[1])
    acc += g2_ref[...].astype(F32) * _dot(b2_ref[...], w_ref[2])
    acc += g3_ref[...].astype(F32) * _dot(b3_ref[...], w_ref[3])
    z_ref[...] = acc.astype(z_ref.dtype)


def _merge(branches, w_branch, layer, gates, rows, tm):
    tn = 512
    nj = D_MODEL // tn
    br = pl.BlockSpec((tm, BRANCH_W), lambda i, j: (i, 0))
    gate = lambda g: pl.BlockSpec((tm, tn), lambda i, j: (i, g * nj + j))
    return pl.pallas_call(
        _merge_kernel,
        out_shape=jax.ShapeDtypeStruct((rows, D_MODEL), BF16),
        grid=(rows // tm, nj),
        in_specs=[br, br, br, br, pl.BlockSpec((None, N_BRANCH, BRANCH_W, tn), lambda i, j: (layer, 0, 0, j)),
                  gate(0), gate(1), gate(2), gate(3)],
        out_specs=pl.BlockSpec((tm, tn), lambda i, j: (i, j)),
        compiler_params=_cparams(40),
        name="merge",
    )(*branches, w_branch, gates, gates, gates, gates)


def _mm_norm_res_kernel(*refs, nj, tn):
    a_ref, w_ref, *x_refs, g_ref, gt_ref, o_ref, y_ref = refs
    j = pl.program_id(1)
    y_ref[j] = _dot(a_ref[...], w_ref[...])

    @pl.when(j == nj - 1)
    def _finish():
        ss = jnp.sum(y_ref[0] * y_ref[0], axis=-1, keepdims=True)
        for jj in range(1, nj):
            ss += jnp.sum(y_ref[jj] * y_ref[jj], axis=-1, keepdims=True)
        rs = lax.rsqrt(ss * (1.0 / D_MODEL) + EPS)

        def emit(x_ref):
            for jj in range(nj):
                cols = slice(jj * tn, (jj + 1) * tn)
                o_ref[:, cols] = x_ref[:, cols] + gt_ref[:, cols] * (y_ref[jj] * rs * g_ref[:, cols])

        _for_tile_source(x_refs, emit)


def _matmul_norm_residual(a, w, x_pair, g, mod4, layer, gate_chunk, n_tiles, tn):
    k_dim = a.shape[1]
    nj = D_MODEL // tn
    x_specs, x_args = _x_specs(x_pair, n_tiles)
    return pl.pallas_call(
        functools.partial(_mm_norm_res_kernel, nj=nj, tn=tn),
        out_shape=jax.ShapeDtypeStruct((n_tiles * TOK_TM, D_MODEL), F32),
        grid=(n_tiles, nj),
        in_specs=[pl.BlockSpec((TOK_TM, k_dim), lambda i, j: (i, 0)),
                  pl.BlockSpec((None, k_dim, tn), lambda i, j: (layer, 0, j))]
                 + x_specs
                 + [pl.BlockSpec((1, D_MODEL), lambda i, j: (0, 0)), _mod_spec(layer, gate_chunk)],
        out_specs=pl.BlockSpec((TOK_TM, D_MODEL), lambda i, j: (i, 0)),
        scratch_shapes=[pltpu.VMEM((nj, TOK_TM, tn), F32)],
        compiler_params=_cparams(52),
        name="matmul_norm_res",
    )(a, w, *x_args, g.reshape(1, D_MODEL), mod4)


FFN_TF = 512
FFN_HALO = 16
FFN_ROW_BLOCKS = 4
FFN_ALIGN = 16
FFN_LAG = 2 * FFN_ALIGN


def _ffn_up_kernel(hp_ref, hm_ref, hn_ref, wa_ref, wg_ref, cwa_ref, cwg_ref, cba_ref, cbg_ref, o_ref,
                   hext_ref, ua_ref, ug_ref, *, tm):
    i = pl.program_id(0)
    j = pl.program_id(1)
    halo = FFN_HALO
    ext = tm + 2 * halo
    rb = -(-ext // (FFN_ROW_BLOCKS * FFN_ALIGN)) * FFN_ALIGN

    @pl.when(j == 0)
    def _assemble_rows():
        hext_ref[0:halo, :] = hp_ref[...]
        hext_ref[halo:halo + tm, :] = hm_ref[...]
        hext_ref[halo + tm:ext, :] = hn_ref[...]

    wa = wa_ref[...].astype(BF16)
    wg = wg_ref[...].astype(BF16)

    def conv(u_ref, cw_ref, cb_ref, lo, n, has_prev, has_next):
        prev = u_ref[pl.ds(lo + halo - 1, n), :] * has_prev
        mid = u_ref[pl.ds(lo + halo, n), :]
        nxt = u_ref[pl.ds(lo + halo + 1, n), :] * has_next
        return prev * cw_ref[0:1, :] + mid * cw_ref[1:2, :] + nxt * cw_ref[2:3, :] + cb_ref[...]

    lo = 0
    for blk in range(FFN_ROW_BLOCKS):
        rows = slice(blk * rb, min((blk + 1) * rb, ext))
        hx = hext_ref[rows, :]
        ua_ref[rows, :] = _dot(hx, wa)
        ug_ref[rows, :] = _dot(hx, wg)
        hi = tm if blk == FFN_ROW_BLOCKS - 1 else (blk + 1) * rb - FFN_LAG
        n = hi - lo
        row = i * tm + lo + lax.broadcasted_iota(jnp.int32, (n, 1), 0)
        first = (row == 0) | (row == SEQ) | (row == T_LAT) | (row == T_LAT + CTX_LEN)
        final = (row == SEQ - 1) | (row == T_LAT - 1) | (row == T_LAT + CTX_LEN - 1) | (row == T_ALL - 1)
        has_prev = jnp.where(first, 0.0, 1.0)
        has_next = jnp.where(final, 0.0, 1.0)
        a = conv(ua_ref, cwa_ref, cba_ref, lo, n, has_prev, has_next)
        g = conv(ug_ref, cwg_ref, cbg_ref, lo, n, has_prev, has_next)
        o_ref[lo:hi, :] = (a * (g * jax.nn.sigmoid(g))).astype(o_ref.dtype)
        lo = hi


def _ffn_up(h, w_up, layer, conv_w, conv_b, rows, tm):
    tf, halo = FFN_TF, FFN_HALO
    nf = FFN_DIM // tf
    hb = tm // halo
    n_halo_blocks = rows // halo
    ext = tm + 2 * halo
    assert tm % halo == 0 and rows % tm == 0
    conv_b = conv_b.reshape(1, 2 * FFN_DIM)
    return pl.pallas_call(
        functools.partial(_ffn_up_kernel, tm=tm),
        out_shape=jax.ShapeDtypeStruct((rows, FFN_DIM), BF16),
        grid=(rows // tm, nf),
        in_specs=[
            pl.BlockSpec((halo, D_MODEL), lambda i, j: (jnp.maximum(i * hb - 1, 0), 0)),
            pl.BlockSpec((tm, D_MODEL), lambda i, j: (i, 0)),
            pl.BlockSpec((halo, D_MODEL), lambda i, j: (jnp.minimum((i + 1) * hb, n_halo_blocks - 1), 0)),
            pl.BlockSpec((None, D_MODEL, tf), lambda i, j: (layer, 0, j)),
            pl.BlockSpec((None, D_MODEL, tf), lambda i, j: (layer, 0, nf + j)),
            pl.BlockSpec((3, tf), lambda i, j: (0, j)),
            pl.BlockSpec((3, tf), lambda i, j: (0, nf + j)),
            pl.BlockSpec((1, tf), lambda i, j: (0, j)),
            pl.BlockSpec((1, tf), lambda i, j: (0, nf + j)),
        ],
        out_specs=pl.BlockSpec((tm, tf), lambda i, j: (i, j)),
        scratch_shapes=[pltpu.VMEM((ext, D_MODEL), BF16), pltpu.VMEM((ext, tf), F32), pltpu.VMEM((ext, tf), F32)],
        compiler_params=_cparams(52),
        name="ffn_up",
    )(h, h, h, w_up, w_up, conv_w, conv_w, conv_b, conv_b)


def kernel(x, c, ctx, c_ctx, w_ada, b_ada, g_mix_pre, g_mix_post, g_ffn_pre, g_ffn_post, w_in, na_rpb, wg_sink,
           s5_lam_re, s5_lam_im, s5_log_step, s5_b_re, s5_b_im, s5_c_re, s5_c_im, s5_d, s5_w_glu, s5_b_glu,
           ml_gate_bias, ml_norm, w_branch, w_out, w_up, ffn_conv_w, ffn_conv_b, w_down):
    assert x.shape == (BATCH, SEQ, D_MODEL) and ctx.shape == (BATCH, CTX_LEN, D_MODEL)
    xs = (x.reshape(T_LAT, D_MODEL), ctx.reshape(T_CTX, D_MODEL))
    cs = jnp.concatenate([c, c_ctx[None, :], jnp.zeros((8 - BATCH - 1, D_MODEL), F32)], axis=0)
    mod4 = _ada(cs, w_ada, b_ada).reshape(DEPTH, 8, 1, 6 * D_MODEL)
    cos_t, sin_t = _rope_tables()

    w_main = w_in[:, :, :PROJ_W].astype(BF16)
    w_mlg = jnp.pad(w_in[:, :, COL_MLG:COL_GATE], ((0, 0), (0, 0), (0, 128 - 4 * ML_HEADS))).astype(BF16)
    w_gate = w_in[:, :, COL_GATE:].astype(BF16)
    w_branch_bf = w_branch.astype(BF16)
    w_out_bf = w_out.astype(BF16)
    w_down_bf = w_down.astype(BF16)
    w_glu_bf = s5_w_glu.astype(BF16)

    for l in range(DEPTH):
        ctx_out = l < DEPTH - 1
        n_tiles = N_ALL_TILES if ctx_out else N_LAT_TILES
        rows = n_tiles * TOK_TM
        big_tm = rows // 4

        h = _norm_mod(xs, g_mix_pre[l], mod4, l, 0, 1, N_ALL_TILES)
        proj = _matmul(h, w_main, PROJ_W, 512, F32, tm=T_ALL // 4, rows=T_ALL, layer=l)
        mlg = _matmul(h, w_mlg, 128, 128, F32, tm=T_ALL // 4, rows=T_ALL, layer=l)
        gates = _matmul(h, w_gate, GATE_W, 512, BF16, tm=big_tm, rows=rows, layer=l, act="sigmoid")

        br_na = _neighbourhood_attention(proj, na_rpb[l])
        br_wg = _windowed_gqa(proj, wg_sink[l], cos_t, sin_t)

        p_re, p_im, bb_re, bb_im = _s5_discretise(s5_lam_re[l], s5_lam_im[l], s5_log_step[l], s5_b_re[l], s5_b_im[l])
        ys = []
        for dr in range(2):
            tabs = _s5_tables(p_re[dr], p_im[dr], bb_re[dr], bb_im[dr], s5_c_re[l, dr], s5_c_im[l, dr], rev=dr == 1)
            tab, bbre_t, bbim_t, cre_t, cim_t = tabs
            ys.append(_s5_scan(proj, tab, bbre_t, bbim_t, cre_t, cim_t, rev=dr == 1))
        br_s5 = _s5_out(proj, ys[0], ys[1], s5_d[l], w_glu_bf, l, s5_b_glu[l])

        gate_bias = jnp.pad(ml_gate_bias[l].reshape(1, 4 * ML_HEADS), ((0, 0), (0, 128 - 4 * ML_HEADS)))
        h_f, h_b = _mlstm(proj, mlg, gate_bias)
        br_ml = _ml_out(h_f, h_b, proj, ml_norm[l])

        z = _merge((br_na, br_s5, br_wg, br_ml), w_branch_bf, l, gates, rows, rows // 8)
        xs = _matmul_norm_residual(z, w_out_bf, xs, g_mix_post[l], mod4, l, 2, n_tiles, 1024)

        h2 = _norm_mod(xs, g_ffn_pre[l], mod4, l, 3, 4, n_tiles)
        act = _ffn_up(h2, w_up, l, ffn_conv_w[l], ffn_conv_b[l], rows, rows // 8)
        xs = _matmul_norm_residual(act, w_down_bf, xs, g_ffn_post[l], mod4, l, 5, n_tiles, 512)

    return xs.reshape(BATCH, SEQ, D_MODEL)
```

```python
import functools

import jax
import jax.numpy as jnp
from jax import lax
from jax.experimental import pallas as pl
from jax.experimental.pallas import tpu as pltpu

F32 = jnp.float32
BF16 = jnp.bfloat16

D_MODEL = 2048
BATCH = 2
SEQ = 4096
DEPTH = 2
GRID_W = 64
CTX_LEN = 256
HEAD_DIM = 128
BRANCH_W = 512
N_BRANCH = 4
NA_HEADS = 4
NA_WIN_ROWS = 8
NA_WIN_COLS = 16
S5_GROUP = 16
S5_GROUPS = BRANCH_W // S5_GROUP
S5_STATE = 64
S5_CH = S5_GROUPS * S5_STATE
WG_Q_HEADS = 4
WG_KV_HEADS = 2
WG_WINDOW = 128
WG_BLOCK = 128
ML_HEADS = 4
ML_CHUNK = 128
FFN_DIM = 5632
ROPE_BASE = 10000.0
EPS = 1e-6
NEG_INF = -1e30

T_LAT = BATCH * SEQ
T_CTX = BATCH * CTX_LEN
T_ALL = T_LAT + T_CTX
GRID_ROWS = SEQ // GRID_W
ATT_SCALE = HEAD_DIM ** -0.5

COL_NAQ, COL_NAK, COL_NAV, COL_S5 = 0, 512, 1024, 1536
COL_WGQ, COL_WGK, COL_WGV = 2048, 2560, 2816
COL_MLQ, COL_MLK, COL_MLV, COL_MLO = 3072, 3584, 4096, 4608
PROJ_W = 5120
COL_MLG = PROJ_W
COL_GATE = PROJ_W + 4 * ML_HEADS
GATE_W = N_BRANCH * D_MODEL

MIB = 1024 * 1024
TOK_TM = 512
N_LAT_TILES = T_LAT // TOK_TM
N_ALL_TILES = T_ALL // TOK_TM
ATT_QBLK = 256
NA_ROWS_PER_STEP = ATT_QBLK // GRID_W
S5_CHUNK = 256
S5_SUB = 8


def _cparams(vmem_mib=None):
    if vmem_mib is None:
        return None
    return pltpu.CompilerParams(vmem_limit_bytes=vmem_mib * MIB)


def _dot(a, b):
    return jnp.dot(a, b, preferred_element_type=F32)


def _dot_nt(a, b):
    return lax.dot_general(a, b, (((1,), (1,)), ((), ())), preferred_element_type=F32)


def _mod_row(i):
    return jnp.where(i >= N_LAT_TILES, BATCH, i // (N_LAT_TILES // BATCH))


def _mod_spec(layer, chunk):
    return pl.BlockSpec((None, None, 1, D_MODEL), lambda i, *_: (layer, _mod_row(i), 0, chunk))


def _seq_block(b, j, blk, rev):
    n_ctx = CTX_LEN // blk
    n_lat = SEQ // blk
    if rev:
        ctx = T_LAT // blk + b * n_ctx + (n_ctx - 1 - j)
        lat = b * n_lat + (n_lat - 1 - (j - n_ctx))
    else:
        ctx = T_LAT // blk + b * n_ctx + j
        lat = b * n_lat + (j - n_ctx)
    return jnp.where(j < n_ctx, ctx, lat)


def _ada_kernel(c_ref, w_ref, b_ref, o_ref):
    c = c_ref[...]
    s = (c * jax.nn.sigmoid(c)).astype(BF16)
    o_ref[...] = _dot(s, w_ref[...].astype(BF16)) + b_ref[...]


def _ada(cs, w_ada, b_ada):
    tn = 1024
    n_out = 6 * D_MODEL
    return pl.pallas_call(
        _ada_kernel,
        out_shape=jax.ShapeDtypeStruct((DEPTH, 8, n_out), F32),
        grid=(DEPTH, n_out // tn),
        in_specs=[
            pl.BlockSpec((8, D_MODEL), lambda l, j: (0, 0)),
            pl.BlockSpec((None, D_MODEL, tn), lambda l, j: (l, 0, j)),
            pl.BlockSpec((None, 1, tn), lambda l, j: (l, 0, j)),
        ],
        out_specs=pl.BlockSpec((None, 8, tn), lambda l, j: (l, 0, j)),
        compiler_params=_cparams(40),
        name="ada",
    )(cs, w_ada, b_ada.reshape(DEPTH, 1, n_out))


def _x_specs(x_pair, n_tiles):
    if isinstance(x_pair, tuple):
        lat, ctx = x_pair
        last = N_LAT_TILES - 1
        specs = [pl.BlockSpec((TOK_TM, D_MODEL), lambda i, *_: (jnp.minimum(i, last), 0)),
                 pl.BlockSpec((TOK_TM, D_MODEL), lambda i, *_: (0, 0))]
        return specs, [lat, ctx]
    return [pl.BlockSpec((TOK_TM, D_MODEL), lambda i, *_: (i, 0))], [x_pair]


def _for_tile_source(x_refs, fn):
    if len(x_refs) == 1:
        fn(x_refs[0])
        return
    i = pl.program_id(0)
    pl.when(i < N_LAT_TILES)(lambda: fn(x_refs[0]))
    pl.when(i >= N_LAT_TILES)(lambda: fn(x_refs[1]))


def _norm_mod_kernel(*refs):
    *x_refs, g_ref, sh_ref, sc_ref, o_ref = refs

    def body(x_ref):
        x = x_ref[...]
        y = x * lax.rsqrt(jnp.mean(x * x, axis=-1, keepdims=True) + EPS) * g_ref[...]
        o_ref[...] = (y * (1.0 + sc_ref[...]) + sh_ref[...]).astype(BF16)

    _for_tile_source(x_refs, body)


def _norm_mod(x_pair, g, mod4, layer, shift_chunk, scale_chunk, n_tiles):
    x_specs, x_args = _x_specs(x_pair, n_tiles)
    return pl.pallas_call(
        _norm_mod_kernel,
        out_shape=jax.ShapeDtypeStruct((n_tiles * TOK_TM, D_MODEL), BF16),
        grid=(n_tiles,),
        in_specs=x_specs + [pl.BlockSpec((1, D_MODEL), lambda i: (0, 0)),
                            _mod_spec(layer, shift_chunk), _mod_spec(layer, scale_chunk)],
        out_specs=pl.BlockSpec((TOK_TM, D_MODEL), lambda i: (i, 0)),
        name="norm_mod",
    )(*x_args, g.reshape(1, D_MODEL), mod4, mod4)


def _mm_kernel(a_ref, w_ref, o_ref, *, act):
    r = _dot(a_ref[...], w_ref[...].astype(BF16))
    if act == "sigmoid":
        r = jax.nn.sigmoid(r)
    o_ref[...] = r.astype(o_ref.dtype)


def _matmul(a, w, n_cols, tn, out_dtype, *, tm, rows, layer=None, act=None):
    k = a.shape[1]
    if layer is None:
        w_spec = pl.BlockSpec((k, tn), lambda i, j: (0, j))
    else:
        w_spec = pl.BlockSpec((None, k, tn), lambda i, j: (layer, 0, j))
    return pl.pallas_call(
        functools.partial(_mm_kernel, act=act),
        out_shape=jax.ShapeDtypeStruct((rows, n_cols), out_dtype),
        grid=(rows // tm, n_cols // tn),
        in_specs=[pl.BlockSpec((tm, k), lambda i, j: (i, 0)), w_spec],
        out_specs=pl.BlockSpec((tm, tn), lambda i, j: (i, j)),
        compiler_params=_cparams(52),
        name="matmul",
    )(a, w)


def _att_row_block(b, s):
    n_lat = SEQ // ATT_QBLK
    return jnp.where(s == 0, T_LAT // ATT_QBLK + b, b * n_lat + s - 1)


NA_UNION_ROWS = NA_ROWS_PER_STEP + NA_WIN_ROWS - 1


def _na_kernel(q_ref, kl_ref, vl_ref, kc_ref, vc_ref, tb_ref, o_ref):
    step = pl.program_id(2)
    kc = kc_ref[...].astype(BF16)
    vc = vc_ref[...].astype(BF16)
    q = q_ref[...].astype(BF16)
    s_ctx = _dot_nt(q, kc) * ATT_SCALE

    @pl.when(step == 0)
    def _context_queries():
        p = jnp.exp(s_ctx - jnp.max(s_ctx, axis=-1, keepdims=True))
        o = _dot(p.astype(BF16), vc) / jnp.sum(p, axis=-1, keepdims=True)
        o_ref[...] = o.astype(o_ref.dtype)

    @pl.when(step > 0)
    def _latent_queries():
        rq0 = (step - 1) * NA_ROWS_PER_STEP
        u0 = jnp.clip(rq0 - NA_WIN_ROWS // 2, 0, GRID_ROWS - NA_UNION_ROWS)
        pattern = jnp.where(rq0 == 0, 0, jnp.where(rq0 == GRID_ROWS - NA_ROWS_PER_STEP, 2, 1))
        start = pl.multiple_of(u0 * GRID_W, GRID_W)
        n_win = NA_UNION_ROWS * GRID_W
        kw = kl_ref[pl.ds(start, n_win), :].astype(BF16)
        vw = vl_ref[pl.ds(start, n_win), :].astype(BF16)
        bias = tb_ref[pattern]
        s_loc = jnp.where(bias > 0.5 * NEG_INF, _dot_nt(q, kw) * ATT_SCALE + bias, NEG_INF)
        m = jnp.maximum(jnp.max(s_loc, axis=-1, keepdims=True), jnp.max(s_ctx, axis=-1, keepdims=True))
        p_loc = jnp.exp(s_loc - m)
        p_ctx = jnp.exp(s_ctx - m)
        den = jnp.sum(p_loc, axis=-1, keepdims=True) + jnp.sum(p_ctx, axis=-1, keepdims=True)
        o = (_dot(p_loc.astype(BF16), vw) + _dot(p_ctx.astype(BF16), vc)) / den
        o_ref[...] = o.astype(o_ref.dtype)


def _na_bias_table(rpb):
    assert NA_ROWS_PER_STEP == NA_WIN_ROWS // 2 and GRID_ROWS % NA_ROWS_PER_STEP == 0
    n_r, n_u = NA_ROWS_PER_STEP, NA_UNION_ROWS
    col = jnp.arange(GRID_W)
    c0 = jnp.clip(col - NA_WIN_COLS // 2, 0, GRID_W - NA_WIN_COLS)
    col_ok = (col[None, :] >= c0[:, None]) & (col[None, :] < c0[:, None] + NA_WIN_COLS)
    dc = jnp.clip(col[None, :] - col[:, None] + NA_WIN_COLS - 1, 0, 2 * NA_WIN_COLS - 2)
    per_dr = jnp.where(col_ok[None, None], rpb[:, :, dc].astype(F32), NEG_INF)
    d = jnp.arange(n_r)[:, None]
    i = jnp.arange(n_u)[None, :]
    zero = jnp.zeros_like(d)
    tabs = []
    for u_off, i0 in ((0, zero), (-(NA_WIN_ROWS // 2), d), (n_r - n_u, zero + (n_u - NA_WIN_ROWS))):
        dr = u_off + i - d + NA_WIN_ROWS - 1
        visible = (i >= i0) & (i < i0 + NA_WIN_ROWS)
        t = per_dr[:, jnp.clip(dr, 0, 2 * NA_WIN_ROWS - 2)]
        t = jnp.where(visible[None, :, :, None, None], t, NEG_INF)
        tabs.append(jnp.transpose(t, (0, 1, 3, 2, 4)).reshape(rpb.shape[0], n_r * GRID_W, n_u * GRID_W))
    return jnp.stack(tabs, axis=1)


def _neighbourhood_attention(proj, tb, layer):
    n_steps = 1 + SEQ // ATT_QBLK
    hb = HEAD_DIM
    ctx_blk = T_LAT // CTX_LEN
    return pl.pallas_call(
        _na_kernel,
        out_shape=jax.ShapeDtypeStruct((T_ALL, BRANCH_W), BF16),
        grid=(BATCH, NA_HEADS, n_steps),
        in_specs=[
            pl.BlockSpec((ATT_QBLK, hb), lambda b, h, s: (_att_row_block(b, s), COL_NAQ // hb + h)),
            pl.BlockSpec((SEQ, hb), lambda b, h, s: (b, COL_NAK // hb + h)),
            pl.BlockSpec((SEQ, hb), lambda b, h, s: (b, COL_NAV // hb + h)),
            pl.BlockSpec((CTX_LEN, hb), lambda b, h, s: (ctx_blk + b, COL_NAK // hb + h)),
            pl.BlockSpec((CTX_LEN, hb), lambda b, h, s: (ctx_blk + b, COL_NAV // hb + h)),
            pl.BlockSpec((None, 3, ATT_QBLK, NA_UNION_ROWS * GRID_W), lambda b, h, s: (layer * NA_HEADS + h, 0, 0, 0)),
        ],
        out_specs=pl.BlockSpec((ATT_QBLK, hb), lambda b, h, s: (_att_row_block(b, s), h)),
        name="na_attn",
    )(proj, proj, proj, proj, proj, tb)


def _rope(x, cos, sin_signed):
    lane = lax.broadcasted_iota(jnp.int32, x.shape, 1)
    partner = jnp.where(lane % 64 < 32, pltpu.roll(x, 96, 1), pltpu.roll(x, 32, 1))
    return x * cos + partner * sin_signed


def _wg_kernel(sink_ref, q_ref, kl_ref, vl_ref, kc_ref, vc_ref, cos_ref, sin_ref, o_ref, kr_ref):
    hk = pl.program_id(1)
    step = pl.program_id(2)
    kc = kc_ref[...].astype(BF16)
    vc = vc_ref[...].astype(BF16)
    group = WG_Q_HEADS // WG_KV_HEADS

    @pl.when(step == 0)
    def _context_queries():
        for g in range(group):
            sink = sink_ref[hk * group + g]
            q = q_ref[:, g * HEAD_DIM:(g + 1) * HEAD_DIM].astype(BF16)
            s = _dot_nt(q, kc) * ATT_SCALE
            m = jnp.maximum(jnp.max(s, axis=-1, keepdims=True), sink)
            p = jnp.exp(s - m)
            den = jnp.sum(p, axis=-1, keepdims=True) + jnp.exp(sink - m)
            o_ref[:, g * HEAD_DIM:(g + 1) * HEAD_DIM] = (_dot(p.astype(BF16), vc) / den).astype(o_ref.dtype)

    @pl.when(step == 1)
    def _rope_keys():
        kr_ref[...] = _rope(kl_ref[...], cos_ref[...], sin_ref[...]).astype(BF16)

    @pl.when(step > 0)
    def _latent_queries():
        n_win = ATT_QBLK + 2 * WG_WINDOW
        base = pl.multiple_of((step - 1) * ATT_QBLK, ATT_QBLK)
        start = pl.multiple_of(jnp.clip(base - WG_WINDOW, 0, SEQ - n_win), WG_WINDOW)
        cos_q = cos_ref[pl.ds(base, ATT_QBLK), :]
        sin_q = sin_ref[pl.ds(base, ATT_QBLK), :]
        k_win = kr_ref[pl.ds(start, n_win), :]
        v_win = vl_ref[pl.ds(start, n_win), :].astype(BF16)
        qi = lax.broadcasted_iota(jnp.int32, (ATT_QBLK, n_win), 0)
        kj = lax.broadcasted_iota(jnp.int32, (ATT_QBLK, n_win), 1)
        in_window = jnp.abs(kj - qi + (start - base)) <= WG_WINDOW
        for g in range(group):
            sink = sink_ref[hk * group + g]
            cols = slice(g * HEAD_DIM, (g + 1) * HEAD_DIM)
            q = _rope(q_ref[:, cols], cos_q, sin_q).astype(BF16)
            s_loc = jnp.where(in_window, _dot_nt(q, k_win) * ATT_SCALE, NEG_INF)
            s_ctx = _dot_nt(q, kc) * ATT_SCALE
            m = jnp.maximum(jnp.max(s_loc, axis=-1, keepdims=True), jnp.max(s_ctx, axis=-1, keepdims=True))
            m = jnp.maximum(m, sink)
            p_loc = jnp.exp(s_loc - m)
            p_ctx = jnp.exp(s_ctx - m)
            den = (jnp.sum(p_loc, axis=-1, keepdims=True) + jnp.sum(p_ctx, axis=-1, keepdims=True)
                   + jnp.exp(sink - m))
            o = (_dot(p_loc.astype(BF16), v_win) + _dot(p_ctx.astype(BF16), vc)) / den
            o_ref[:, cols] = o.astype(o_ref.dtype)


def _rope_tables():
    t = jnp.arange(SEQ)
    pos = jnp.stack([t // GRID_W, t % GRID_W], axis=-1).astype(F32)
    n_freq = HEAD_DIM // 4
    inv_freq = ROPE_BASE ** (-jnp.arange(n_freq, dtype=F32) / n_freq)
    ang = pos[:, :, None] * inv_freq
    cos, sin = jnp.cos(ang), jnp.sin(ang)
    cos_t = jnp.concatenate([cos[:, 0], cos[:, 0], cos[:, 1], cos[:, 1]], axis=-1)
    sin_t = jnp.concatenate([-sin[:, 0], sin[:, 0], -sin[:, 1], sin[:, 1]], axis=-1)
    return cos_t, sin_t


def _windowed_gqa(proj, sink, cos_t, sin_t):
    n_steps = 1 + SEQ // ATT_QBLK
    qw = (WG_Q_HEADS // WG_KV_HEADS) * HEAD_DIM
    hb = HEAD_DIM
    ctx_blk = T_LAT // CTX_LEN
    return pl.pallas_call(
        _wg_kernel,
        out_shape=jax.ShapeDtypeStruct((T_ALL, BRANCH_W), BF16),
        grid=(BATCH, WG_KV_HEADS, n_steps),
        in_specs=[
            pl.BlockSpec(memory_space=pltpu.SMEM),
            pl.BlockSpec((ATT_QBLK, qw), lambda b, h, s: (_att_row_block(b, s), COL_WGQ // qw + h)),
            pl.BlockSpec((SEQ, hb), lambda b, h, s: (b, COL_WGK // hb + h)),
            pl.BlockSpec((SEQ, hb), lambda b, h, s: (b, COL_WGV // hb + h)),
            pl.BlockSpec((CTX_LEN, hb), lambda b, h, s: (ctx_blk + b, COL_WGK // hb + h)),
            pl.BlockSpec((CTX_LEN, hb), lambda b, h, s: (ctx_blk + b, COL_WGV // hb + h)),
            pl.BlockSpec((SEQ, hb), lambda b, h, s: (0, 0)),
            pl.BlockSpec((SEQ, hb), lambda b, h, s: (0, 0)),
        ],
        out_specs=pl.BlockSpec((ATT_QBLK, qw), lambda b, h, s: (_att_row_block(b, s), h)),
        scratch_shapes=[pltpu.VMEM((SEQ, hb), BF16)],
        name="wg_attn",
    )(sink, proj, proj, proj, proj, proj, cos_t, sin_t)


def _s5_disc_kernel(lre_ref, lim_ref, lstep_ref, bre_ref, bim_ref, pre_ref, pim_ref, bbre_ref, bbim_ref):
    lre = jnp.minimum(lre_ref[...], -1e-4)
    lim = lim_ref[...]
    step = jnp.exp(lstep_ref[...])
    kk = (lax.broadcasted_iota(jnp.int32, (S5_SUB, 1), 0) + 1).astype(F32)
    mag = jnp.exp(kk * (lre * step))
    ang = kk * (lim * step)
    p_re = mag * jnp.cos(ang)
    p_im = mag * jnp.sin(ang)
    pre_ref[...] = p_re
    pim_ref[...] = p_im
    a_re = p_re[0:1, :]
    a_im = p_im[0:1, :]
    den = lre * lre + lim * lim
    f_re = ((a_re - 1.0) * lre + a_im * lim) / den
    f_im = (a_im * lre - (a_re - 1.0) * lim) / den
    br = bre_ref[...]
    bi = bim_ref[...]
    bbre_ref[...] = f_re * br - f_im * bi
    bbim_ref[...] = f_re * bi + f_im * br


def _s5_discretise(lam_re, lam_im, log_step, b_re, b_im):
    n = DEPTH * 2
    lre = lam_re.reshape(n, 1, S5_CH)
    lim = lam_im.reshape(n, 1, S5_CH)
    lstep = jnp.repeat(log_step.reshape(n, S5_GROUPS), S5_STATE, axis=-1).reshape(n, 1, S5_CH)
    br = jnp.transpose(b_re, (0, 1, 4, 2, 3)).reshape(n, S5_GROUP, S5_CH)
    bi = jnp.transpose(b_im, (0, 1, 4, 2, 3)).reshape(n, S5_GROUP, S5_CH)
    row = lambda r: pl.BlockSpec((None, r, S5_CH), lambda d: (d, 0, 0))
    outs = pl.pallas_call(
        _s5_disc_kernel,
        out_shape=(jax.ShapeDtypeStruct((n, S5_SUB, S5_CH), F32), jax.ShapeDtypeStruct((n, S5_SUB, S5_CH), F32),
                   jax.ShapeDtypeStruct((n, S5_GROUP, S5_CH), F32), jax.ShapeDtypeStruct((n, S5_GROUP, S5_CH), F32)),
        grid=(n,),
        in_specs=[row(1), row(1), row(1), row(S5_GROUP), row(S5_GROUP)],
        out_specs=(row(S5_SUB), row(S5_SUB), row(S5_GROUP), row(S5_GROUP)),
        name="s5_disc",
    )(lre, lim, lstep, br, bi)
    return [o.reshape(DEPTH, 2, *o.shape[1:]) for o in outs]


def _s5_scan_kernel(u_ref, bbre_ref, bbim_ref, cre_ref, cim_ref, tab_ref, y_ref, sre_ref, sim_ref, car_ref, *, rev):
    j = pl.program_id(1)

    @pl.when(j == 0)
    def _reset():
        car_ref[...] = jnp.zeros_like(car_ref)

    u = u_ref[...].astype(BF16)
    n_col_tiles = S5_CH // 256
    for c in range(n_col_tiles):
        ub = u[:, 128 * (c // 2):128 * (c // 2) + 128]
        sre_ref[:, 256 * c:256 * (c + 1)] = _dot(ub, bbre_ref[c])
        sim_ref[:, 256 * c:256 * (c + 1)] = _dot(ub, bbim_ref[c])

    n_groups = S5_CHUNK // S5_SUB
    last = 0 if rev else S5_SUB - 1

    def group(gi, carry):
        cr, ci = carry
        g = (n_groups - 1 - gi) if rev else gi
        r0 = pl.multiple_of(g * S5_SUB, S5_SUB)
        xr = sre_ref[pl.ds(r0, S5_SUB), :]
        xi = sim_ref[pl.ds(r0, S5_SUB), :]
        for t, k in enumerate((1, 2, 4)):
            shift = (S5_SUB - k) if rev else k
            ar = tab_ref[2 * t]
            ai = tab_ref[2 * t + 1]
            rr = pltpu.roll(xr, shift, 0)
            ri = pltpu.roll(xi, shift, 0)
            xr, xi = xr + ar * rr - ai * ri, xi + ar * ri + ai * rr
        apr = tab_ref[6]
        api = tab_ref[7]
        xr, xi = xr + apr * cr - api * ci, xi + apr * ci + api * cr
        sre_ref[pl.ds(r0, S5_SUB), :] = xr
        sim_ref[pl.ds(r0, S5_SUB), :] = xi
        return xr[last:last + 1, :], xi[last:last + 1, :]

    cr, ci = lax.fori_loop(0, n_groups, group, (car_ref[0:1, :], car_ref[1:2, :]))
    car_ref[0:1, :] = cr
    car_ref[1:2, :] = ci

    n_out_tiles = BRANCH_W // 128
    kw = S5_CH // n_out_tiles
    for oc in range(n_out_tiles):
        sr = sre_ref[:, kw * oc:kw * (oc + 1)].astype(BF16)
        si = sim_ref[:, kw * oc:kw * (oc + 1)].astype(BF16)
        y_ref[:, 128 * oc:128 * (oc + 1)] = _dot(sr, cre_ref[oc]) + _dot(si, cim_ref[oc])


def _s5_tables(p_re, p_im, bb_re, bb_im, c_re, c_im):
    row = jnp.arange(S5_SUB)
    ks = jnp.array([1, 2, 4])
    keep = jnp.stack([row[None, :] >= ks[:, None], row[None, :] <= S5_SUB - 1 - ks[:, None]]).astype(F32)

    def shift_tabs(p):
        return keep[None, :, :, :, None] * p[:, :, ks - 1][:, :, :, None, :]

    def carry_tabs(p):
        return jnp.stack([p[:, 0], p[:, 1, ::-1]], axis=1)

    s_re, s_im = shift_tabs(p_re), shift_tabs(p_im)
    tab = jnp.stack([s_re[:, :, 0], s_im[:, :, 0], s_re[:, :, 1], s_im[:, :, 1], s_re[:, :, 2], s_im[:, :, 2],
                     carry_tabs(p_re), carry_tabs(p_im)], axis=2)

    n_in = S5_CH // 256
    c_idx, gl, gj = jnp.arange(n_in)[:, None, None], jnp.arange(8)[None, :, None], jnp.arange(4)[None, None, :]
    in_mask = (gl == 4 * (c_idx % 2) + gj).astype(F32)

    def in_tiles(bb):
        x = bb.reshape(DEPTH, 2, S5_GROUP, n_in, 4, S5_STATE)
        x = jnp.transpose(x, (0, 1, 3, 2, 4, 5))
        t = in_mask[None, None, :, :, None, :, None] * x[:, :, :, None]
        return t.reshape(DEPTH, 2, n_in, 128, 256).astype(BF16)

    n_out = BRANCH_W // 128
    eye8 = jnp.eye(8, dtype=F32)

    def out_tiles(cc):
        x = cc.astype(F32).reshape(DEPTH, 2, n_out, 8, S5_GROUP, S5_STATE)
        x = jnp.transpose(x, (0, 1, 2, 5, 3, 4))
        t = eye8[None, None, None, :, None, :, None] * x[:, :, :, None]
        return t.reshape(DEPTH, 2, n_out, S5_CH // n_out, 128).astype(BF16)

    return tab, in_tiles(bb_re), in_tiles(bb_im), out_tiles(c_re), out_tiles(-c_im)


def _s5_scan(proj, tab, bbre_t, bbim_t, cre_t, cim_t, layer, rev):
    n_chunks = (CTX_LEN + SEQ) // S5_CHUNK
    blk = functools.partial(_seq_block, blk=S5_CHUNK, rev=rev)
    d = 1 if rev else 0

    def full(arr):
        shape = arr.shape[2:]
        return pl.BlockSpec((None, None) + shape, lambda b, j: (layer, d) + (0,) * len(shape))

    return pl.pallas_call(
        functools.partial(_s5_scan_kernel, rev=rev),
        out_shape=jax.ShapeDtypeStruct((T_ALL, BRANCH_W), F32),
        grid=(BATCH, n_chunks),
        in_specs=[
            pl.BlockSpec((S5_CHUNK, BRANCH_W), lambda b, j: (blk(b, j), COL_S5 // BRANCH_W)),
            full(bbre_t), full(bbim_t), full(cre_t), full(cim_t), full(tab),
        ],
        out_specs=pl.BlockSpec((S5_CHUNK, BRANCH_W), lambda b, j: (blk(b, j), 0)),
        scratch_shapes=[pltpu.VMEM((S5_CHUNK, S5_CH), F32), pltpu.VMEM((S5_CHUNK, S5_CH), F32),
                        pltpu.VMEM((8, S5_CH), F32)],
        name="s5_scan_bwd" if rev else "s5_scan_fwd",
    )(proj, bbre_t, bbim_t, cre_t, cim_t, tab)


def _s5_out_kernel(u_ref, yf_ref, yb_ref, d_ref, w_ref, b_ref, o_ref):
    y = d_ref[...] * u_ref[...] + yf_ref[...] + yb_ref[...]
    y = jax.nn.gelu(y)
    z = _dot(y.astype(BF16), w_ref[...]) + b_ref[...]
    o_ref[...] = (y * jax.nn.sigmoid(z)).astype(o_ref.dtype)


def _s5_out(proj, y_f, y_b, d_skip, w_glu, layer, b_glu):
    tm = TOK_TM
    tok = lambda cb: pl.BlockSpec((tm, BRANCH_W), lambda i: (i, cb))
    vec = pl.BlockSpec((1, BRANCH_W), lambda i: (0, 0))
    return pl.pallas_call(
        _s5_out_kernel,
        out_shape=jax.ShapeDtypeStruct((T_ALL, BRANCH_W), BF16),
        grid=(T_ALL // tm,),
        in_specs=[tok(COL_S5 // BRANCH_W), tok(0), tok(0), vec,
                  pl.BlockSpec((None, BRANCH_W, BRANCH_W), lambda i: (layer, 0, 0)), vec],
        out_specs=tok(0),
        name="s5_out",
    )(proj, y_f, y_b, d_skip.reshape(1, BRANCH_W), w_glu, b_glu.reshape(1, BRANCH_W))


def _mlstm_kernel(qf_ref, kf_ref, vf_ref, gf_ref, qb_ref, kb_ref, vb_ref, gb_ref, bias_ref, hf_ref, hb_ref,
                  cf_ref, nf_ref, mf_ref, cb_ref, nb_ref, mb_ref):
    j = pl.program_id(1)

    @pl.when(j == 0)
    def _reset():
        for ref in (cf_ref, nf_ref, mf_ref, cb_ref, nb_ref, mb_ref):
            ref[...] = jnp.zeros_like(ref)

    _mlstm_chunk(qf_ref, kf_ref, vf_ref, gf_ref, bias_ref, hf_ref, cf_ref, nf_ref, mf_ref, rev=False)
    _mlstm_chunk(qb_ref, kb_ref, vb_ref, gb_ref, bias_ref, hb_ref, cb_ref, nb_ref, mb_ref, rev=True)


def _mlstm_chunk(q_ref, k_ref, v_ref, g_ref, bias_ref, h_ref, c_ref, n_ref, m_ref, *, rev):
    L = ML_CHUNK
    state = [(c_ref[h], n_ref[h][0:1, :], m_ref[h][0:1, 0:1]) for h in range(ML_HEADS)]
    t_idx = lax.broadcasted_iota(jnp.int32, (L, L), 0)
    s_idx = lax.broadcasted_iota(jnp.int32, (L, L), 1)
    tri = (s_idx >= t_idx) if rev else (s_idx <= t_idx)
    tri_bf = jnp.where(tri, 1.0, 0.0).astype(BF16)
    ones_bf = jnp.ones((L, HEAD_DIM), BF16)

    gates = g_ref[...] + bias_ref[...]
    log_f = jax.nn.log_sigmoid(gates)
    hi = log_f.astype(BF16)
    r1 = log_f - hi.astype(F32)
    mid = r1.astype(BF16)
    lo = (r1 - mid.astype(F32)).astype(BF16)
    bcum = _dot(tri_bf, hi) + _dot(tri_bf, mid) + _dot(tri_bf, lo)
    gates_t = gates.T
    bcum_t = bcum.T
    last = 0 if rev else L - 1
    d = 1 if rev else 0

    for h in range(ML_HEADS):
        ii = (2 * d) * ML_HEADS + h
        fi = (2 * d + 1) * ML_HEADS + h
        li_r = gates_t[ii:ii + 1, :]
        bc_c = bcum[:, fi:fi + 1]
        bc_r = bcum_t[fi:fi + 1, :]
        b_last = bcum_t[fi:fi + 1, last:last + 1]
        c_prev, n_prev, m_prev = state[h]
        cols = slice(h * HEAD_DIM, (h + 1) * HEAD_DIM)
        q = q_ref[:, cols]
        k = k_ref[:, cols] * ATT_SCALE
        v = v_ref[:, cols]
        qb, kb, vb = q.astype(BF16), k.astype(BF16), v.astype(BF16)

        log_end = b_last - bc_r + li_r
        m_new = jnp.maximum(b_last + m_prev, jnp.max(log_end, axis=-1, keepdims=True))
        w_end = jnp.exp(log_end - m_new)
        decay = jnp.exp(b_last + m_prev - m_new)

        bc_full = jnp.broadcast_to(bc_c, (L, L))
        log_w = jnp.where(tri, bc_full - bc_r + li_r, NEG_INF)
        log_inter = bc_full + m_prev
        m_t = jnp.maximum(log_inter, jnp.broadcast_to(jnp.max(log_w, axis=-1, keepdims=True), (L, L)))
        w = jnp.exp(log_w - m_t)
        inter = jnp.exp(log_inter - m_t)
        sb = (_dot_nt(qb, kb) * w).astype(BF16)
        num = inter * _dot_nt(qb, c_prev.astype(BF16)) + _dot(sb, vb)
        n_rows = jnp.broadcast_to(n_prev, (L, HEAD_DIM)).astype(BF16)
        den = inter * _dot_nt(qb, n_rows) + _dot(sb, ones_bf)
        h_ref[:, cols] = num / jnp.maximum(jnp.abs(den), jnp.exp(-m_t))

        vw_t = (v.T * w_end).astype(BF16)
        c_ref[h] = decay * c_prev + _dot(vw_t, kb)
        n_new = decay * n_prev + _dot(jnp.broadcast_to(w_end, (8, L)).astype(BF16), kb)[0:1, :]
        n_ref[h] = jnp.broadcast_to(n_new, (8, HEAD_DIM))
        m_ref[h] = jnp.broadcast_to(m_new, (8, HEAD_DIM))


def _mlstm(proj, mlg, gate_bias):
    n_chunks = (CTX_LEN + SEQ) // ML_CHUNK
    w = ML_HEADS * HEAD_DIM

    def specs(rev):
        blk = functools.partial(_seq_block, blk=ML_CHUNK, rev=rev)
        tok = lambda cb: pl.BlockSpec((ML_CHUNK, w), lambda b, j: (blk(b, j), cb))
        return [tok(COL_MLQ // w), tok(COL_MLK // w), tok(COL_MLV // w),
                pl.BlockSpec((ML_CHUNK, 128), lambda b, j: (blk(b, j), 0))], tok(0)

    in_f, out_f = specs(False)
    in_b, out_b = specs(True)
    out = jax.ShapeDtypeStruct((T_ALL, w), F32)
    return pl.pallas_call(
        _mlstm_kernel,
        out_shape=(out, out),
        grid=(BATCH, n_chunks),
        in_specs=in_f + in_b + [pl.BlockSpec((1, 128), lambda b, j: (0, 0))],
        out_specs=(out_f, out_b),
        scratch_shapes=[pltpu.VMEM((ML_HEADS, HEAD_DIM, HEAD_DIM), F32), pltpu.VMEM((ML_HEADS, 8, HEAD_DIM), F32),
                        pltpu.VMEM((ML_HEADS, 8, HEAD_DIM), F32)] * 2,
        name="mlstm",
    )(proj, proj, proj, mlg, proj, proj, proj, mlg, gate_bias)


def _ml_out_kernel(hf_ref, hb_ref, o_ref, nrm_ref, out_ref):
    for h in range(ML_HEADS):
        cols = slice(h * HEAD_DIM, (h + 1) * HEAD_DIM)
        x = hf_ref[:, cols] + hb_ref[:, cols]
        y = x * lax.rsqrt(jnp.mean(x * x, axis=-1, keepdims=True) + EPS) * nrm_ref[:, cols]
        out_ref[:, cols] = (y * jax.nn.sigmoid(o_ref[:, cols])).astype(out_ref.dtype)


def _ml_out(h_f, h_b, proj, ml_norm):
    tm = TOK_TM
    w = ML_HEADS * HEAD_DIM
    tok = lambda cb: pl.BlockSpec((tm, w), lambda i: (i, cb))
    return pl.pallas_call(
        _ml_out_kernel,
        out_shape=jax.ShapeDtypeStruct((T_ALL, w), BF16),
        grid=(T_ALL // tm,),
        in_specs=[tok(0), tok(0), tok(COL_MLO // w), pl.BlockSpec((1, w), lambda i: (0, 0))],
        out_specs=tok(0),
        name="ml_out",
    )(h_f, h_b, proj, ml_norm.reshape(1, w))


def _merge_kernel(b0_ref, b1_ref, b2_ref, b3_ref, w_ref, g0_ref, g1_ref, g2_ref, g3_ref, z_ref):
    acc = g0_ref[...].astype(F32) * _dot(b0_ref[...], w_ref[0])
    acc += g1_ref[...].astype(F32) * _dot(b1_ref[...], w_ref[1])
    acc += g2_ref[...].astype(F32) * _dot(b2_ref[...], w_ref[2])
    acc += g3_ref[...].astype(F32) * _dot(b3_ref[...], w_ref[3])
    z_ref[...] = acc.astype(z_ref.dtype)


def _merge(branches, w_branch, layer, gates, rows, tm):
    tn = 512
    nj = D_MODEL // tn
    br = pl.BlockSpec((tm, BRANCH_W), lambda i, j: (i, 0))
    gate = lambda g: pl.BlockSpec((tm, tn), lambda i, j: (i, g * nj + j))
    return pl.pallas_call(
        _merge_kernel,
        out_shape=jax.ShapeDtypeStruct((rows, D_MODEL), BF16),
        grid=(rows // tm, nj),
        in_specs=[br, br, br, br, pl.BlockSpec((None, N_BRANCH, BRANCH_W, tn), lambda i, j: (layer, 0, 0, j)),
                  gate(0), gate(1), gate(2), gate(3)],
        out_specs=pl.BlockSpec((tm, tn), lambda i, j: (i, j)),
        compiler_params=_cparams(40),
        name="merge",
    )(*branches, w_branch, gates, gates, gates, gates)


def _mm_norm_res_kernel(*refs, nj, tn):
    a_ref, w_ref, *x_refs, g_ref, gt_ref, o_ref, y_ref = refs
    j = pl.program_id(1)
    y_ref[j] = _dot(a_ref[...], w_ref[...])

    @pl.when(j == nj - 1)
    def _finish():
        ss = jnp.sum(y_ref[0] * y_ref[0], axis=-1, keepdims=True)
        for jj in range(1, nj):
            ss += jnp.sum(y_ref[jj] * y_ref[jj], axis=-1, keepdims=True)
        rs = lax.rsqrt(ss * (1.0 / D_MODEL) + EPS)

        def emit(x_ref):
            for jj in range(nj):
                cols = slice(jj * tn, (jj + 1) * tn)
                o_ref[:, cols] = x_ref[:, cols] + gt_ref[:, cols] * (y_ref[jj] * rs * g_ref[:, cols])

        _for_tile_source(x_refs, emit)


def _matmul_norm_residual(a, w, x_pair, g, mod4, layer, gate_chunk, n_tiles, tn):
    k_dim = a.shape[1]
    nj = D_MODEL // tn
    x_specs, x_args = _x_specs(x_pair, n_tiles)
    return pl.pallas_call(
        functools.partial(_mm_norm_res_kernel, nj=nj, tn=tn),
        out_shape=jax.ShapeDtypeStruct((n_tiles * TOK_TM, D_MODEL), F32),
        grid=(n_tiles, nj),
        in_specs=[pl.BlockSpec((TOK_TM, k_dim), lambda i, j: (i, 0)),
                  pl.BlockSpec((None, k_dim, tn), lambda i, j: (layer, 0, j))]
                 + x_specs
                 + [pl.BlockSpec((1, D_MODEL), lambda i, j: (0, 0)), _mod_spec(layer, gate_chunk)],
        out_specs=pl.BlockSpec((TOK_TM, D_MODEL), lambda i, j: (i, 0)),
        scratch_shapes=[pltpu.VMEM((nj, TOK_TM, tn), F32)],
        compiler_params=_cparams(52),
        name="matmul_norm_res",
    )(a, w, *x_args, g.reshape(1, D_MODEL), mod4)


FFN_TF = 512
FFN_HALO = 16
FFN_ROW_BLOCKS = 4
FFN_ALIGN = 16
FFN_LAG = 2 * FFN_ALIGN


def _ffn_up_kernel(hp_ref, hm_ref, hn_ref, wa_ref, wg_ref, cwa_ref, cwg_ref, cba_ref, cbg_ref, o_ref,
                   hext_ref, ua_ref, ug_ref, *, tm):
    i = pl.program_id(0)
    j = pl.program_id(1)
    halo = FFN_HALO
    ext = tm + 2 * halo
    rb = -(-ext // (FFN_ROW_BLOCKS * FFN_ALIGN)) * FFN_ALIGN

    @pl.when(j == 0)
    def _assemble_rows():
        hext_ref[0:halo, :] = hp_ref[...]
        hext_ref[halo:halo + tm, :] = hm_ref[...]
        hext_ref[halo + tm:ext, :] = hn_ref[...]

    wa = wa_ref[...].astype(BF16)
    wg = wg_ref[...].astype(BF16)

    def conv(u_ref, cw_ref, cb_ref, lo, n, has_prev, has_next):
        prev = u_ref[pl.ds(lo + halo - 1, n), :] * has_prev
        mid = u_ref[pl.ds(lo + halo, n), :]
        nxt = u_ref[pl.ds(lo + halo + 1, n), :] * has_next
        return prev * cw_ref[0:1, :] + mid * cw_ref[1:2, :] + nxt * cw_ref[2:3, :] + cb_ref[...]

    lo = 0
    for blk in range(FFN_ROW_BLOCKS):
        rows = slice(blk * rb, min((blk + 1) * rb, ext))
        hx = hext_ref[rows, :]
        ua_ref[rows, :] = _dot(hx, wa)
        ug_ref[rows, :] = _dot(hx, wg)
        hi = tm if blk == FFN_ROW_BLOCKS - 1 else (blk + 1) * rb - FFN_LAG
        n = hi - lo
        row = i * tm + lo + lax.broadcasted_iota(jnp.int32, (n, 1), 0)
        first = (row == 0) | (row == SEQ) | (row == T_LAT) | (row == T_LAT + CTX_LEN)
        final = (row == SEQ - 1) | (row == T_LAT - 1) | (row == T_LAT + CTX_LEN - 1) | (row == T_ALL - 1)
        has_prev = jnp.where(first, 0.0, 1.0)
        has_next = jnp.where(final, 0.0, 1.0)
        a = conv(ua_ref, cwa_ref, cba_ref, lo, n, has_prev, has_next)
        g = conv(ug_ref, cwg_ref, cbg_ref, lo, n, has_prev, has_next)
        o_ref[lo:hi, :] = (a * (g * jax.nn.sigmoid(g))).astype(o_ref.dtype)
        lo = hi


def _ffn_up(h, w_up, layer, conv_w, conv_b, rows, tm):
    tf, halo = FFN_TF, FFN_HALO
    nf = FFN_DIM // tf
    hb = tm // halo
    n_halo_blocks = rows // halo
    ext = tm + 2 * halo
    assert tm % halo == 0 and rows % tm == 0
    conv_b = conv_b.reshape(1, 2 * FFN_DIM)
    return pl.pallas_call(
        functools.partial(_ffn_up_kernel, tm=tm),
        out_shape=jax.ShapeDtypeStruct((rows, FFN_DIM), BF16),
        grid=(rows // tm, nf),
        in_specs=[
            pl.BlockSpec((halo, D_MODEL), lambda i, j: (jnp.maximum(i * hb - 1, 0), 0)),
            pl.BlockSpec((tm, D_MODEL), lambda i, j: (i, 0)),
            pl.BlockSpec((halo, D_MODEL), lambda i, j: (jnp.minimum((i + 1) * hb, n_halo_blocks - 1), 0)),
            pl.BlockSpec((None, D_MODEL, tf), lambda i, j: (layer, 0, j)),
            pl.BlockSpec((None, D_MODEL, tf), lambda i, j: (layer, 0, nf + j)),
            pl.BlockSpec((3, tf), lambda i, j: (0, j)),
            pl.BlockSpec((3, tf), lambda i, j: (0, nf + j)),
            pl.BlockSpec((1, tf), lambda i, j: (0, j)),
            pl.BlockSpec((1, tf), lambda i, j: (0, nf + j)),
        ],
        out_specs=pl.BlockSpec((tm, tf), lambda i, j: (i, j)),
        scratch_shapes=[pltpu.VMEM((ext, D_MODEL), BF16), pltpu.VMEM((ext, tf), F32), pltpu.VMEM((ext, tf), F32)],
        compiler_params=_cparams(52),
        name="ffn_up",
    )(h, h, h, w_up, w_up, conv_w, conv_w, conv_b, conv_b)


def kernel(x, c, ctx, c_ctx, w_ada, b_ada, g_mix_pre, g_mix_post, g_ffn_pre, g_ffn_post, w_in, na_rpb, wg_sink,
           s5_lam_re, s5_lam_im, s5_log_step, s5_b_re, s5_b_im, s5_c_re, s5_c_im, s5_d, s5_w_glu, s5_b_glu,
           ml_gate_bias, ml_norm, w_branch, w_out, w_up, ffn_conv_w, ffn_conv_b, w_down):
    assert x.shape == (BATCH, SEQ, D_MODEL) and ctx.shape == (BATCH, CTX_LEN, D_MODEL)
    xs = (x.reshape(T_LAT, D_MODEL), ctx.reshape(T_CTX, D_MODEL))
    cs = jnp.concatenate([c, c_ctx[None, :], jnp.zeros((8 - BATCH - 1, D_MODEL), F32)], axis=0)
    mod4 = _ada(cs, w_ada, b_ada).reshape(DEPTH, 8, 1, 6 * D_MODEL)
    cos_t, sin_t = _rope_tables()

    na_tabs = _na_bias_table(na_rpb.reshape(DEPTH * NA_HEADS, 2 * NA_WIN_ROWS - 1, 2 * NA_WIN_COLS - 1))
    s5_tabs = _s5_tables(*_s5_discretise(s5_lam_re, s5_lam_im, s5_log_step, s5_b_re, s5_b_im), s5_c_re, s5_c_im)

    w_mlg = jnp.pad(w_in[:, :, COL_MLG:COL_GATE], ((0, 0), (0, 0), (0, 128 - 4 * ML_HEADS))).astype(BF16)
    w_branch_bf = w_branch.astype(BF16)
    w_out_bf = w_out.astype(BF16)
    w_down_bf = w_down.astype(BF16)
    w_glu_bf = s5_w_glu.astype(BF16)

    for l in range(DEPTH):
        ctx_out = l < DEPTH - 1
        n_tiles = N_ALL_TILES if ctx_out else N_LAT_TILES
        rows = n_tiles * TOK_TM
        big_tm = rows // 4

        w_main = w_in[l, :, :PROJ_W].astype(BF16)
        w_gate = w_in[l, :, COL_GATE:].astype(BF16)

        h = _norm_mod(xs, g_mix_pre[l], mod4, l, 0, 1, N_ALL_TILES)
        proj = _matmul(h, w_main, PROJ_W, 512, F32, tm=T_ALL // 4, rows=T_ALL)
        mlg = _matmul(h, w_mlg, 128, 128, F32, tm=T_ALL // 4, rows=T_ALL, layer=l)
        gates = _matmul(h, w_gate, GATE_W, 512, BF16, tm=big_tm, rows=rows, act="sigmoid")

        br_na = _neighbourhood_attention(proj, na_tabs, l)
        br_wg = _windowed_gqa(proj, wg_sink[l], cos_t, sin_t)

        ys = [_s5_scan(proj, *s5_tabs, l, rev=dr == 1) for dr in range(2)]
        br_s5 = _s5_out(proj, ys[0], ys[1], s5_d[l], w_glu_bf, l, s5_b_glu[l])

        gate_bias = jnp.pad(ml_gate_bias[l].reshape(1, 4 * ML_HEADS), ((0, 0), (0, 128 - 4 * ML_HEADS)))
        h_f, h_b = _mlstm(proj, mlg, gate_bias)
        br_ml = _ml_out(h_f, h_b, proj, ml_norm[l])

        z = _merge((br_na, br_s5, br_wg, br_ml), w_branch_bf, l, gates, rows, rows // 8)
        xs = _matmul_norm_residual(z, w_out_bf, xs, g_mix_post[l], mod4, l, 2, n_tiles, 1024)

        h2 = _norm_mod(xs, g_ffn_pre[l], mod4, l, 3, 4, n_tiles)
        act = _ffn_up(h2, w_up, l, ffn_conv_w[l], ffn_conv_b[l], rows, rows // 8)
        xs = _matmul_norm_residual(act, w_down_bf, xs, g_ffn_post[l], mod4, l, 5, n_tiles, 512)

    return xs.reshape(BATCH, SEQ, D_MODEL)
```

```python
import functools

import jax
import jax.numpy as jnp
from jax import lax
from jax.experimental import pallas as pl
from jax.experimental.pallas import tpu as pltpu

F32 = jnp.float32
BF16 = jnp.bfloat16

D_MODEL = 2048
BATCH = 2
SEQ = 4096
DEPTH = 2
GRID_W = 64
CTX_LEN = 256
HEAD_DIM = 128
BRANCH_W = 512
N_BRANCH = 4
NA_HEADS = 4
NA_WIN_ROWS = 8
NA_WIN_COLS = 16
S5_GROUP = 16
S5_GROUPS = BRANCH_W // S5_GROUP
S5_STATE = 64
S5_CH = S5_GROUPS * S5_STATE
WG_Q_HEADS = 4
WG_KV_HEADS = 2
WG_WINDOW = 128
WG_BLOCK = 128
ML_HEADS = 4
ML_CHUNK = 128
FFN_DIM = 5632
ROPE_BASE = 10000.0
EPS = 1e-6
NEG_INF = -1e30

T_LAT = BATCH * SEQ
T_CTX = BATCH * CTX_LEN
T_ALL = T_LAT + T_CTX
GRID_ROWS = SEQ // GRID_W
ATT_SCALE = HEAD_DIM ** -0.5

COL_NAQ, COL_NAK, COL_NAV, COL_S5 = 0, 512, 1024, 1536
COL_WGQ, COL_WGK, COL_WGV = 2048, 2560, 2816
COL_MLQ, COL_MLK, COL_MLV, COL_MLO = 3072, 3584, 4096, 4608
PROJ_W = 5120
COL_MLG = PROJ_W
COL_GATE = PROJ_W + 4 * ML_HEADS
GATE_W = N_BRANCH * D_MODEL

MIB = 1024 * 1024
TOK_TM = 512
N_LAT_TILES = T_LAT // TOK_TM
N_ALL_TILES = T_ALL // TOK_TM
ATT_QBLK = 256
NA_ROWS_PER_STEP = ATT_QBLK // GRID_W
S5_CHUNK = 256
S5_SUB = 8


def _cparams(vmem_mib=None):
    if vmem_mib is None:
        return None
    return pltpu.CompilerParams(vmem_limit_bytes=vmem_mib * MIB)


def _dot(a, b):
    return jnp.dot(a, b, preferred_element_type=F32)


def _dot_nt(a, b):
    return lax.dot_general(a, b, (((1,), (1,)), ((), ())), preferred_element_type=F32)


def _mod_row(i):
    return jnp.where(i >= N_LAT_TILES, BATCH, i // (N_LAT_TILES // BATCH))


def _mod_spec(layer, chunk):
    return pl.BlockSpec((None, None, 1, D_MODEL), lambda i, *_: (layer, _mod_row(i), 0, chunk))


def _seq_block(b, j, blk, rev):
    n_ctx = CTX_LEN // blk
    n_lat = SEQ // blk
    if rev:
        ctx = T_LAT // blk + b * n_ctx + (n_ctx - 1 - j)
        lat = b * n_lat + (n_lat - 1 - (j - n_ctx))
    else:
        ctx = T_LAT // blk + b * n_ctx + j
        lat = b * n_lat + (j - n_ctx)
    return jnp.where(j < n_ctx, ctx, lat)


def _ada_kernel(c_ref, w_ref, b_ref, o_ref):
    c = c_ref[...]
    s = (c * jax.nn.sigmoid(c)).astype(BF16)
    o_ref[...] = _dot(s, w_ref[...].astype(BF16)) + b_ref[...]


def _ada(cs, w_ada, b_ada):
    tn = 1024
    n_out = 6 * D_MODEL
    return pl.pallas_call(
        _ada_kernel,
        out_shape=jax.ShapeDtypeStruct((DEPTH, 8, n_out), F32),
        grid=(DEPTH, n_out // tn),
        in_specs=[
            pl.BlockSpec((8, D_MODEL), lambda l, j: (0, 0)),
            pl.BlockSpec((None, D_MODEL, tn), lambda l, j: (l, 0, j)),
            pl.BlockSpec((None, 1, tn), lambda l, j: (l, 0, j)),
        ],
        out_specs=pl.BlockSpec((None, 8, tn), lambda l, j: (l, 0, j)),
        compiler_params=_cparams(40),
        name="ada",
    )(cs, w_ada, b_ada.reshape(DEPTH, 1, n_out))


def _x_specs(x_pair, n_tiles):
    if isinstance(x_pair, tuple):
        lat, ctx = x_pair
        last = N_LAT_TILES - 1
        specs = [pl.BlockSpec((TOK_TM, D_MODEL), lambda i, *_: (jnp.minimum(i, last), 0)),
                 pl.BlockSpec((TOK_TM, D_MODEL), lambda i, *_: (0, 0))]
        return specs, [lat, ctx]
    return [pl.BlockSpec((TOK_TM, D_MODEL), lambda i, *_: (i, 0))], [x_pair]


def _for_tile_source(x_refs, fn):
    if len(x_refs) == 1:
        fn(x_refs[0])
        return
    i = pl.program_id(0)
    pl.when(i < N_LAT_TILES)(lambda: fn(x_refs[0]))
    pl.when(i >= N_LAT_TILES)(lambda: fn(x_refs[1]))


def _norm_mod_kernel(*refs):
    *x_refs, g_ref, sh_ref, sc_ref, o_ref = refs

    def body(x_ref):
        x = x_ref[...]
        y = x * lax.rsqrt(jnp.mean(x * x, axis=-1, keepdims=True) + EPS) * g_ref[...]
        o_ref[...] = (y * (1.0 + sc_ref[...]) + sh_ref[...]).astype(BF16)

    _for_tile_source(x_refs, body)


def _norm_mod(x_pair, g, mod4, layer, shift_chunk, scale_chunk, n_tiles):
    x_specs, x_args = _x_specs(x_pair, n_tiles)
    return pl.pallas_call(
        _norm_mod_kernel,
        out_shape=jax.ShapeDtypeStruct((n_tiles * TOK_TM, D_MODEL), BF16),
        grid=(n_tiles,),
        in_specs=x_specs + [pl.BlockSpec((1, D_MODEL), lambda i: (0, 0)),
                            _mod_spec(layer, shift_chunk), _mod_spec(layer, scale_chunk)],
        out_specs=pl.BlockSpec((TOK_TM, D_MODEL), lambda i: (i, 0)),
        name="norm_mod",
    )(*x_args, g.reshape(1, D_MODEL), mod4, mod4)


def _mm_kernel(a_ref, w_ref, o_ref, *, act):
    r = _dot(a_ref[...], w_ref[...].astype(BF16))
    if act == "sigmoid":
        r = jax.nn.sigmoid(r)
    o_ref[...] = r.astype(o_ref.dtype)


def _matmul(a, w, n_cols, tn, out_dtype, *, tm, rows, layer=None, act=None):
    k = a.shape[1]
    if layer is None:
        w_spec = pl.BlockSpec((k, tn), lambda i, j: (0, j))
    else:
        w_spec = pl.BlockSpec((None, k, tn), lambda i, j: (layer, 0, j))
    return pl.pallas_call(
        functools.partial(_mm_kernel, act=act),
        out_shape=jax.ShapeDtypeStruct((rows, n_cols), out_dtype),
        grid=(rows // tm, n_cols // tn),
        in_specs=[pl.BlockSpec((tm, k), lambda i, j: (i, 0)), w_spec],
        out_specs=pl.BlockSpec((tm, tn), lambda i, j: (i, j)),
        compiler_params=_cparams(52),
        name="matmul",
    )(a, w)


def _att_row_block(b, s):
    n_lat = SEQ // ATT_QBLK
    return jnp.where(s == 0, T_LAT // ATT_QBLK + b, b * n_lat + s - 1)


NA_UNION_ROWS = NA_ROWS_PER_STEP + NA_WIN_ROWS - 1


def _na_kernel(q_ref, kl_ref, vl_ref, kc_ref, vc_ref, tb_ref, o_ref):
    step = pl.program_id(2)
    kc = kc_ref[...].astype(BF16)
    vc = vc_ref[...].astype(BF16)
    q = q_ref[...].astype(BF16)
    s_ctx = _dot_nt(q, kc) * ATT_SCALE

    @pl.when(step == 0)
    def _context_queries():
        p = jnp.exp(s_ctx - jnp.max(s_ctx, axis=-1, keepdims=True))
        o = _dot(p.astype(BF16), vc) / jnp.sum(p, axis=-1, keepdims=True)
        o_ref[...] = o.astype(o_ref.dtype)

    @pl.when(step > 0)
    def _latent_queries():
        rq0 = (step - 1) * NA_ROWS_PER_STEP
        u0 = jnp.clip(rq0 - NA_WIN_ROWS // 2, 0, GRID_ROWS - NA_UNION_ROWS)
        pattern = jnp.where(rq0 == 0, 0, jnp.where(rq0 == GRID_ROWS - NA_ROWS_PER_STEP, 2, 1))
        start = pl.multiple_of(u0 * GRID_W, GRID_W)
        n_win = NA_UNION_ROWS * GRID_W
        kw = kl_ref[pl.ds(start, n_win), :].astype(BF16)
        vw = vl_ref[pl.ds(start, n_win), :].astype(BF16)
        bias = tb_ref[pattern]
        s_loc = jnp.where(bias > 0.5 * NEG_INF, _dot_nt(q, kw) * ATT_SCALE + bias, NEG_INF)
        m = jnp.maximum(jnp.max(s_loc, axis=-1, keepdims=True), jnp.max(s_ctx, axis=-1, keepdims=True))
        p_loc = jnp.exp(s_loc - m)
        p_ctx = jnp.exp(s_ctx - m)
        den = jnp.sum(p_loc, axis=-1, keepdims=True) + jnp.sum(p_ctx, axis=-1, keepdims=True)
        o = (_dot(p_loc.astype(BF16), vw) + _dot(p_ctx.astype(BF16), vc)) / den
        o_ref[...] = o.astype(o_ref.dtype)


def _na_bias_table(rpb):
    assert NA_ROWS_PER_STEP == NA_WIN_ROWS // 2 and GRID_ROWS % NA_ROWS_PER_STEP == 0
    n_r, n_u = NA_ROWS_PER_STEP, NA_UNION_ROWS
    col = jnp.arange(GRID_W)
    c0 = jnp.clip(col - NA_WIN_COLS // 2, 0, GRID_W - NA_WIN_COLS)
    col_ok = (col[None, :] >= c0[:, None]) & (col[None, :] < c0[:, None] + NA_WIN_COLS)
    dc = jnp.clip(col[None, :] - col[:, None] + NA_WIN_COLS - 1, 0, 2 * NA_WIN_COLS - 2)
    per_dr = jnp.where(col_ok[None, None], rpb[:, :, dc].astype(F32), NEG_INF)
    d = jnp.arange(n_r)[:, None]
    i = jnp.arange(n_u)[None, :]
    zero = jnp.zeros_like(d)
    tabs = []
    for u_off, i0 in ((0, zero), (-(NA_WIN_ROWS // 2), d), (n_r - n_u, zero + (n_u - NA_WIN_ROWS))):
        dr = u_off + i - d + NA_WIN_ROWS - 1
        visible = (i >= i0) & (i < i0 + NA_WIN_ROWS)
        t = per_dr[:, jnp.clip(dr, 0, 2 * NA_WIN_ROWS - 2)]
        t = jnp.where(visible[None, :, :, None, None], t, NEG_INF)
        tabs.append(jnp.transpose(t, (0, 1, 3, 2, 4)).reshape(rpb.shape[0], n_r * GRID_W, n_u * GRID_W))
    return jnp.stack(tabs, axis=1)


def _neighbourhood_attention(proj, tb, layer):
    n_steps = 1 + SEQ // ATT_QBLK
    hb = HEAD_DIM
    ctx_blk = T_LAT // CTX_LEN
    return pl.pallas_call(
        _na_kernel,
        out_shape=jax.ShapeDtypeStruct((T_ALL, BRANCH_W), BF16),
        grid=(BATCH, NA_HEADS, n_steps),
        in_specs=[
            pl.BlockSpec((ATT_QBLK, hb), lambda b, h, s: (_att_row_block(b, s), COL_NAQ // hb + h)),
            pl.BlockSpec((SEQ, hb), lambda b, h, s: (b, COL_NAK // hb + h)),
            pl.BlockSpec((SEQ, hb), lambda b, h, s: (b, COL_NAV // hb + h)),
            pl.BlockSpec((CTX_LEN, hb), lambda b, h, s: (ctx_blk + b, COL_NAK // hb + h)),
            pl.BlockSpec((CTX_LEN, hb), lambda b, h, s: (ctx_blk + b, COL_NAV // hb + h)),
            pl.BlockSpec((None, 3, ATT_QBLK, NA_UNION_ROWS * GRID_W), lambda b, h, s: (layer * NA_HEADS + h, 0, 0, 0)),
        ],
        out_specs=pl.BlockSpec((ATT_QBLK, hb), lambda b, h, s: (_att_row_block(b, s), h)),
        name="na_attn",
    )(proj, proj, proj, proj, proj, tb)


def _rope(x, cos, sin_signed):
    lane = lax.broadcasted_iota(jnp.int32, x.shape, 1)
    partner = jnp.where(lane % 64 < 32, pltpu.roll(x, 96, 1), pltpu.roll(x, 32, 1))
    return x * cos + partner * sin_signed


def _wg_kernel(sink_ref, q_ref, kl_ref, vl_ref, kc_ref, vc_ref, cos_ref, sin_ref, o_ref, kr_ref):
    hk = pl.program_id(1)
    step = pl.program_id(2)
    kc = kc_ref[...].astype(BF16)
    vc = vc_ref[...].astype(BF16)
    group = WG_Q_HEADS // WG_KV_HEADS

    @pl.when(step == 0)
    def _context_queries():
        for g in range(group):
            sink = sink_ref[hk * group + g]
            q = q_ref[:, g * HEAD_DIM:(g + 1) * HEAD_DIM].astype(BF16)
            s = _dot_nt(q, kc) * ATT_SCALE
            m = jnp.maximum(jnp.max(s, axis=-1, keepdims=True), sink)
            p = jnp.exp(s - m)
            den = jnp.sum(p, axis=-1, keepdims=True) + jnp.exp(sink - m)
            o_ref[:, g * HEAD_DIM:(g + 1) * HEAD_DIM] = (_dot(p.astype(BF16), vc) / den).astype(o_ref.dtype)

    @pl.when(step == 1)
    def _rope_keys():
        kr_ref[...] = _rope(kl_ref[...], cos_ref[...], sin_ref[...]).astype(BF16)

    @pl.when(step > 0)
    def _latent_queries():
        n_win = ATT_QBLK + 2 * WG_WINDOW
        base = pl.multiple_of((step - 1) * ATT_QBLK, ATT_QBLK)
        start = pl.multiple_of(jnp.clip(base - WG_WINDOW, 0, SEQ - n_win), WG_WINDOW)
        cos_q = cos_ref[pl.ds(base, ATT_QBLK), :]
        sin_q = sin_ref[pl.ds(base, ATT_QBLK), :]
        k_win = kr_ref[pl.ds(start, n_win), :]
        v_win = vl_ref[pl.ds(start, n_win), :].astype(BF16)
        qi = lax.broadcasted_iota(jnp.int32, (ATT_QBLK, n_win), 0)
        kj = lax.broadcasted_iota(jnp.int32, (ATT_QBLK, n_win), 1)
        in_window = jnp.abs(kj - qi + (start - base)) <= WG_WINDOW
        for g in range(group):
            sink = sink_ref[hk * group + g]
            cols = slice(g * HEAD_DIM, (g + 1) * HEAD_DIM)
            q = _rope(q_ref[:, cols], cos_q, sin_q).astype(BF16)
            s_loc = jnp.where(in_window, _dot_nt(q, k_win) * ATT_SCALE, NEG_INF)
            s_ctx = _dot_nt(q, kc) * ATT_SCALE
            m = jnp.maximum(jnp.max(s_loc, axis=-1, keepdims=True), jnp.max(s_ctx, axis=-1, keepdims=True))
            m = jnp.maximum(m, sink)
            p_loc = jnp.exp(s_loc - m)
            p_ctx = jnp.exp(s_ctx - m)
            den = (jnp.sum(p_loc, axis=-1, keepdims=True) + jnp.sum(p_ctx, axis=-1, keepdims=True)
                   + jnp.exp(sink - m))
            o = (_dot(p_loc.astype(BF16), v_win) + _dot(p_ctx.astype(BF16), vc)) / den
            o_ref[:, cols] = o.astype(o_ref.dtype)


def _rope_tables():
    t = jnp.arange(SEQ)
    pos = jnp.stack([t // GRID_W, t % GRID_W], axis=-1).astype(F32)
    n_freq = HEAD_DIM // 4
    inv_freq = ROPE_BASE ** (-jnp.arange(n_freq, dtype=F32) / n_freq)
    ang = pos[:, :, None] * inv_freq
    cos, sin = jnp.cos(ang), jnp.sin(ang)
    cos_t = jnp.concatenate([cos[:, 0], cos[:, 0], cos[:, 1], cos[:, 1]], axis=-1)
    sin_t = jnp.concatenate([-sin[:, 0], sin[:, 0], -sin[:, 1], sin[:, 1]], axis=-1)
    return cos_t, sin_t


def _windowed_gqa(proj, sink, cos_t, sin_t):
    n_steps = 1 + SEQ // ATT_QBLK
    qw = (WG_Q_HEADS // WG_KV_HEADS) * HEAD_DIM
    hb = HEAD_DIM
    ctx_blk = T_LAT // CTX_LEN
    return pl.pallas_call(
        _wg_kernel,
        out_shape=jax.ShapeDtypeStruct((T_ALL, BRANCH_W), BF16),
        grid=(BATCH, WG_KV_HEADS, n_steps),
        in_specs=[
            pl.BlockSpec(memory_space=pltpu.SMEM),
            pl.BlockSpec((ATT_QBLK, qw), lambda b, h, s: (_att_row_block(b, s), COL_WGQ // qw + h)),
            pl.BlockSpec((SEQ, hb), lambda b, h, s: (b, COL_WGK // hb + h)),
            pl.BlockSpec((SEQ, hb), lambda b, h, s: (b, COL_WGV // hb + h)),
            pl.BlockSpec((CTX_LEN, hb), lambda b, h, s: (ctx_blk + b, COL_WGK // hb + h)),
            pl.BlockSpec((CTX_LEN, hb), lambda b, h, s: (ctx_blk + b, COL_WGV // hb + h)),
            pl.BlockSpec((SEQ, hb), lambda b, h, s: (0, 0)),
            pl.BlockSpec((SEQ, hb), lambda b, h, s: (0, 0)),
        ],
        out_specs=pl.BlockSpec((ATT_QBLK, qw), lambda b, h, s: (_att_row_block(b, s), h)),
        scratch_shapes=[pltpu.VMEM((SEQ, hb), BF16)],
        name="wg_attn",
    )(sink, proj, proj, proj, proj, proj, cos_t, sin_t)


def _s5_disc_kernel(lre_ref, lim_ref, lstep_ref, bre_ref, bim_ref, pre_ref, pim_ref, bbre_ref, bbim_ref):
    lre = jnp.minimum(lre_ref[...], -1e-4)
    lim = lim_ref[...]
    step = jnp.exp(lstep_ref[...])
    kk = (lax.broadcasted_iota(jnp.int32, (S5_SUB, 1), 0) + 1).astype(F32)
    mag = jnp.exp(kk * (lre * step))
    ang = kk * (lim * step)
    p_re = mag * jnp.cos(ang)
    p_im = mag * jnp.sin(ang)
    pre_ref[...] = p_re
    pim_ref[...] = p_im
    a_re = p_re[0:1, :]
    a_im = p_im[0:1, :]
    den = lre * lre + lim * lim
    f_re = ((a_re - 1.0) * lre + a_im * lim) / den
    f_im = (a_im * lre - (a_re - 1.0) * lim) / den
    br = bre_ref[...]
    bi = bim_ref[...]
    bbre_ref[...] = f_re * br - f_im * bi
    bbim_ref[...] = f_re * bi + f_im * br


def _s5_discretise(lam_re, lam_im, log_step, b_re, b_im):
    n = DEPTH * 2
    lre = lam_re.reshape(n, 1, S5_CH)
    lim = lam_im.reshape(n, 1, S5_CH)
    lstep = jnp.repeat(log_step.reshape(n, S5_GROUPS), S5_STATE, axis=-1).reshape(n, 1, S5_CH)
    br = jnp.transpose(b_re, (0, 1, 4, 2, 3)).reshape(n, S5_GROUP, S5_CH)
    bi = jnp.transpose(b_im, (0, 1, 4, 2, 3)).reshape(n, S5_GROUP, S5_CH)
    row = lambda r: pl.BlockSpec((None, r, S5_CH), lambda d: (d, 0, 0))
    outs = pl.pallas_call(
        _s5_disc_kernel,
        out_shape=(jax.ShapeDtypeStruct((n, S5_SUB, S5_CH), F32), jax.ShapeDtypeStruct((n, S5_SUB, S5_CH), F32),
                   jax.ShapeDtypeStruct((n, S5_GROUP, S5_CH), F32), jax.ShapeDtypeStruct((n, S5_GROUP, S5_CH), F32)),
        grid=(n,),
        in_specs=[row(1), row(1), row(1), row(S5_GROUP), row(S5_GROUP)],
        out_specs=(row(S5_SUB), row(S5_SUB), row(S5_GROUP), row(S5_GROUP)),
        name="s5_disc",
    )(lre, lim, lstep, br, bi)
    return [o.reshape(DEPTH, 2, *o.shape[1:]) for o in outs]


def _s5_scan_kernel(u_ref, bbre_ref, bbim_ref, cre_ref, cim_ref, tab_ref, y_ref, sre_ref, sim_ref, car_ref, *, rev):
    j = pl.program_id(1)

    @pl.when(j == 0)
    def _reset():
        car_ref[...] = jnp.zeros_like(car_ref)

    u = u_ref[...].astype(BF16)
    n_col_tiles = S5_CH // 256
    for c in range(n_col_tiles):
        ub = u[:, 128 * (c // 2):128 * (c // 2) + 128]
        sre_ref[:, 256 * c:256 * (c + 1)] = _dot(ub, bbre_ref[c])
        sim_ref[:, 256 * c:256 * (c + 1)] = _dot(ub, bbim_ref[c])

    n_groups = S5_CHUNK // S5_SUB
    last = 0 if rev else S5_SUB - 1

    def group(gi, carry):
        cr, ci = carry
        g = (n_groups - 1 - gi) if rev else gi
        r0 = pl.multiple_of(g * S5_SUB, S5_SUB)
        xr = sre_ref[pl.ds(r0, S5_SUB), :]
        xi = sim_ref[pl.ds(r0, S5_SUB), :]
        for t, k in enumerate((1, 2, 4)):
            shift = (S5_SUB - k) if rev else k
            ar = tab_ref[2 * t]
            ai = tab_ref[2 * t + 1]
            rr = pltpu.roll(xr, shift, 0)
            ri = pltpu.roll(xi, shift, 0)
            xr, xi = xr + ar * rr - ai * ri, xi + ar * ri + ai * rr
        apr = tab_ref[6]
        api = tab_ref[7]
        xr, xi = xr + apr * cr - api * ci, xi + apr * ci + api * cr
        sre_ref[pl.ds(r0, S5_SUB), :] = xr
        sim_ref[pl.ds(r0, S5_SUB), :] = xi
        return xr[last:last + 1, :], xi[last:last + 1, :]

    cr, ci = lax.fori_loop(0, n_groups, group, (car_ref[0:1, :], car_ref[1:2, :]))
    car_ref[0:1, :] = cr
    car_ref[1:2, :] = ci

    n_out_tiles = BRANCH_W // 128
    kw = S5_CH // n_out_tiles
    for oc in range(n_out_tiles):
        sr = sre_ref[:, kw * oc:kw * (oc + 1)].astype(BF16)
        si = sim_ref[:, kw * oc:kw * (oc + 1)].astype(BF16)
        y_ref[:, 128 * oc:128 * (oc + 1)] = _dot(sr, cre_ref[oc]) + _dot(si, cim_ref[oc])


def _s5_tables(p_re, p_im, bb_re, bb_im, c_re, c_im):
    row = jnp.arange(S5_SUB)
    ks = jnp.array([1, 2, 4])
    keep = jnp.stack([row[None, :] >= ks[:, None], row[None, :] <= S5_SUB - 1 - ks[:, None]]).astype(F32)

    def shift_tabs(p):
        return keep[None, :, :, :, None] * p[:, :, ks - 1][:, :, :, None, :]

    def carry_tabs(p):
        return jnp.stack([p[:, 0], p[:, 1, ::-1]], axis=1)

    s_re, s_im = shift_tabs(p_re), shift_tabs(p_im)
    tab = jnp.stack([s_re[:, :, 0], s_im[:, :, 0], s_re[:, :, 1], s_im[:, :, 1], s_re[:, :, 2], s_im[:, :, 2],
                     carry_tabs(p_re), carry_tabs(p_im)], axis=2)

    n_in = S5_CH // 256
    c_idx, gl, gj = jnp.arange(n_in)[:, None, None], jnp.arange(8)[None, :, None], jnp.arange(4)[None, None, :]
    in_mask = (gl == 4 * (c_idx % 2) + gj).astype(F32)

    def in_tiles(bb):
        x = bb.reshape(DEPTH, 2, S5_GROUP, n_in, 4, S5_STATE)
        x = jnp.transpose(x, (0, 1, 3, 2, 4, 5))
        t = in_mask[None, None, :, :, None, :, None] * x[:, :, :, None]
        return t.reshape(DEPTH, 2, n_in, 128, 256).astype(BF16)

    n_out = BRANCH_W // 128
    eye8 = jnp.eye(8, dtype=F32)

    def out_tiles(cc):
        x = cc.astype(F32).reshape(DEPTH, 2, n_out, 8, S5_GROUP, S5_STATE)
        x = jnp.transpose(x, (0, 1, 2, 5, 3, 4))
        t = eye8[None, None, None, :, None, :, None] * x[:, :, :, None]
        return t.reshape(DEPTH, 2, n_out, S5_CH // n_out, 128).astype(BF16)

    return tab, in_tiles(bb_re), in_tiles(bb_im), out_tiles(c_re), out_tiles(-c_im)


def _s5_scan(proj, tab, bbre_t, bbim_t, cre_t, cim_t, layer, rev):
    n_chunks = (CTX_LEN + SEQ) // S5_CHUNK
    blk = functools.partial(_seq_block, blk=S5_CHUNK, rev=rev)
    d = 1 if rev else 0

    def full(arr):
        shape = arr.shape[2:]
        return pl.BlockSpec((None, None) + shape, lambda b, j: (layer, d) + (0,) * len(shape))

    return pl.pallas_call(
        functools.partial(_s5_scan_kernel, rev=rev),
        out_shape=jax.ShapeDtypeStruct((T_ALL, BRANCH_W), F32),
        grid=(BATCH, n_chunks),
        in_specs=[
            pl.BlockSpec((S5_CHUNK, BRANCH_W), lambda b, j: (blk(b, j), COL_S5 // BRANCH_W)),
            full(bbre_t), full(bbim_t), full(cre_t), full(cim_t), full(tab),
        ],
        out_specs=pl.BlockSpec((S5_CHUNK, BRANCH_W), lambda b, j: (blk(b, j), 0)),
        scratch_shapes=[pltpu.VMEM((S5_CHUNK, S5_CH), F32), pltpu.VMEM((S5_CHUNK, S5_CH), F32),
                        pltpu.VMEM((8, S5_CH), F32)],
        name="s5_scan_bwd" if rev else "s5_scan_fwd",
    )(proj, bbre_t, bbim_t, cre_t, cim_t, tab)


def _s5_out_kernel(u_ref, yf_ref, yb_ref, d_ref, w_ref, b_ref, o_ref):
    y = d_ref[...] * u_ref[...] + yf_ref[...] + yb_ref[...]
    y = jax.nn.gelu(y)
    z = _dot(y.astype(BF16), w_ref[...]) + b_ref[...]
    o_ref[...] = (y * jax.nn.sigmoid(z)).astype(o_ref.dtype)


def _s5_out(proj, y_f, y_b, d_skip, w_glu, layer, b_glu):
    tm = TOK_TM
    tok = lambda cb: pl.BlockSpec((tm, BRANCH_W), lambda i: (i, cb))
    vec = pl.BlockSpec((1, BRANCH_W), lambda i: (0, 0))
    return pl.pallas_call(
        _s5_out_kernel,
        out_shape=jax.ShapeDtypeStruct((T_ALL, BRANCH_W), BF16),
        grid=(T_ALL // tm,),
        in_specs=[tok(COL_S5 // BRANCH_W), tok(0), tok(0), vec,
                  pl.BlockSpec((None, BRANCH_W, BRANCH_W), lambda i: (layer, 0, 0)), vec],
        out_specs=tok(0),
        name="s5_out",
    )(proj, y_f, y_b, d_skip.reshape(1, BRANCH_W), w_glu, b_glu.reshape(1, BRANCH_W))


def _mlstm_kernel(qf_ref, kf_ref, vf_ref, gf_ref, qb_ref, kb_ref, vb_ref, gb_ref, bias_ref, hf_ref, hb_ref,
                  cf_ref, nf_ref, mf_ref, cb_ref, nb_ref, mb_ref):
    j = pl.program_id(1)

    @pl.when(j == 0)
    def _reset():
        for ref in (cf_ref, nf_ref, mf_ref, cb_ref, nb_ref, mb_ref):
            ref[...] = jnp.zeros_like(ref)

    _mlstm_chunk(qf_ref, kf_ref, vf_ref, gf_ref, bias_ref, hf_ref, cf_ref, nf_ref, mf_ref, rev=False)
    _mlstm_chunk(qb_ref, kb_ref, vb_ref, gb_ref, bias_ref, hb_ref, cb_ref, nb_ref, mb_ref, rev=True)


def _mlstm_chunk(q_ref, k_ref, v_ref, g_ref, bias_ref, h_ref, c_ref, n_ref, m_ref, *, rev):
    L = ML_CHUNK
    state = [(c_ref[h], n_ref[h][0:1, :], m_ref[h][0:1, 0:1]) for h in range(ML_HEADS)]
    t_idx = lax.broadcasted_iota(jnp.int32, (L, L), 0)
    s_idx = lax.broadcasted_iota(jnp.int32, (L, L), 1)
    tri = (s_idx >= t_idx) if rev else (s_idx <= t_idx)
    tri_bf = jnp.where(tri, 1.0, 0.0).astype(BF16)
    ones_bf = jnp.ones((L, HEAD_DIM), BF16)

    gates = g_ref[...] + bias_ref[...]
    log_f = jax.nn.log_sigmoid(gates)
    hi = log_f.astype(BF16)
    r1 = log_f - hi.astype(F32)
    mid = r1.astype(BF16)
    lo = (r1 - mid.astype(F32)).astype(BF16)
    bcum = _dot(tri_bf, hi) + _dot(tri_bf, mid) + _dot(tri_bf, lo)
    gates_t = gates.T
    bcum_t = bcum.T
    last = 0 if rev else L - 1
    d = 1 if rev else 0

    for h in range(ML_HEADS):
        ii = (2 * d) * ML_HEADS + h
        fi = (2 * d + 1) * ML_HEADS + h
        li_r = gates_t[ii:ii + 1, :]
        bc_c = bcum[:, fi:fi + 1]
        bc_r = bcum_t[fi:fi + 1, :]
        b_last = bcum_t[fi:fi + 1, last:last + 1]
        c_prev, n_prev, m_prev = state[h]
        cols = slice(h * HEAD_DIM, (h + 1) * HEAD_DIM)
        q = q_ref[:, cols]
        k = k_ref[:, cols] * ATT_SCALE
        v = v_ref[:, cols]
        qb, kb, vb = q.astype(BF16), k.astype(BF16), v.astype(BF16)

        log_end = b_last - bc_r + li_r
        m_new = jnp.maximum(b_last + m_prev, jnp.max(log_end, axis=-1, keepdims=True))
        w_end = jnp.exp(log_end - m_new)
        decay = jnp.exp(b_last + m_prev - m_new)

        bc_full = jnp.broadcast_to(bc_c, (L, L))
        log_w = jnp.where(tri, bc_full - bc_r + li_r, NEG_INF)
        log_inter = bc_full + m_prev
        m_t = jnp.maximum(log_inter, jnp.broadcast_to(jnp.max(log_w, axis=-1, keepdims=True), (L, L)))
        w = jnp.exp(log_w - m_t)
        inter = jnp.exp(log_inter - m_t)
        sb = (_dot_nt(qb, kb) * w).astype(BF16)
        num = inter * _dot_nt(qb, c_prev.astype(BF16)) + _dot(sb, vb)
        n_rows = jnp.broadcast_to(n_prev, (L, HEAD_DIM)).astype(BF16)
        den = inter * _dot_nt(qb, n_rows) + _dot(sb, ones_bf)
        h_ref[:, cols] = num / jnp.maximum(jnp.abs(den), jnp.exp(-m_t))

        vw_t = (v.T * w_end).astype(BF16)
        c_ref[h] = decay * c_prev + _dot(vw_t, kb)
        n_new = decay * n_prev + _dot(jnp.broadcast_to(w_end, (8, L)).astype(BF16), kb)[0:1, :]
        n_ref[h] = jnp.broadcast_to(n_new, (8, HEAD_DIM))
        m_ref[h] = jnp.broadcast_to(m_new, (8, HEAD_DIM))


def _mlstm(proj, mlg, gate_bias):
    n_chunks = (CTX_LEN + SEQ) // ML_CHUNK
    w = ML_HEADS * HEAD_DIM

    def specs(rev):
        blk = functools.partial(_seq_block, blk=ML_CHUNK, rev=rev)
        tok = lambda cb: pl.BlockSpec((ML_CHUNK, w), lambda b, j: (blk(b, j), cb))
        return [tok(COL_MLQ // w), tok(COL_MLK // w), tok(COL_MLV // w),
                pl.BlockSpec((ML_CHUNK, 128), lambda b, j: (blk(b, j), 0))], tok(0)

    in_f, out_f = specs(False)
    in_b, out_b = specs(True)
    out = jax.ShapeDtypeStruct((T_ALL, w), F32)
    return pl.pallas_call(
        _mlstm_kernel,
        out_shape=(out, out),
        grid=(BATCH, n_chunks),
        in_specs=in_f + in_b + [pl.BlockSpec((1, 128), lambda b, j: (0, 0))],
        out_specs=(out_f, out_b),
        scratch_shapes=[pltpu.VMEM((ML_HEADS, HEAD_DIM, HEAD_DIM), F32), pltpu.VMEM((ML_HEADS, 8, HEAD_DIM), F32),
                        pltpu.VMEM((ML_HEADS, 8, HEAD_DIM), F32)] * 2,
        name="mlstm",
    )(proj, proj, proj, mlg, proj, proj, proj, mlg, gate_bias)


def _ml_out_kernel(hf_ref, hb_ref, o_ref, nrm_ref, out_ref):
    for h in range(ML_HEADS):
        cols = slice(h * HEAD_DIM, (h + 1) * HEAD_DIM)
        x = hf_ref[:, cols] + hb_ref[:, cols]
        y = x * lax.rsqrt(jnp.mean(x * x, axis=-1, keepdims=True) + EPS) * nrm_ref[:, cols]
        out_ref[:, cols] = (y * jax.nn.sigmoid(o_ref[:, cols])).astype(out_ref.dtype)


def _ml_out(h_f, h_b, proj, ml_norm):
    tm = TOK_TM
    w = ML_HEADS * HEAD_DIM
    tok = lambda cb: pl.BlockSpec((tm, w), lambda i: (i, cb))
    return pl.pallas_call(
        _ml_out_kernel,
        out_shape=jax.ShapeDtypeStruct((T_ALL, w), BF16),
        grid=(T_ALL // tm,),
        in_specs=[tok(0), tok(0), tok(COL_MLO // w), pl.BlockSpec((1, w), lambda i: (0, 0))],
        out_specs=tok(0),
        name="ml_out",
    )(h_f, h_b, proj, ml_norm.reshape(1, w))


def _merge_kernel(b0_ref, b1_ref, b2_ref, b3_ref, w_ref, g0_ref, g1_ref, g2_ref, g3_ref, z_ref):
    acc = g0_ref[...].astype(F32) * _dot(b0_ref[...], w_ref[0])
    acc += g1_ref[...].astype(F32) * _dot(b1_ref[...], w_ref[1])
    acc += g2_ref[...].astype(F32) * _dot(b2_ref[...], w_ref[2])
    acc += g3_ref[...].astype(F32) * _dot(b3_ref[...], w_ref[3])
    z_ref[...] = acc.astype(z_ref.dtype)


def _merge(branches, w_branch, layer, gates, rows, tm):
    tn = 512
    nj = D_MODEL // tn
    br = pl.BlockSpec((tm, BRANCH_W), lambda i, j: (i, 0))
    gate = lambda g: pl.BlockSpec((tm, tn), lambda i, j: (i, g * nj + j))
    return pl.pallas_call(
        _merge_kernel,
        out_shape=jax.ShapeDtypeStruct((rows, D_MODEL), BF16),
        grid=(rows // tm, nj),
        in_specs=[br, br, br, br, pl.BlockSpec((None, N_BRANCH, BRANCH_W, tn), lambda i, j: (layer, 0, 0, j)),
                  gate(0), gate(1), gate(2), gate(3)],
        out_specs=pl.BlockSpec((tm, tn), lambda i, j: (i, j)),
        compiler_params=_cparams(40),
        name="merge",
    )(*branches, w_branch, gates, gates, gates, gates)


def _mm_norm_res_kernel(*refs, nj, tn, n_x, with_next):
    a_ref, w_ref = refs[:2]
    x_refs = refs[2:2 + n_x]
    g_ref, gt_ref = refs[2 + n_x:4 + n_x]
    rest = refs[4 + n_x:]
    if with_next:
        g2_ref, sh_ref, sc_ref, o_ref, h_ref, y_ref = rest
    else:
        o_ref, y_ref = rest
    j = pl.program_id(1)
    y_ref[j] = _dot(a_ref[...], w_ref[...])

    @pl.when(j == nj - 1)
    def _finish():
        ss = jnp.sum(y_ref[0] * y_ref[0], axis=-1, keepdims=True)
        for jj in range(1, nj):
            ss += jnp.sum(y_ref[jj] * y_ref[jj], axis=-1, keepdims=True)
        rs = lax.rsqrt(ss * (1.0 / D_MODEL) + EPS)

        def emit(x_ref):
            ss2 = jnp.zeros((TOK_TM, 1), F32)
            for jj in range(nj):
                cols = slice(jj * tn, (jj + 1) * tn)
                x_new = x_ref[:, cols] + gt_ref[:, cols] * (y_ref[jj] * rs * g_ref[:, cols])
                o_ref[:, cols] = x_new
                ss2 += jnp.sum(x_new * x_new, axis=-1, keepdims=True)
            if with_next:
                rs2 = lax.rsqrt(ss2 * (1.0 / D_MODEL) + EPS)
                for jj in range(nj):
                    cols = slice(jj * tn, (jj + 1) * tn)
                    hn = o_ref[:, cols] * rs2 * g2_ref[:, cols]
                    h_ref[:, cols] = (hn * (1.0 + sc_ref[:, cols]) + sh_ref[:, cols]).astype(BF16)

        _for_tile_source(x_refs, emit)


def _matmul_norm_residual(a, w, x_pair, g, mod4, layer, gate_chunk, n_tiles, tn, next_norm=None):
    k_dim = a.shape[1]
    nj = D_MODEL // tn
    rows = n_tiles * TOK_TM
    x_specs, x_args = _x_specs(x_pair, n_tiles)
    vec = pl.BlockSpec((1, D_MODEL), lambda i, j: (0, 0))
    tile = pl.BlockSpec((TOK_TM, D_MODEL), lambda i, j: (i, 0))
    in_specs = ([pl.BlockSpec((TOK_TM, k_dim), lambda i, j: (i, 0)),
                 pl.BlockSpec((None, k_dim, tn), lambda i, j: (layer, 0, j))]
                + x_specs + [vec, _mod_spec(layer, gate_chunk)])
    args = [a, w, *x_args, g.reshape(1, D_MODEL), mod4]
    out_shape = jax.ShapeDtypeStruct((rows, D_MODEL), F32)
    out_specs = tile
    if next_norm is not None:
        g2, layer2, shift_chunk, scale_chunk = next_norm
        in_specs += [vec, _mod_spec(layer2, shift_chunk), _mod_spec(layer2, scale_chunk)]
        args += [g2.reshape(1, D_MODEL), mod4, mod4]
        out_shape = (out_shape, jax.ShapeDtypeStruct((rows, D_MODEL), BF16))
        out_specs = (tile, tile)
    return pl.pallas_call(
        functools.partial(_mm_norm_res_kernel, nj=nj, tn=tn, n_x=len(x_args), with_next=next_norm is not None),
        out_shape=out_shape,
        grid=(n_tiles, nj),
        in_specs=in_specs,
        out_specs=out_specs,
        scratch_shapes=[pltpu.VMEM((nj, TOK_TM, tn), F32)],
        compiler_params=_cparams(54),
        name="matmul_norm_res",
    )(*args)


FFN_TF = 512
FFN_HALO = 16
FFN_ROW_BLOCKS = 4
FFN_ALIGN = 16
FFN_LAG = 2 * FFN_ALIGN


def _ffn_up_kernel(hp_ref, hm_ref, hn_ref, wa_ref, wg_ref, cwa_ref, cwg_ref, cba_ref, cbg_ref, o_ref,
                   hext_ref, ua_ref, ug_ref, *, tm):
    i = pl.program_id(0)
    j = pl.program_id(1)
    halo = FFN_HALO
    ext = tm + 2 * halo
    rb = -(-ext // (FFN_ROW_BLOCKS * FFN_ALIGN)) * FFN_ALIGN

    @pl.when(j == 0)
    def _assemble_rows():
        hext_ref[0:halo, :] = hp_ref[...]
        hext_ref[halo:halo + tm, :] = hm_ref[...]
        hext_ref[halo + tm:ext, :] = hn_ref[...]

    wa = wa_ref[...].astype(BF16)
    wg = wg_ref[...].astype(BF16)

    def conv(u_ref, cw_ref, cb_ref, lo, n, has_prev, has_next):
        prev = u_ref[pl.ds(lo + halo - 1, n), :] * has_prev
        mid = u_ref[pl.ds(lo + halo, n), :]
        nxt = u_ref[pl.ds(lo + halo + 1, n), :] * has_next
        return prev * cw_ref[0:1, :] + mid * cw_ref[1:2, :] + nxt * cw_ref[2:3, :] + cb_ref[...]

    lo = 0
    for blk in range(FFN_ROW_BLOCKS):
        rows = slice(blk * rb, min((blk + 1) * rb, ext))
        hx = hext_ref[rows, :]
        ua_ref[rows, :] = _dot(hx, wa)
        ug_ref[rows, :] = _dot(hx, wg)
        hi = tm if blk == FFN_ROW_BLOCKS - 1 else (blk + 1) * rb - FFN_LAG
        n = hi - lo
        row = i * tm + lo + lax.broadcasted_iota(jnp.int32, (n, 1), 0)
        first = (row == 0) | (row == SEQ) | (row == T_LAT) | (row == T_LAT + CTX_LEN)
        final = (row == SEQ - 1) | (row == T_LAT - 1) | (row == T_LAT + CTX_LEN - 1) | (row == T_ALL - 1)
        has_prev = jnp.where(first, 0.0, 1.0)
        has_next = jnp.where(final, 0.0, 1.0)
        a = conv(ua_ref, cwa_ref, cba_ref, lo, n, has_prev, has_next)
        g = conv(ug_ref, cwg_ref, cbg_ref, lo, n, has_prev, has_next)
        o_ref[lo:hi, :] = (a * (g * jax.nn.sigmoid(g))).astype(o_ref.dtype)
        lo = hi


def _ffn_up(h, w_up, layer, conv_w, conv_b, rows, tm):
    tf, halo = FFN_TF, FFN_HALO
    nf = FFN_DIM // tf
    hb = tm // halo
    n_halo_blocks = rows // halo
    ext = tm + 2 * halo
    assert tm % halo == 0 and rows % tm == 0
    conv_b = conv_b.reshape(1, 2 * FFN_DIM)
    return pl.pallas_call(
        functools.partial(_ffn_up_kernel, tm=tm),
        out_shape=jax.ShapeDtypeStruct((rows, FFN_DIM), BF16),
        grid=(rows // tm, nf),
        in_specs=[
            pl.BlockSpec((halo, D_MODEL), lambda i, j: (jnp.maximum(i * hb - 1, 0), 0)),
            pl.BlockSpec((tm, D_MODEL), lambda i, j: (i, 0)),
            pl.BlockSpec((halo, D_MODEL), lambda i, j: (jnp.minimum((i + 1) * hb, n_halo_blocks - 1), 0)),
            pl.BlockSpec((None, D_MODEL, tf), lambda i, j: (layer, 0, j)),
            pl.BlockSpec((None, D_MODEL, tf), lambda i, j: (layer, 0, nf + j)),
            pl.BlockSpec((3, tf), lambda i, j: (0, j)),
            pl.BlockSpec((3, tf), lambda i, j: (0, nf + j)),
            pl.BlockSpec((1, tf), lambda i, j: (0, j)),
            pl.BlockSpec((1, tf), lambda i, j: (0, nf + j)),
        ],
        out_specs=pl.BlockSpec((tm, tf), lambda i, j: (i, j)),
        scratch_shapes=[pltpu.VMEM((ext, D_MODEL), BF16), pltpu.VMEM((ext, tf), F32), pltpu.VMEM((ext, tf), F32)],
        compiler_params=_cparams(52),
        name="ffn_up",
    )(h, h, h, w_up, w_up, conv_w, conv_w, conv_b, conv_b)


def kernel(x, c, ctx, c_ctx, w_ada, b_ada, g_mix_pre, g_mix_post, g_ffn_pre, g_ffn_post, w_in, na_rpb, wg_sink,
           s5_lam_re, s5_lam_im, s5_log_step, s5_b_re, s5_b_im, s5_c_re, s5_c_im, s5_d, s5_w_glu, s5_b_glu,
           ml_gate_bias, ml_norm, w_branch, w_out, w_up, ffn_conv_w, ffn_conv_b, w_down):
    assert x.shape == (BATCH, SEQ, D_MODEL) and ctx.shape == (BATCH, CTX_LEN, D_MODEL)
    xs = (x.reshape(T_LAT, D_MODEL), ctx.reshape(T_CTX, D_MODEL))
    cs = jnp.concatenate([c, c_ctx[None, :], jnp.zeros((8 - BATCH - 1, D_MODEL), F32)], axis=0)
    mod4 = _ada(cs, w_ada, b_ada).reshape(DEPTH, 8, 1, 6 * D_MODEL)
    cos_t, sin_t = _rope_tables()

    na_tabs = _na_bias_table(na_rpb.reshape(DEPTH * NA_HEADS, 2 * NA_WIN_ROWS - 1, 2 * NA_WIN_COLS - 1))
    s5_tabs = _s5_tables(*_s5_discretise(s5_lam_re, s5_lam_im, s5_log_step, s5_b_re, s5_b_im), s5_c_re, s5_c_im)

    w_mlg = jnp.pad(w_in[:, :, COL_MLG:COL_GATE], ((0, 0), (0, 0), (0, 128 - 4 * ML_HEADS))).astype(BF16)
    w_branch_bf = w_branch.astype(BF16)
    w_out_bf = w_out.astype(BF16)
    w_down_bf = w_down.astype(BF16)
    w_glu_bf = s5_w_glu.astype(BF16)

    for l in range(DEPTH):
        ctx_out = l < DEPTH - 1
        n_tiles = N_ALL_TILES if ctx_out else N_LAT_TILES
        rows = n_tiles * TOK_TM
        big_tm = rows // 4

        w_main = w_in[l, :, :PROJ_W].astype(BF16)
        w_gate = w_in[l, :, COL_GATE:].astype(BF16)

        if l == 0:
            h = _norm_mod(xs, g_mix_pre[l], mod4, l, 0, 1, N_ALL_TILES)
        proj = _matmul(h, w_main, PROJ_W, 512, F32, tm=T_ALL // 4, rows=T_ALL)
        mlg = _matmul(h, w_mlg, 128, 128, F32, tm=T_ALL // 4, rows=T_ALL, layer=l)
        gates = _matmul(h, w_gate, GATE_W, 1024, BF16, tm=big_tm, rows=rows, act="sigmoid")

        br_na = _neighbourhood_attention(proj, na_tabs, l)
        br_wg = _windowed_gqa(proj, wg_sink[l], cos_t, sin_t)

        ys = [_s5_scan(proj, *s5_tabs, l, rev=dr == 1) for dr in range(2)]
        br_s5 = _s5_out(proj, ys[0], ys[1], s5_d[l], w_glu_bf, l, s5_b_glu[l])

        gate_bias = jnp.pad(ml_gate_bias[l].reshape(1, 4 * ML_HEADS), ((0, 0), (0, 128 - 4 * ML_HEADS)))
        h_f, h_b = _mlstm(proj, mlg, gate_bias)
        br_ml = _ml_out(h_f, h_b, proj, ml_norm[l])

        z = _merge((br_na, br_s5, br_wg, br_ml), w_branch_bf, l, gates, rows, rows // 8)
        xs, h2 = _matmul_norm_residual(z, w_out_bf, xs, g_mix_post[l], mod4, l, 2, n_tiles, 1024,
                                       next_norm=(g_ffn_pre[l], l, 3, 4))
        act = _ffn_up(h2, w_up, l, ffn_conv_w[l], ffn_conv_b[l], rows, rows // 8)
        if ctx_out:
            xs, h = _matmul_norm_residual(act, w_down_bf, xs, g_ffn_post[l], mod4, l, 5, n_tiles, 512,
                                          next_norm=(g_mix_pre[l + 1], l + 1, 0, 1))
        else:
            xs = _matmul_norm_residual(act, w_down_bf, xs, g_ffn_post[l], mod4, l, 5, n_tiles, 512)

    return xs.reshape(BATCH, SEQ, D_MODEL)
```

```python
import functools

import jax
import jax.numpy as jnp
from jax import lax
from jax.experimental import pallas as pl
from jax.experimental.pallas import tpu as pltpu

F32 = jnp.float32
BF16 = jnp.bfloat16

D_MODEL = 2048
BATCH = 2
SEQ = 4096
DEPTH = 2
GRID_W = 64
CTX_LEN = 256
HEAD_DIM = 128
BRANCH_W = 512
N_BRANCH = 4
NA_HEADS = 4
NA_WIN_ROWS = 8
NA_WIN_COLS = 16
S5_GROUP = 16
S5_GROUPS = BRANCH_W // S5_GROUP
S5_STATE = 64
S5_CH = S5_GROUPS * S5_STATE
WG_Q_HEADS = 4
WG_KV_HEADS = 2
WG_WINDOW = 128
WG_BLOCK = 128
ML_HEADS = 4
ML_CHUNK = 128
FFN_DIM = 5632
ROPE_BASE = 10000.0
EPS = 1e-6
NEG_INF = -1e30

T_LAT = BATCH * SEQ
T_CTX = BATCH * CTX_LEN
T_ALL = T_LAT + T_CTX
GRID_ROWS = SEQ // GRID_W
ATT_SCALE = HEAD_DIM ** -0.5

COL_NAQ, COL_NAK, COL_NAV, COL_S5 = 0, 512, 1024, 1536
COL_WGQ, COL_WGK, COL_WGV = 2048, 2560, 2816
COL_MLQ, COL_MLK, COL_MLV, COL_MLO = 3072, 3584, 4096, 4608
PROJ_W = 5120
COL_MLG = PROJ_W
COL_GATE = PROJ_W + 4 * ML_HEADS
GATE_W = N_BRANCH * D_MODEL

MIB = 1024 * 1024
TOK_TM = 512
N_LAT_TILES = T_LAT // TOK_TM
N_ALL_TILES = T_ALL // TOK_TM
ATT_QBLK = 256
NA_ROWS_PER_STEP = ATT_QBLK // GRID_W
S5_CHUNK = 256
S5_SUB = 8


def _cparams(vmem_mib=None):
    if vmem_mib is None:
        return None
    return pltpu.CompilerParams(vmem_limit_bytes=vmem_mib * MIB)


def _dot(a, b):
    return jnp.dot(a, b, preferred_element_type=F32)


def _dot_nt(a, b):
    return lax.dot_general(a, b, (((1,), (1,)), ((), ())), preferred_element_type=F32)


def _mod_row(i):
    return jnp.where(i >= N_LAT_TILES, BATCH, i // (N_LAT_TILES // BATCH))


def _mod_spec(layer, chunk):
    return pl.BlockSpec((None, None, 1, D_MODEL), lambda i, *_: (layer, _mod_row(i), 0, chunk))


def _seq_block(b, j, blk, rev):
    n_ctx = CTX_LEN // blk
    n_lat = SEQ // blk
    if rev:
        ctx = T_LAT // blk + b * n_ctx + (n_ctx - 1 - j)
        lat = b * n_lat + (n_lat - 1 - (j - n_ctx))
    else:
        ctx = T_LAT // blk + b * n_ctx + j
        lat = b * n_lat + (j - n_ctx)
    return jnp.where(j < n_ctx, ctx, lat)


def _ada_kernel(c_ref, w_ref, b_ref, o_ref):
    c = c_ref[...]
    s = (c * jax.nn.sigmoid(c)).astype(BF16)
    o_ref[...] = _dot(s, w_ref[...].astype(BF16)) + b_ref[...]


def _ada(cs, w_ada, b_ada):
    tn = 1024
    n_out = 6 * D_MODEL
    return pl.pallas_call(
        _ada_kernel,
        out_shape=jax.ShapeDtypeStruct((DEPTH, 8, n_out), F32),
        grid=(DEPTH, n_out // tn),
        in_specs=[
            pl.BlockSpec((8, D_MODEL), lambda l, j: (0, 0)),
            pl.BlockSpec((None, D_MODEL, tn), lambda l, j: (l, 0, j)),
            pl.BlockSpec((None, 1, tn), lambda l, j: (l, 0, j)),
        ],
        out_specs=pl.BlockSpec((None, 8, tn), lambda l, j: (l, 0, j)),
        compiler_params=_cparams(40),
        name="ada",
    )(cs, w_ada, b_ada.reshape(DEPTH, 1, n_out))


def _x_specs(x_pair, n_tiles):
    if isinstance(x_pair, tuple):
        lat, ctx = x_pair
        last = N_LAT_TILES - 1
        specs = [pl.BlockSpec((TOK_TM, D_MODEL), lambda i, *_: (jnp.minimum(i, last), 0)),
                 pl.BlockSpec((TOK_TM, D_MODEL), lambda i, *_: (0, 0))]
        return specs, [lat, ctx]
    return [pl.BlockSpec((TOK_TM, D_MODEL), lambda i, *_: (i, 0))], [x_pair]


def _for_tile_source(x_refs, fn):
    if len(x_refs) == 1:
        fn(x_refs[0])
        return
    i = pl.program_id(0)
    pl.when(i < N_LAT_TILES)(lambda: fn(x_refs[0]))
    pl.when(i >= N_LAT_TILES)(lambda: fn(x_refs[1]))


def _norm_mod_kernel(*refs):
    *x_refs, g_ref, sh_ref, sc_ref, o_ref = refs

    def body(x_ref):
        x = x_ref[...]
        y = x * lax.rsqrt(jnp.mean(x * x, axis=-1, keepdims=True) + EPS) * g_ref[...]
        o_ref[...] = (y * (1.0 + sc_ref[...]) + sh_ref[...]).astype(BF16)

    _for_tile_source(x_refs, body)


def _norm_mod(x_pair, g, mod4, layer, shift_chunk, scale_chunk, n_tiles):
    x_specs, x_args = _x_specs(x_pair, n_tiles)
    return pl.pallas_call(
        _norm_mod_kernel,
        out_shape=jax.ShapeDtypeStruct((n_tiles * TOK_TM, D_MODEL), BF16),
        grid=(n_tiles,),
        in_specs=x_specs + [pl.BlockSpec((1, D_MODEL), lambda i: (0, 0)),
                            _mod_spec(layer, shift_chunk), _mod_spec(layer, scale_chunk)],
        out_specs=pl.BlockSpec((TOK_TM, D_MODEL), lambda i: (i, 0)),
        name="norm_mod",
    )(*x_args, g.reshape(1, D_MODEL), mod4, mod4)


def _mm_kernel(a_ref, w_ref, o_ref, *, act):
    r = _dot(a_ref[...], w_ref[...].astype(BF16))
    if act == "sigmoid":
        r = jax.nn.sigmoid(r)
    o_ref[...] = r.astype(o_ref.dtype)


def _matmul(a, w, n_cols, tn, out_dtype, *, tm, rows, layer=None, act=None):
    k = a.shape[1]
    if layer is None:
        w_spec = pl.BlockSpec((k, tn), lambda i, j: (0, j))
    else:
        w_spec = pl.BlockSpec((None, k, tn), lambda i, j: (layer, 0, j))
    return pl.pallas_call(
        functools.partial(_mm_kernel, act=act),
        out_shape=jax.ShapeDtypeStruct((rows, n_cols), out_dtype),
        grid=(rows // tm, n_cols // tn),
        in_specs=[pl.BlockSpec((tm, k), lambda i, j: (i, 0)), w_spec],
        out_specs=pl.BlockSpec((tm, tn), lambda i, j: (i, j)),
        compiler_params=_cparams(52),
        name="matmul",
    )(a, w)


def _att_row_block(b, s):
    n_lat = SEQ // ATT_QBLK
    return jnp.where(s == 0, T_LAT // ATT_QBLK + b, b * n_lat + s - 1)


NA_UNION_ROWS = NA_ROWS_PER_STEP + NA_WIN_ROWS - 1


NA_HEADS_PER_STEP = 2


def _na_kernel(q_ref, kl_ref, vl_ref, kc_ref, vc_ref, tb_ref, o_ref):
    step = pl.program_id(2)
    heads = range(NA_HEADS_PER_STEP)
    cols = [slice(h * HEAD_DIM, (h + 1) * HEAD_DIM) for h in heads]
    kc = [kc_ref[:, cols[h]].astype(BF16) for h in heads]
    vc = [vc_ref[:, cols[h]].astype(BF16) for h in heads]
    q = [q_ref[:, cols[h]].astype(BF16) for h in heads]
    s_ctx = [_dot_nt(q[h], kc[h]) * ATT_SCALE for h in heads]

    @pl.when(step == 0)
    def _context_queries():
        p = [jnp.exp(s_ctx[h] - jnp.max(s_ctx[h], axis=-1, keepdims=True)) for h in heads]
        o = [_dot(p[h].astype(BF16), vc[h]) / jnp.sum(p[h], axis=-1, keepdims=True) for h in heads]
        for h in heads:
            o_ref[:, cols[h]] = o[h].astype(o_ref.dtype)

    @pl.when(step > 0)
    def _latent_queries():
        rq0 = (step - 1) * NA_ROWS_PER_STEP
        u0 = jnp.clip(rq0 - NA_WIN_ROWS // 2, 0, GRID_ROWS - NA_UNION_ROWS)
        pattern = jnp.where(rq0 == 0, 0, jnp.where(rq0 == GRID_ROWS - NA_ROWS_PER_STEP, 2, 1))
        start = pl.multiple_of(u0 * GRID_W, GRID_W)
        n_win = NA_UNION_ROWS * GRID_W
        kw = [kl_ref[pl.ds(start, n_win), cols[h]].astype(BF16) for h in heads]
        vw = [vl_ref[pl.ds(start, n_win), cols[h]].astype(BF16) for h in heads]
        bias = [tb_ref[h, pattern] for h in heads]
        s_loc = [jnp.where(bias[h] > 0.5 * NEG_INF, _dot_nt(q[h], kw[h]) * ATT_SCALE + bias[h], NEG_INF)
                 for h in heads]
        m = [jnp.maximum(jnp.max(s_loc[h], axis=-1, keepdims=True), jnp.max(s_ctx[h], axis=-1, keepdims=True))
             for h in heads]
        p_loc = [jnp.exp(s_loc[h] - m[h]) for h in heads]
        p_ctx = [jnp.exp(s_ctx[h] - m[h]) for h in heads]
        den = [jnp.sum(p_loc[h], axis=-1, keepdims=True) + jnp.sum(p_ctx[h], axis=-1, keepdims=True) for h in heads]
        o = [(_dot(p_loc[h].astype(BF16), vw[h]) + _dot(p_ctx[h].astype(BF16), vc[h])) / den[h] for h in heads]
        for h in heads:
            o_ref[:, cols[h]] = o[h].astype(o_ref.dtype)


def _na_bias_table(rpb):
    assert NA_ROWS_PER_STEP == NA_WIN_ROWS // 2 and GRID_ROWS % NA_ROWS_PER_STEP == 0
    n_r, n_u = NA_ROWS_PER_STEP, NA_UNION_ROWS
    col = jnp.arange(GRID_W)
    c0 = jnp.clip(col - NA_WIN_COLS // 2, 0, GRID_W - NA_WIN_COLS)
    col_ok = (col[None, :] >= c0[:, None]) & (col[None, :] < c0[:, None] + NA_WIN_COLS)
    dc = jnp.clip(col[None, :] - col[:, None] + NA_WIN_COLS - 1, 0, 2 * NA_WIN_COLS - 2)
    per_dr = jnp.where(col_ok[None, None], rpb[:, :, dc].astype(F32), NEG_INF)
    d = jnp.arange(n_r)[:, None]
    i = jnp.arange(n_u)[None, :]
    zero = jnp.zeros_like(d)
    tabs = []
    for u_off, i0 in ((0, zero), (-(NA_WIN_ROWS // 2), d), (n_r - n_u, zero + (n_u - NA_WIN_ROWS))):
        dr = u_off + i - d + NA_WIN_ROWS - 1
        visible = (i >= i0) & (i < i0 + NA_WIN_ROWS)
        t = per_dr[:, jnp.clip(dr, 0, 2 * NA_WIN_ROWS - 2)]
        t = jnp.where(visible[None, :, :, None, None], t, NEG_INF)
        tabs.append(jnp.transpose(t, (0, 1, 3, 2, 4)).reshape(rpb.shape[0], n_r * GRID_W, n_u * GRID_W))
    return jnp.stack(tabs, axis=1)


def _neighbourhood_attention(proj, tb, layer):
    n_steps = 1 + SEQ // ATT_QBLK
    hps = NA_HEADS_PER_STEP
    hb = hps * HEAD_DIM
    n_groups = NA_HEADS // hps
    ctx_blk = T_LAT // CTX_LEN
    return pl.pallas_call(
        _na_kernel,
        out_shape=jax.ShapeDtypeStruct((T_ALL, BRANCH_W), BF16),
        grid=(BATCH, n_groups, n_steps),
        in_specs=[
            pl.BlockSpec((ATT_QBLK, hb), lambda b, h, s: (_att_row_block(b, s), COL_NAQ // hb + h)),
            pl.BlockSpec((SEQ, hb), lambda b, h, s: (b, COL_NAK // hb + h)),
            pl.BlockSpec((SEQ, hb), lambda b, h, s: (b, COL_NAV // hb + h)),
            pl.BlockSpec((CTX_LEN, hb), lambda b, h, s: (ctx_blk + b, COL_NAK // hb + h)),
            pl.BlockSpec((CTX_LEN, hb), lambda b, h, s: (ctx_blk + b, COL_NAV // hb + h)),
            pl.BlockSpec((hps, 3, ATT_QBLK, NA_UNION_ROWS * GRID_W), lambda b, h, s: (layer * n_groups + h, 0, 0, 0)),
        ],
        out_specs=pl.BlockSpec((ATT_QBLK, hb), lambda b, h, s: (_att_row_block(b, s), h)),
        compiler_params=_cparams(48),
        name="na_attn",
    )(proj, proj, proj, proj, proj, tb)


def _rope(x, cos, sin_signed):
    lane = lax.broadcasted_iota(jnp.int32, x.shape, 1)
    partner = jnp.where(lane % 64 < 32, pltpu.roll(x, 96, 1), pltpu.roll(x, 32, 1))
    return x * cos + partner * sin_signed


def _wg_kernel(sink_ref, q_ref, kl_ref, vl_ref, kc_ref, vc_ref, cos_ref, sin_ref, o_ref, kr_ref):
    hk = pl.program_id(1)
    step = pl.program_id(2)
    kc = kc_ref[...].astype(BF16)
    vc = vc_ref[...].astype(BF16)
    group = WG_Q_HEADS // WG_KV_HEADS

    @pl.when(step == 0)
    def _context_queries():
        for g in range(group):
            sink = sink_ref[hk * group + g]
            q = q_ref[:, g * HEAD_DIM:(g + 1) * HEAD_DIM].astype(BF16)
            s = _dot_nt(q, kc) * ATT_SCALE
            m = jnp.maximum(jnp.max(s, axis=-1, keepdims=True), sink)
            p = jnp.exp(s - m)
            den = jnp.sum(p, axis=-1, keepdims=True) + jnp.exp(sink - m)
            o_ref[:, g * HEAD_DIM:(g + 1) * HEAD_DIM] = (_dot(p.astype(BF16), vc) / den).astype(o_ref.dtype)

    @pl.when(step == 1)
    def _rope_keys():
        kr_ref[...] = _rope(kl_ref[...], cos_ref[...], sin_ref[...]).astype(BF16)

    @pl.when(step > 0)
    def _latent_queries():
        n_win = ATT_QBLK + 2 * WG_WINDOW
        base = pl.multiple_of((step - 1) * ATT_QBLK, ATT_QBLK)
        start = pl.multiple_of(jnp.clip(base - WG_WINDOW, 0, SEQ - n_win), WG_WINDOW)
        cos_q = cos_ref[pl.ds(base, ATT_QBLK), :]
        sin_q = sin_ref[pl.ds(base, ATT_QBLK), :]
        k_win = kr_ref[pl.ds(start, n_win), :]
        v_win = vl_ref[pl.ds(start, n_win), :].astype(BF16)
        qi = lax.broadcasted_iota(jnp.int32, (ATT_QBLK, n_win), 0)
        kj = lax.broadcasted_iota(jnp.int32, (ATT_QBLK, n_win), 1)
        in_window = jnp.abs(kj - qi + (start - base)) <= WG_WINDOW
        gs = range(group)
        sink = [sink_ref[hk * group + g] for g in gs]
        cols = [slice(g * HEAD_DIM, (g + 1) * HEAD_DIM) for g in gs]
        q = [_rope(q_ref[:, cols[g]], cos_q, sin_q).astype(BF16) for g in gs]
        s_loc = [jnp.where(in_window, _dot_nt(q[g], k_win) * ATT_SCALE, NEG_INF) for g in gs]
        s_ctx = [_dot_nt(q[g], kc) * ATT_SCALE for g in gs]
        m = [jnp.maximum(jnp.maximum(jnp.max(s_loc[g], axis=-1, keepdims=True),
                                     jnp.max(s_ctx[g], axis=-1, keepdims=True)), sink[g]) for g in gs]
        p_loc = [jnp.exp(s_loc[g] - m[g]) for g in gs]
        p_ctx = [jnp.exp(s_ctx[g] - m[g]) for g in gs]
        den = [jnp.sum(p_loc[g], axis=-1, keepdims=True) + jnp.sum(p_ctx[g], axis=-1, keepdims=True)
               + jnp.exp(sink[g] - m[g]) for g in gs]
        o = [(_dot(p_loc[g].astype(BF16), v_win) + _dot(p_ctx[g].astype(BF16), vc)) / den[g] for g in gs]
        for g in gs:
            o_ref[:, cols[g]] = o[g].astype(o_ref.dtype)


def _rope_tables():
    t = jnp.arange(SEQ)
    pos = jnp.stack([t // GRID_W, t % GRID_W], axis=-1).astype(F32)
    n_freq = HEAD_DIM // 4
    inv_freq = ROPE_BASE ** (-jnp.arange(n_freq, dtype=F32) / n_freq)
    ang = pos[:, :, None] * inv_freq
    cos, sin = jnp.cos(ang), jnp.sin(ang)
    cos_t = jnp.concatenate([cos[:, 0], cos[:, 0], cos[:, 1], cos[:, 1]], axis=-1)
    sin_t = jnp.concatenate([-sin[:, 0], sin[:, 0], -sin[:, 1], sin[:, 1]], axis=-1)
    return cos_t, sin_t


def _windowed_gqa(proj, sink, cos_t, sin_t):
    n_steps = 1 + SEQ // ATT_QBLK
    qw = (WG_Q_HEADS // WG_KV_HEADS) * HEAD_DIM
    hb = HEAD_DIM
    ctx_blk = T_LAT // CTX_LEN
    return pl.pallas_call(
        _wg_kernel,
        out_shape=jax.ShapeDtypeStruct((T_ALL, BRANCH_W), BF16),
        grid=(BATCH, WG_KV_HEADS, n_steps),
        in_specs=[
            pl.BlockSpec(memory_space=pltpu.SMEM),
            pl.BlockSpec((ATT_QBLK, qw), lambda b, h, s: (_att_row_block(b, s), COL_WGQ // qw + h)),
            pl.BlockSpec((SEQ, hb), lambda b, h, s: (b, COL_WGK // hb + h)),
            pl.BlockSpec((SEQ, hb), lambda b, h, s: (b, COL_WGV // hb + h)),
            pl.BlockSpec((CTX_LEN, hb), lambda b, h, s: (ctx_blk + b, COL_WGK // hb + h)),
            pl.BlockSpec((CTX_LEN, hb), lambda b, h, s: (ctx_blk + b, COL_WGV // hb + h)),
            pl.BlockSpec((SEQ, hb), lambda b, h, s: (0, 0)),
            pl.BlockSpec((SEQ, hb), lambda b, h, s: (0, 0)),
        ],
        out_specs=pl.BlockSpec((ATT_QBLK, qw), lambda b, h, s: (_att_row_block(b, s), h)),
        scratch_shapes=[pltpu.VMEM((SEQ, hb), BF16)],
        name="wg_attn",
    )(sink, proj, proj, proj, proj, proj, cos_t, sin_t)


def _s5_disc_kernel(lre_ref, lim_ref, lstep_ref, bre_ref, bim_ref, pre_ref, pim_ref, bbre_ref, bbim_ref):
    lre = jnp.minimum(lre_ref[...], -1e-4)
    lim = lim_ref[...]
    step = jnp.exp(lstep_ref[...])
    kk = (lax.broadcasted_iota(jnp.int32, (S5_SUB, 1), 0) + 1).astype(F32)
    mag = jnp.exp(kk * (lre * step))
    ang = kk * (lim * step)
    p_re = mag * jnp.cos(ang)
    p_im = mag * jnp.sin(ang)
    pre_ref[...] = p_re
    pim_ref[...] = p_im
    a_re = p_re[0:1, :]
    a_im = p_im[0:1, :]
    den = lre * lre + lim * lim
    f_re = ((a_re - 1.0) * lre + a_im * lim) / den
    f_im = (a_im * lre - (a_re - 1.0) * lim) / den
    br = bre_ref[...]
    bi = bim_ref[...]
    bbre_ref[...] = f_re * br - f_im * bi
    bbim_ref[...] = f_re * bi + f_im * br


def _s5_discretise(lam_re, lam_im, log_step, b_re, b_im):
    n = DEPTH * 2
    lre = lam_re.reshape(n, 1, S5_CH)
    lim = lam_im.reshape(n, 1, S5_CH)
    lstep = jnp.repeat(log_step.reshape(n, S5_GROUPS), S5_STATE, axis=-1).reshape(n, 1, S5_CH)
    br = jnp.transpose(b_re, (0, 1, 4, 2, 3)).reshape(n, S5_GROUP, S5_CH)
    bi = jnp.transpose(b_im, (0, 1, 4, 2, 3)).reshape(n, S5_GROUP, S5_CH)
    row = lambda r: pl.BlockSpec((None, r, S5_CH), lambda d: (d, 0, 0))
    outs = pl.pallas_call(
        _s5_disc_kernel,
        out_shape=(jax.ShapeDtypeStruct((n, S5_SUB, S5_CH), F32), jax.ShapeDtypeStruct((n, S5_SUB, S5_CH), F32),
                   jax.ShapeDtypeStruct((n, S5_GROUP, S5_CH), F32), jax.ShapeDtypeStruct((n, S5_GROUP, S5_CH), F32)),
        grid=(n,),
        in_specs=[row(1), row(1), row(1), row(S5_GROUP), row(S5_GROUP)],
        out_specs=(row(S5_SUB), row(S5_SUB), row(S5_GROUP), row(S5_GROUP)),
        name="s5_disc",
    )(lre, lim, lstep, br, bi)
    return [o.reshape(DEPTH, 2, *o.shape[1:]) for o in outs]


def _s5_scan_kernel(u_ref, bbre_ref, bbim_ref, cre_ref, cim_ref, tab_ref, y_ref, sre_ref, sim_ref, car_ref, *, rev):
    j = pl.program_id(1)

    @pl.when(j == 0)
    def _reset():
        car_ref[...] = jnp.zeros_like(car_ref)

    u = u_ref[...].astype(BF16)
    n_col_tiles = S5_CH // 256
    for c in range(n_col_tiles):
        ub = u[:, 128 * (c // 2):128 * (c // 2) + 128]
        sre_ref[:, 256 * c:256 * (c + 1)] = _dot(ub, bbre_ref[c])
        sim_ref[:, 256 * c:256 * (c + 1)] = _dot(ub, bbim_ref[c])

    n_groups = S5_CHUNK // S5_SUB
    last = 0 if rev else S5_SUB - 1

    def group(gi, carry):
        cr, ci = carry
        g = (n_groups - 1 - gi) if rev else gi
        r0 = pl.multiple_of(g * S5_SUB, S5_SUB)
        xr = sre_ref[pl.ds(r0, S5_SUB), :]
        xi = sim_ref[pl.ds(r0, S5_SUB), :]
        for t, k in enumerate((1, 2, 4)):
            shift = (S5_SUB - k) if rev else k
            ar = tab_ref[2 * t]
            ai = tab_ref[2 * t + 1]
            rr = pltpu.roll(xr, shift, 0)
            ri = pltpu.roll(xi, shift, 0)
            xr, xi = xr + ar * rr - ai * ri, xi + ar * ri + ai * rr
        apr = tab_ref[6]
        api = tab_ref[7]
        xr, xi = xr + apr * cr - api * ci, xi + apr * ci + api * cr
        sre_ref[pl.ds(r0, S5_SUB), :] = xr
        sim_ref[pl.ds(r0, S5_SUB), :] = xi
        return xr[last:last + 1, :], xi[last:last + 1, :]

    cr, ci = lax.fori_loop(0, n_groups, group, (car_ref[0:1, :], car_ref[1:2, :]))
    car_ref[0:1, :] = cr
    car_ref[1:2, :] = ci

    n_out_tiles = BRANCH_W // 128
    kw = S5_CH // n_out_tiles
    for oc in range(n_out_tiles):
        sr = sre_ref[:, kw * oc:kw * (oc + 1)].astype(BF16)
        si = sim_ref[:, kw * oc:kw * (oc + 1)].astype(BF16)
        y_ref[:, 128 * oc:128 * (oc + 1)] = _dot(sr, cre_ref[oc]) + _dot(si, cim_ref[oc])


def _s5_tables(p_re, p_im, bb_re, bb_im, c_re, c_im):
    row = jnp.arange(S5_SUB)
    ks = jnp.array([1, 2, 4])
    keep = jnp.stack([row[None, :] >= ks[:, None], row[None, :] <= S5_SUB - 1 - ks[:, None]]).astype(F32)

    def shift_tabs(p):
        return keep[None, :, :, :, None] * p[:, :, ks - 1][:, :, :, None, :]

    def carry_tabs(p):
        return jnp.stack([p[:, 0], p[:, 1, ::-1]], axis=1)

    s_re, s_im = shift_tabs(p_re), shift_tabs(p_im)
    tab = jnp.stack([s_re[:, :, 0], s_im[:, :, 0], s_re[:, :, 1], s_im[:, :, 1], s_re[:, :, 2], s_im[:, :, 2],
                     carry_tabs(p_re), carry_tabs(p_im)], axis=2)

    n_in = S5_CH // 256
    c_idx, gl, gj = jnp.arange(n_in)[:, None, None], jnp.arange(8)[None, :, None], jnp.arange(4)[None, None, :]
    in_mask = (gl == 4 * (c_idx % 2) + gj).astype(F32)

    def in_tiles(bb):
        x = bb.reshape(DEPTH, 2, S5_GROUP, n_in, 4, S5_STATE)
        x = jnp.transpose(x, (0, 1, 3, 2, 4, 5))
        t = in_mask[None, None, :, :, None, :, None] * x[:, :, :, None]
        return t.reshape(DEPTH, 2, n_in, 128, 256).astype(BF16)

    n_out = BRANCH_W // 128
    eye8 = jnp.eye(8, dtype=F32)

    def out_tiles(cc):
        x = cc.astype(F32).reshape(DEPTH, 2, n_out, 8, S5_GROUP, S5_STATE)
        x = jnp.transpose(x, (0, 1, 2, 5, 3, 4))
        t = eye8[None, None, None, :, None, :, None] * x[:, :, :, None]
        return t.reshape(DEPTH, 2, n_out, S5_CH // n_out, 128).astype(BF16)

    return tab, in_tiles(bb_re), in_tiles(bb_im), out_tiles(c_re), out_tiles(-c_im)


def _s5_scan(proj, tab, bbre_t, bbim_t, cre_t, cim_t, layer, rev):
    n_chunks = (CTX_LEN + SEQ) // S5_CHUNK
    blk = functools.partial(_seq_block, blk=S5_CHUNK, rev=rev)
    d = 1 if rev else 0

    def full(arr):
        shape = arr.shape[2:]
        return pl.BlockSpec((None, None) + shape, lambda b, j: (layer, d) + (0,) * len(shape))

    return pl.pallas_call(
        functools.partial(_s5_scan_kernel, rev=rev),
        out_shape=jax.ShapeDtypeStruct((T_ALL, BRANCH_W), F32),
        grid=(BATCH, n_chunks),
        in_specs=[
            pl.BlockSpec((S5_CHUNK, BRANCH_W), lambda b, j: (blk(b, j), COL_S5 // BRANCH_W)),
            full(bbre_t), full(bbim_t), full(cre_t), full(cim_t), full(tab),
        ],
        out_specs=pl.BlockSpec((S5_CHUNK, BRANCH_W), lambda b, j: (blk(b, j), 0)),
        scratch_shapes=[pltpu.VMEM((S5_CHUNK, S5_CH), F32), pltpu.VMEM((S5_CHUNK, S5_CH), F32),
                        pltpu.VMEM((8, S5_CH), F32)],
        name="s5_scan_bwd" if rev else "s5_scan_fwd",
    )(proj, bbre_t, bbim_t, cre_t, cim_t, tab)


def _s5_out_kernel(u_ref, yf_ref, yb_ref, d_ref, w_ref, b_ref, o_ref):
    y = d_ref[...] * u_ref[...] + yf_ref[...] + yb_ref[...]
    y = jax.nn.gelu(y)
    z = _dot(y.astype(BF16), w_ref[...]) + b_ref[...]
    o_ref[...] = (y * jax.nn.sigmoid(z)).astype(o_ref.dtype)


def _s5_out(proj, y_f, y_b, d_skip, w_glu, layer, b_glu):
    tm = TOK_TM
    tok = lambda cb: pl.BlockSpec((tm, BRANCH_W), lambda i: (i, cb))
    vec = pl.BlockSpec((1, BRANCH_W), lambda i: (0, 0))
    return pl.pallas_call(
        _s5_out_kernel,
        out_shape=jax.ShapeDtypeStruct((T_ALL, BRANCH_W), BF16),
        grid=(T_ALL // tm,),
        in_specs=[tok(COL_S5 // BRANCH_W), tok(0), tok(0), vec,
                  pl.BlockSpec((None, BRANCH_W, BRANCH_W), lambda i: (layer, 0, 0)), vec],
        out_specs=tok(0),
        name="s5_out",
    )(proj, y_f, y_b, d_skip.reshape(1, BRANCH_W), w_glu, b_glu.reshape(1, BRANCH_W))


def _mlstm_kernel(qf_ref, kf_ref, vf_ref, gf_ref, qb_ref, kb_ref, vb_ref, gb_ref, bias_ref, hf_ref, hb_ref,
                  cf_ref, nf_ref, mf_ref, cb_ref, nb_ref, mb_ref):
    j = pl.program_id(1)

    @pl.when(j == 0)
    def _reset():
        for ref in (cf_ref, nf_ref, mf_ref, cb_ref, nb_ref, mb_ref):
            ref[...] = jnp.zeros_like(ref)

    L = ML_CHUNK
    q_refs, k_refs, v_refs, g_refs = (qf_ref, qb_ref), (kf_ref, kb_ref), (vf_ref, vb_ref), (gf_ref, gb_ref)
    h_refs, c_refs, n_refs, m_refs = (hf_ref, hb_ref), (cf_ref, cb_ref), (nf_ref, nb_ref), (mf_ref, mb_ref)
    dirs = range(2)
    chains = [(d, h) for d in dirs for h in range(ML_HEADS)]
    c_prev = [c_refs[d][h] for d, h in chains]
    n_prev = [n_refs[d][h][0:1, :] for d, h in chains]
    m_prev = [m_refs[d][h][0:1, 0:1] for d, h in chains]

    t_idx = lax.broadcasted_iota(jnp.int32, (L, L), 0)
    s_idx = lax.broadcasted_iota(jnp.int32, (L, L), 1)
    tri_dir = [s_idx <= t_idx, s_idx >= t_idx]
    tri_bf = [jnp.where(t, 1.0, 0.0).astype(BF16) for t in tri_dir]
    ones_bf = jnp.ones((L, HEAD_DIM), BF16)

    gates = [g_refs[d][...] + bias_ref[...] for d in dirs]
    log_f = [jax.nn.log_sigmoid(gates[d]) for d in dirs]
    hi = [log_f[d].astype(BF16) for d in dirs]
    r1 = [log_f[d] - hi[d].astype(F32) for d in dirs]
    mid = [r1[d].astype(BF16) for d in dirs]
    lo = [(r1[d] - mid[d].astype(F32)).astype(BF16) for d in dirs]
    bcum = [_dot(tri_bf[d], hi[d]) + _dot(tri_bf[d], mid[d]) + _dot(tri_bf[d], lo[d]) for d in dirs]
    gates_t = [gates[d].T for d in dirs]
    bcum_t = [bcum[d].T for d in dirs]
    last = [L - 1, 0]

    n_ch = range(len(chains))
    ii = [(2 * d) * ML_HEADS + h for d, h in chains]
    fi = [(2 * d + 1) * ML_HEADS + h for d, h in chains]
    cols = [slice(h * HEAD_DIM, (h + 1) * HEAD_DIM) for d, h in chains]
    tri = [tri_dir[d] for d, h in chains]
    li_r = [gates_t[d][ii[c]:ii[c] + 1, :] for c, (d, h) in enumerate(chains)]
    bc_c = [bcum[d][:, fi[c]:fi[c] + 1] for c, (d, h) in enumerate(chains)]
    bc_r = [bcum_t[d][fi[c]:fi[c] + 1, :] for c, (d, h) in enumerate(chains)]
    b_last = [bcum_t[d][fi[c]:fi[c] + 1, last[d]:last[d] + 1] for c, (d, h) in enumerate(chains)]
    k = [k_refs[d][:, cols[c]] * ATT_SCALE for c, (d, h) in enumerate(chains)]
    v = [v_refs[d][:, cols[c]] for c, (d, h) in enumerate(chains)]
    qb = [q_refs[d][:, cols[c]].astype(BF16) for c, (d, h) in enumerate(chains)]
    kb = [k[c].astype(BF16) for c in n_ch]
    vb = [v[c].astype(BF16) for c in n_ch]

    qk = [_dot_nt(qb[c], kb[c]) for c in n_ch]
    qc = [_dot_nt(qb[c], c_prev[c].astype(BF16)) for c in n_ch]
    qn = [_dot_nt(qb[c], jnp.broadcast_to(n_prev[c], (L, HEAD_DIM)).astype(BF16)) for c in n_ch]
    v_t = [v[c].T for c in n_ch]

    log_end = [b_last[c] - bc_r[c] + li_r[c] for c in n_ch]
    m_new = [jnp.maximum(b_last[c] + m_prev[c], jnp.max(log_end[c], axis=-1, keepdims=True)) for c in n_ch]
    w_end = [jnp.exp(log_end[c] - m_new[c]) for c in n_ch]
    decay = [jnp.exp(b_last[c] + m_prev[c] - m_new[c]) for c in n_ch]

    bc_full = [jnp.broadcast_to(bc_c[c], (L, L)) for c in n_ch]
    log_w = [jnp.where(tri[c], bc_full[c] - bc_r[c] + li_r[c], NEG_INF) for c in n_ch]
    log_inter = [bc_full[c] + m_prev[c] for c in n_ch]
    m_t = [jnp.maximum(log_inter[c], jnp.broadcast_to(jnp.max(log_w[c], axis=-1, keepdims=True), (L, L)))
           for c in n_ch]
    w = [jnp.exp(log_w[c] - m_t[c]) for c in n_ch]
    inter = [jnp.exp(log_inter[c] - m_t[c]) for c in n_ch]
    sb = [(qk[c] * w[c]).astype(BF16) for c in n_ch]
    num = [inter[c] * qc[c] + _dot(sb[c], vb[c]) for c in n_ch]
    den = [inter[c] * qn[c] + _dot(sb[c], ones_bf) for c in n_ch]
    for c, (d, h) in enumerate(chains):
        h_refs[d][:, cols[c]] = num[c] / jnp.maximum(jnp.abs(den[c]), jnp.exp(-m_t[c]))

    vw_t = [(v_t[c] * w_end[c]).astype(BF16) for c in n_ch]
    c_new = [decay[c] * c_prev[c] + _dot(vw_t[c], kb[c]) for c in n_ch]
    n_new = [decay[c] * n_prev[c] + _dot(jnp.broadcast_to(w_end[c], (8, L)).astype(BF16), kb[c])[0:1, :]
             for c in n_ch]
    for c, (d, h) in enumerate(chains):
        c_refs[d][h] = c_new[c]
        n_refs[d][h] = jnp.broadcast_to(n_new[c], (8, HEAD_DIM))
        m_refs[d][h] = jnp.broadcast_to(m_new[c], (8, HEAD_DIM))


def _mlstm(proj, mlg, gate_bias):
    n_chunks = (CTX_LEN + SEQ) // ML_CHUNK
    w = ML_HEADS * HEAD_DIM

    def specs(rev):
        blk = functools.partial(_seq_block, blk=ML_CHUNK, rev=rev)
        tok = lambda cb: pl.BlockSpec((ML_CHUNK, w), lambda b, j: (blk(b, j), cb))
        return [tok(COL_MLQ // w), tok(COL_MLK // w), tok(COL_MLV // w),
                pl.BlockSpec((ML_CHUNK, 128), lambda b, j: (blk(b, j), 0))], tok(0)

    in_f, out_f = specs(False)
    in_b, out_b = specs(True)
    out = jax.ShapeDtypeStruct((T_ALL, w), F32)
    return pl.pallas_call(
        _mlstm_kernel,
        out_shape=(out, out),
        grid=(BATCH, n_chunks),
        in_specs=in_f + in_b + [pl.BlockSpec((1, 128), lambda b, j: (0, 0))],
        out_specs=(out_f, out_b),
        scratch_shapes=[pltpu.VMEM((ML_HEADS, HEAD_DIM, HEAD_DIM), F32), pltpu.VMEM((ML_HEADS, 8, HEAD_DIM), F32),
                        pltpu.VMEM((ML_HEADS, 8, HEAD_DIM), F32)] * 2,
        name="mlstm",
    )(proj, proj, proj, mlg, proj, proj, proj, mlg, gate_bias)


def _ml_out_kernel(hf_ref, hb_ref, o_ref, nrm_ref, out_ref):
    for h in range(ML_HEADS):
        cols = slice(h * HEAD_DIM, (h + 1) * HEAD_DIM)
        x = hf_ref[:, cols] + hb_ref[:, cols]
        y = x * lax.rsqrt(jnp.mean(x * x, axis=-1, keepdims=True) + EPS) * nrm_ref[:, cols]
        out_ref[:, cols] = (y * jax.nn.sigmoid(o_ref[:, cols])).astype(out_ref.dtype)


def _ml_out(h_f, h_b, proj, ml_norm):
    tm = TOK_TM
    w = ML_HEADS * HEAD_DIM
    tok = lambda cb: pl.BlockSpec((tm, w), lambda i: (i, cb))
    return pl.pallas_call(
        _ml_out_kernel,
        out_shape=jax.ShapeDtypeStruct((T_ALL, w), BF16),
        grid=(T_ALL // tm,),
        in_specs=[tok(0), tok(0), tok(COL_MLO // w), pl.BlockSpec((1, w), lambda i: (0, 0))],
        out_specs=tok(0),
        name="ml_out",
    )(h_f, h_b, proj, ml_norm.reshape(1, w))


def _merge_kernel(b0_ref, b1_ref, b2_ref, b3_ref, w_ref, g0_ref, g1_ref, g2_ref, g3_ref, z_ref):
    acc = g0_ref[...].astype(F32) * _dot(b0_ref[...], w_ref[0])
    acc += g1_ref[...].astype(F32) * _dot(b1_ref[...], w_ref[1])
    acc += g2_ref[...].astype(F32) * _dot(b2_ref[...], w_ref[2])
    acc += g3_ref[...].astype(F32) * _dot(b3_ref[...], w_ref[3])
    z_ref[...] = acc.astype(z_ref.dtype)


def _merge(branches, w_branch, layer, gates, rows, tm):
    tn = 512
    nj = D_MODEL // tn
    br = pl.BlockSpec((tm, BRANCH_W), lambda i, j: (i, 0))
    gate = lambda g: pl.BlockSpec((tm, tn), lambda i, j: (i, g * nj + j))
    return pl.pallas_call(
        _merge_kernel,
        out_shape=jax.ShapeDtypeStruct((rows, D_MODEL), BF16),
        grid=(rows // tm, nj),
        in_specs=[br, br, br, br, pl.BlockSpec((None, N_BRANCH, BRANCH_W, tn), lambda i, j: (layer, 0, 0, j)),
                  gate(0), gate(1), gate(2), gate(3)],
        out_specs=pl.BlockSpec((tm, tn), lambda i, j: (i, j)),
        compiler_params=_cparams(40),
        name="merge",
    )(*branches, w_branch, gates, gates, gates, gates)


def _mm_norm_res_kernel(*refs, nj, tn, n_x, with_next):
    a_ref, w_ref = refs[:2]
    x_refs = refs[2:2 + n_x]
    g_ref, gt_ref = refs[2 + n_x:4 + n_x]
    rest = refs[4 + n_x:]
    if with_next:
        g2_ref, sh_ref, sc_ref, o_ref, h_ref, y_ref = rest
    else:
        o_ref, y_ref = rest
    j = pl.program_id(1)
    y_ref[j] = _dot(a_ref[...], w_ref[...])

    @pl.when(j == nj - 1)
    def _finish():
        ss = jnp.sum(y_ref[0] * y_ref[0], axis=-1, keepdims=True)
        for jj in range(1, nj):
            ss += jnp.sum(y_ref[jj] * y_ref[jj], axis=-1, keepdims=True)
        rs = lax.rsqrt(ss * (1.0 / D_MODEL) + EPS)

        def emit(x_ref):
            ss2 = jnp.zeros((TOK_TM, 1), F32)
            for jj in range(nj):
                cols = slice(jj * tn, (jj + 1) * tn)
                x_new = x_ref[:, cols] + gt_ref[:, cols] * (y_ref[jj] * rs * g_ref[:, cols])
                o_ref[:, cols] = x_new
                ss2 += jnp.sum(x_new * x_new, axis=-1, keepdims=True)
            if with_next:
                rs2 = lax.rsqrt(ss2 * (1.0 / D_MODEL) + EPS)
                for jj in range(nj):
                    cols = slice(jj * tn, (jj + 1) * tn)
                    hn = o_ref[:, cols] * rs2 * g2_ref[:, cols]
                    h_ref[:, cols] = (hn * (1.0 + sc_ref[:, cols]) + sh_ref[:, cols]).astype(BF16)

        _for_tile_source(x_refs, emit)


def _matmul_norm_residual(a, w, x_pair, g, mod4, layer, gate_chunk, n_tiles, tn, next_norm=None):
    k_dim = a.shape[1]
    nj = D_MODEL // tn
    rows = n_tiles * TOK_TM
    x_specs, x_args = _x_specs(x_pair, n_tiles)
    vec = pl.BlockSpec((1, D_MODEL), lambda i, j: (0, 0))
    tile = pl.BlockSpec((TOK_TM, D_MODEL), lambda i, j: (i, 0))
    in_specs = ([pl.BlockSpec((TOK_TM, k_dim), lambda i, j: (i, 0)),
                 pl.BlockSpec((None, k_dim, tn), lambda i, j: (layer, 0, j))]
                + x_specs + [vec, _mod_spec(layer, gate_chunk)])
    args = [a, w, *x_args, g.reshape(1, D_MODEL), mod4]
    out_shape = jax.ShapeDtypeStruct((rows, D_MODEL), F32)
    out_specs = tile
    if next_norm is not None:
        g2, layer2, shift_chunk, scale_chunk = next_norm
        in_specs += [vec, _mod_spec(layer2, shift_chunk), _mod_spec(layer2, scale_chunk)]
        args += [g2.reshape(1, D_MODEL), mod4, mod4]
        out_shape = (out_shape, jax.ShapeDtypeStruct((rows, D_MODEL), BF16))
        out_specs = (tile, tile)
    return pl.pallas_call(
        functools.partial(_mm_norm_res_kernel, nj=nj, tn=tn, n_x=len(x_args), with_next=next_norm is not None),
        out_shape=out_shape,
        grid=(n_tiles, nj),
        in_specs=in_specs,
        out_specs=out_specs,
        scratch_shapes=[pltpu.VMEM((nj, TOK_TM, tn), F32)],
        compiler_params=_cparams(54),
        name="matmul_norm_res",
    )(*args)


FFN_TF = 512
FFN_HALO = 16
FFN_ROW_BLOCKS = 8
FFN_ALIGN = 16
FFN_LAG = 2 * FFN_ALIGN


def _ffn_up_kernel(hp_ref, hm_ref, hn_ref, wa_ref, wg_ref, cwa_ref, cwg_ref, cba_ref, cbg_ref, o_ref,
                   hext_ref, ua_ref, ug_ref, *, tm):
    i = pl.program_id(0)
    j = pl.program_id(1)
    halo = FFN_HALO
    ext = tm + 2 * halo
    rb = -(-ext // (FFN_ROW_BLOCKS * FFN_ALIGN)) * FFN_ALIGN

    @pl.when(j == 0)
    def _assemble_rows():
        hext_ref[0:halo, :] = hp_ref[...]
        hext_ref[halo:halo + tm, :] = hm_ref[...]
        hext_ref[halo + tm:ext, :] = hn_ref[...]

    wa = wa_ref[...].astype(BF16)
    wg = wg_ref[...].astype(BF16)

    def conv(u_ref, cw_ref, cb_ref, lo, n, has_prev, has_next):
        prev = u_ref[pl.ds(lo + halo - 1, n), :] * has_prev
        mid = u_ref[pl.ds(lo + halo, n), :]
        nxt = u_ref[pl.ds(lo + halo + 1, n), :] * has_next
        return prev * cw_ref[0:1, :] + mid * cw_ref[1:2, :] + nxt * cw_ref[2:3, :] + cb_ref[...]

    lo = 0
    for blk in range(FFN_ROW_BLOCKS):
        rows = slice(blk * rb, min((blk + 1) * rb, ext))
        hx = hext_ref[rows, :]
        ua_ref[rows, :] = _dot(hx, wa)
        ug_ref[rows, :] = _dot(hx, wg)
        hi = tm if blk == FFN_ROW_BLOCKS - 1 else (blk + 1) * rb - FFN_LAG
        n = hi - lo
        row = i * tm + lo + lax.broadcasted_iota(jnp.int32, (n, 1), 0)
        first = (row == 0) | (row == SEQ) | (row == T_LAT) | (row == T_LAT + CTX_LEN)
        final = (row == SEQ - 1) | (row == T_LAT - 1) | (row == T_LAT + CTX_LEN - 1) | (row == T_ALL - 1)
        has_prev = jnp.where(first, 0.0, 1.0)
        has_next = jnp.where(final, 0.0, 1.0)
        a = conv(ua_ref, cwa_ref, cba_ref, lo, n, has_prev, has_next)
        g = conv(ug_ref, cwg_ref, cbg_ref, lo, n, has_prev, has_next)
        o_ref[lo:hi, :] = (a * (g * jax.nn.sigmoid(g))).astype(o_ref.dtype)
        lo = hi


def _ffn_up(h, w_up, layer, conv_w, conv_b, rows, tm):
    tf, halo = FFN_TF, FFN_HALO
    nf = FFN_DIM // tf
    hb = tm // halo
    n_halo_blocks = rows // halo
    ext = tm + 2 * halo
    assert tm % halo == 0 and rows % tm == 0
    conv_b = conv_b.reshape(1, 2 * FFN_DIM)
    return pl.pallas_call(
        functools.partial(_ffn_up_kernel, tm=tm),
        out_shape=jax.ShapeDtypeStruct((rows, FFN_DIM), BF16),
        grid=(rows // tm, nf),
        in_specs=[
            pl.BlockSpec((halo, D_MODEL), lambda i, j: (jnp.maximum(i * hb - 1, 0), 0)),
            pl.BlockSpec((tm, D_MODEL), lambda i, j: (i, 0)),
            pl.BlockSpec((halo, D_MODEL), lambda i, j: (jnp.minimum((i + 1) * hb, n_halo_blocks - 1), 0)),
            pl.BlockSpec((None, D_MODEL, tf), lambda i, j: (layer, 0, j)),
            pl.BlockSpec((None, D_MODEL, tf), lambda i, j: (layer, 0, nf + j)),
            pl.BlockSpec((3, tf), lambda i, j: (0, j)),
            pl.BlockSpec((3, tf), lambda i, j: (0, nf + j)),
            pl.BlockSpec((1, tf), lambda i, j: (0, j)),
            pl.BlockSpec((1, tf), lambda i, j: (0, nf + j)),
        ],
        out_specs=pl.BlockSpec((tm, tf), lambda i, j: (i, j)),
        scratch_shapes=[pltpu.VMEM((ext, D_MODEL), BF16), pltpu.VMEM((ext, tf), F32), pltpu.VMEM((ext, tf), F32)],
        compiler_params=_cparams(52),
        name="ffn_up",
    )(h, h, h, w_up, w_up, conv_w, conv_w, conv_b, conv_b)


def kernel(x, c, ctx, c_ctx, w_ada, b_ada, g_mix_pre, g_mix_post, g_ffn_pre, g_ffn_post, w_in, na_rpb, wg_sink,
           s5_lam_re, s5_lam_im, s5_log_step, s5_b_re, s5_b_im, s5_c_re, s5_c_im, s5_d, s5_w_glu, s5_b_glu,
           ml_gate_bias, ml_norm, w_branch, w_out, w_up, ffn_conv_w, ffn_conv_b, w_down):
    assert x.shape == (BATCH, SEQ, D_MODEL) and ctx.shape == (BATCH, CTX_LEN, D_MODEL)
    xs = (x.reshape(T_LAT, D_MODEL), ctx.reshape(T_CTX, D_MODEL))
    cs = jnp.concatenate([c, c_ctx[None, :], jnp.zeros((8 - BATCH - 1, D_MODEL), F32)], axis=0)
    mod4 = _ada(cs, w_ada, b_ada).reshape(DEPTH, 8, 1, 6 * D_MODEL)
    cos_t, sin_t = _rope_tables()

    na_tabs = _na_bias_table(na_rpb.reshape(DEPTH * NA_HEADS, 2 * NA_WIN_ROWS - 1, 2 * NA_WIN_COLS - 1))
    s5_tabs = _s5_tables(*_s5_discretise(s5_lam_re, s5_lam_im, s5_log_step, s5_b_re, s5_b_im), s5_c_re, s5_c_im)

    w_mlg = jnp.pad(w_in[:, :, COL_MLG:COL_GATE], ((0, 0), (0, 0), (0, 128 - 4 * ML_HEADS))).astype(BF16)
    w_branch_bf = w_branch.astype(BF16)
    w_out_bf = w_out.astype(BF16)
    w_down_bf = w_down.astype(BF16)
    w_glu_bf = s5_w_glu.astype(BF16)

    for l in range(DEPTH):
        ctx_out = l < DEPTH - 1
        n_tiles = N_ALL_TILES if ctx_out else N_LAT_TILES
        rows = n_tiles * TOK_TM
        big_tm = rows // 4

        w_main = w_in[l, :, :PROJ_W].astype(BF16)
        w_gate = w_in[l, :, COL_GATE:].astype(BF16)

        if l == 0:
            h = _norm_mod(xs, g_mix_pre[l], mod4, l, 0, 1, N_ALL_TILES)
        proj = _matmul(h, w_main, PROJ_W, 512, F32, tm=T_ALL // 4, rows=T_ALL)
        mlg = _matmul(h, w_mlg, 128, 128, F32, tm=T_ALL // 4, rows=T_ALL, layer=l)
        gates = _matmul(h, w_gate, GATE_W, 1024, BF16, tm=big_tm, rows=rows, act="sigmoid")

        br_na = _neighbourhood_attention(proj, na_tabs, l)
        br_wg = _windowed_gqa(proj, wg_sink[l], cos_t, sin_t)

        ys = [_s5_scan(proj, *s5_tabs, l, rev=dr == 1) for dr in range(2)]
        br_s5 = _s5_out(proj, ys[0], ys[1], s5_d[l], w_glu_bf, l, s5_b_glu[l])

        gate_bias = jnp.pad(ml_gate_bias[l].reshape(1, 4 * ML_HEADS), ((0, 0), (0, 128 - 4 * ML_HEADS)))
        h_f, h_b = _mlstm(proj, mlg, gate_bias)
        br_ml = _ml_out(h_f, h_b, proj, ml_norm[l])

        z = _merge((br_na, br_s5, br_wg, br_ml), w_branch_bf, l, gates, rows, rows // 8)
        xs, h2 = _matmul_norm_residual(z, w_out_bf, xs, g_mix_post[l], mod4, l, 2, n_tiles, 1024,
                                       next_norm=(g_ffn_pre[l], l, 3, 4))
        act = _ffn_up(h2, w_up, l, ffn_conv_w[l], ffn_conv_b[l], rows, rows // 8)
        if ctx_out:
            xs, h = _matmul_norm_residual(act, w_down_bf, xs, g_ffn_post[l], mod4, l, 5, n_tiles, 512,
                                          next_norm=(g_mix_pre[l + 1], l + 1, 0, 1))
        else:
            xs = _matmul_norm_residual(act, w_down_bf, xs, g_ffn_post[l], mod4, l, 5, n_tiles, 512)

    return xs.reshape(BATCH, SEQ, D_MODEL)
```

```python
import functools

import jax
import jax.numpy as jnp
from jax import lax
from jax.experimental import pallas as pl
from jax.experimental.pallas import tpu as pltpu

F32 = jnp.float32
BF16 = jnp.bfloat16

D_MODEL = 2048
BATCH = 2
SEQ = 4096
DEPTH = 2
GRID_W = 64
CTX_LEN = 256
HEAD_DIM = 128
BRANCH_W = 512
N_BRANCH = 4
NA_HEADS = 4
NA_WIN_ROWS = 8
NA_WIN_COLS = 16
S5_GROUP = 16
S5_GROUPS = BRANCH_W // S5_GROUP
S5_STATE = 64
S5_CH = S5_GROUPS * S5_STATE
WG_Q_HEADS = 4
WG_KV_HEADS = 2
WG_WINDOW = 128
WG_BLOCK = 128
ML_HEADS = 4
ML_CHUNK = 128
FFN_DIM = 5632
ROPE_BASE = 10000.0
EPS = 1e-6
NEG_INF = -1e30

T_LAT = BATCH * SEQ
T_CTX = BATCH * CTX_LEN
T_ALL = T_LAT + T_CTX
GRID_ROWS = SEQ // GRID_W
ATT_SCALE = HEAD_DIM ** -0.5

COL_NAQ, COL_NAK, COL_NAV, COL_S5 = 0, 512, 1024, 1536
COL_WGQ, COL_WGK, COL_WGV = 2048, 2560, 2816
COL_MLQ, COL_MLK, COL_MLV, COL_MLO = 3072, 3584, 4096, 4608
PROJ_W = 5120
COL_MLG = PROJ_W
COL_GATE = PROJ_W + 4 * ML_HEADS
GATE_W = N_BRANCH * D_MODEL

MIB = 1024 * 1024
TOK_TM = 512
N_LAT_TILES = T_LAT // TOK_TM
N_ALL_TILES = T_ALL // TOK_TM
ATT_QBLK = 256
NA_ROWS_PER_STEP = ATT_QBLK // GRID_W
S5_CHUNK = 256
S5_SUB = 8


def _cparams(vmem_mib=None):
    if vmem_mib is None:
        return None
    return pltpu.CompilerParams(vmem_limit_bytes=vmem_mib * MIB)


def _dot(a, b):
    return jnp.dot(a, b, preferred_element_type=F32)


def _dot_nt(a, b):
    return lax.dot_general(a, b, (((1,), (1,)), ((), ())), preferred_element_type=F32)


def _mod_row(i):
    return jnp.where(i >= N_LAT_TILES, BATCH, i // (N_LAT_TILES // BATCH))


def _mod_spec(layer, chunk):
    return pl.BlockSpec((None, None, 1, D_MODEL), lambda i, *_: (layer, _mod_row(i), 0, chunk))


def _seq_block(b, j, blk, rev):
    n_ctx = CTX_LEN // blk
    n_lat = SEQ // blk
    if rev:
        ctx = T_LAT // blk + b * n_ctx + (n_ctx - 1 - j)
        lat = b * n_lat + (n_lat - 1 - (j - n_ctx))
    else:
        ctx = T_LAT // blk + b * n_ctx + j
        lat = b * n_lat + (j - n_ctx)
    return jnp.where(j < n_ctx, ctx, lat)


def _ada_kernel(c_ref, w_ref, b_ref, o_ref):
    c = c_ref[...]
    s = (c * jax.nn.sigmoid(c)).astype(BF16)
    o_ref[...] = _dot(s, w_ref[...].astype(BF16)) + b_ref[...]


def _ada(cs, w_ada, b_ada):
    tn = 1024
    n_out = 6 * D_MODEL
    return pl.pallas_call(
        _ada_kernel,
        out_shape=jax.ShapeDtypeStruct((DEPTH, 8, n_out), F32),
        grid=(DEPTH, n_out // tn),
        in_specs=[
            pl.BlockSpec((8, D_MODEL), lambda l, j: (0, 0)),
            pl.BlockSpec((None, D_MODEL, tn), lambda l, j: (l, 0, j)),
            pl.BlockSpec((None, 1, tn), lambda l, j: (l, 0, j)),
        ],
        out_specs=pl.BlockSpec((None, 8, tn), lambda l, j: (l, 0, j)),
        compiler_params=_cparams(40),
        name="ada",
    )(cs, w_ada, b_ada.reshape(DEPTH, 1, n_out))


def _x_specs(x_pair, n_tiles):
    if isinstance(x_pair, tuple):
        lat, ctx = x_pair
        last = N_LAT_TILES - 1
        specs = [pl.BlockSpec((TOK_TM, D_MODEL), lambda i, *_: (jnp.minimum(i, last), 0)),
                 pl.BlockSpec((TOK_TM, D_MODEL), lambda i, *_: (0, 0))]
        return specs, [lat, ctx]
    return [pl.BlockSpec((TOK_TM, D_MODEL), lambda i, *_: (i, 0))], [x_pair]


def _for_tile_source(x_refs, fn):
    if len(x_refs) == 1:
        fn(x_refs[0])
        return
    i = pl.program_id(0)
    pl.when(i < N_LAT_TILES)(lambda: fn(x_refs[0]))
    pl.when(i >= N_LAT_TILES)(lambda: fn(x_refs[1]))


def _norm_mod_kernel(*refs):
    *x_refs, g_ref, sh_ref, sc_ref, o_ref = refs

    def body(x_ref):
        x = x_ref[...]
        y = x * lax.rsqrt(jnp.mean(x * x, axis=-1, keepdims=True) + EPS) * g_ref[...]
        o_ref[...] = (y * (1.0 + sc_ref[...]) + sh_ref[...]).astype(BF16)

    _for_tile_source(x_refs, body)


def _norm_mod(x_pair, g, mod4, layer, shift_chunk, scale_chunk, n_tiles):
    x_specs, x_args = _x_specs(x_pair, n_tiles)
    return pl.pallas_call(
        _norm_mod_kernel,
        out_shape=jax.ShapeDtypeStruct((n_tiles * TOK_TM, D_MODEL), BF16),
        grid=(n_tiles,),
        in_specs=x_specs + [pl.BlockSpec((1, D_MODEL), lambda i: (0, 0)),
                            _mod_spec(layer, shift_chunk), _mod_spec(layer, scale_chunk)],
        out_specs=pl.BlockSpec((TOK_TM, D_MODEL), lambda i: (i, 0)),
        name="norm_mod",
    )(*x_args, g.reshape(1, D_MODEL), mod4, mod4)


def _mm_kernel(a_ref, w_ref, o_ref, *, act):
    r = _dot(a_ref[...], w_ref[...].astype(BF16))
    if act == "sigmoid":
        r = jax.nn.sigmoid(r)
    o_ref[...] = r.astype(o_ref.dtype)


def _matmul(a, w, n_cols, tn, out_dtype, *, tm, rows, layer=None, act=None):
    k = a.shape[1]
    if layer is None:
        w_spec = pl.BlockSpec((k, tn), lambda i, j: (0, j))
    else:
        w_spec = pl.BlockSpec((None, k, tn), lambda i, j: (layer, 0, j))
    return pl.pallas_call(
        functools.partial(_mm_kernel, act=act),
        out_shape=jax.ShapeDtypeStruct((rows, n_cols), out_dtype),
        grid=(rows // tm, n_cols // tn),
        in_specs=[pl.BlockSpec((tm, k), lambda i, j: (i, 0)), w_spec],
        out_specs=pl.BlockSpec((tm, tn), lambda i, j: (i, j)),
        compiler_params=_cparams(52),
        name="matmul",
    )(a, w)


def _att_row_block(b, s):
    n_lat = SEQ // ATT_QBLK
    return jnp.where(s == 0, T_LAT // ATT_QBLK + b, b * n_lat + s - 1)


NA_UNION_ROWS = NA_ROWS_PER_STEP + NA_WIN_ROWS - 1


NA_HEADS_PER_STEP = 2


def _na_kernel(q_ref, kl_ref, vl_ref, kc_ref, vc_ref, tb_ref, o_ref):
    step = pl.program_id(2)
    heads = range(NA_HEADS_PER_STEP)
    cols = [slice(h * HEAD_DIM, (h + 1) * HEAD_DIM) for h in heads]
    kc = [kc_ref[:, cols[h]].astype(BF16) for h in heads]
    vc = [vc_ref[:, cols[h]].astype(BF16) for h in heads]
    q = [q_ref[:, cols[h]].astype(BF16) for h in heads]
    s_ctx = [_dot_nt(q[h], kc[h]) * ATT_SCALE for h in heads]

    @pl.when(step == 0)
    def _context_queries():
        p = [jnp.exp(s_ctx[h] - jnp.max(s_ctx[h], axis=-1, keepdims=True)) for h in heads]
        o = [_dot(p[h].astype(BF16), vc[h]) / jnp.sum(p[h], axis=-1, keepdims=True) for h in heads]
        for h in heads:
            o_ref[:, cols[h]] = o[h].astype(o_ref.dtype)

    @pl.when(step > 0)
    def _latent_queries():
        rq0 = (step - 1) * NA_ROWS_PER_STEP
        u0 = jnp.clip(rq0 - NA_WIN_ROWS // 2, 0, GRID_ROWS - NA_UNION_ROWS)
        pattern = jnp.where(rq0 == 0, 0, jnp.where(rq0 == GRID_ROWS - NA_ROWS_PER_STEP, 2, 1))
        start = pl.multiple_of(u0 * GRID_W, GRID_W)
        n_win = NA_UNION_ROWS * GRID_W
        kw = [kl_ref[pl.ds(start, n_win), cols[h]].astype(BF16) for h in heads]
        vw = [vl_ref[pl.ds(start, n_win), cols[h]].astype(BF16) for h in heads]
        bias = [tb_ref[h, pattern] for h in heads]
        s_loc = [jnp.where(bias[h] > 0.5 * NEG_INF, _dot_nt(q[h], kw[h]) * ATT_SCALE + bias[h], NEG_INF)
                 for h in heads]
        m = [jnp.maximum(jnp.max(s_loc[h], axis=-1, keepdims=True), jnp.max(s_ctx[h], axis=-1, keepdims=True))
             for h in heads]
        p_loc = [jnp.exp(s_loc[h] - m[h]) for h in heads]
        p_ctx = [jnp.exp(s_ctx[h] - m[h]) for h in heads]
        den = [jnp.sum(p_loc[h], axis=-1, keepdims=True) + jnp.sum(p_ctx[h], axis=-1, keepdims=True) for h in heads]
        o = [(_dot(p_loc[h].astype(BF16), vw[h]) + _dot(p_ctx[h].astype(BF16), vc[h])) / den[h] for h in heads]
        for h in heads:
            o_ref[:, cols[h]] = o[h].astype(o_ref.dtype)


def _na_bias_table(rpb):
    assert NA_ROWS_PER_STEP == NA_WIN_ROWS // 2 and GRID_ROWS % NA_ROWS_PER_STEP == 0
    n_r, n_u = NA_ROWS_PER_STEP, NA_UNION_ROWS
    col = jnp.arange(GRID_W)
    c0 = jnp.clip(col - NA_WIN_COLS // 2, 0, GRID_W - NA_WIN_COLS)
    col_ok = (col[None, :] >= c0[:, None]) & (col[None, :] < c0[:, None] + NA_WIN_COLS)
    dc = jnp.clip(col[None, :] - col[:, None] + NA_WIN_COLS - 1, 0, 2 * NA_WIN_COLS - 2)
    per_dr = jnp.where(col_ok[None, None], rpb[:, :, dc].astype(F32), NEG_INF)
    d = jnp.arange(n_r)[:, None]
    i = jnp.arange(n_u)[None, :]
    zero = jnp.zeros_like(d)
    tabs = []
    for u_off, i0 in ((0, zero), (-(NA_WIN_ROWS // 2), d), (n_r - n_u, zero + (n_u - NA_WIN_ROWS))):
        dr = u_off + i - d + NA_WIN_ROWS - 1
        visible = (i >= i0) & (i < i0 + NA_WIN_ROWS)
        t = per_dr[:, jnp.clip(dr, 0, 2 * NA_WIN_ROWS - 2)]
        t = jnp.where(visible[None, :, :, None, None], t, NEG_INF)
        tabs.append(jnp.transpose(t, (0, 1, 3, 2, 4)).reshape(rpb.shape[0], n_r * GRID_W, n_u * GRID_W))
    return jnp.stack(tabs, axis=1)


def _neighbourhood_attention(proj, tb, layer):
    n_steps = 1 + SEQ // ATT_QBLK
    hps = NA_HEADS_PER_STEP
    hb = hps * HEAD_DIM
    n_groups = NA_HEADS // hps
    ctx_blk = T_LAT // CTX_LEN
    return pl.pallas_call(
        _na_kernel,
        out_shape=jax.ShapeDtypeStruct((T_ALL, BRANCH_W), BF16),
        grid=(BATCH, n_groups, n_steps),
        in_specs=[
            pl.BlockSpec((ATT_QBLK, hb), lambda b, h, s: (_att_row_block(b, s), COL_NAQ // hb + h)),
            pl.BlockSpec((SEQ, hb), lambda b, h, s: (b, COL_NAK // hb + h)),
            pl.BlockSpec((SEQ, hb), lambda b, h, s: (b, COL_NAV // hb + h)),
            pl.BlockSpec((CTX_LEN, hb), lambda b, h, s: (ctx_blk + b, COL_NAK // hb + h)),
            pl.BlockSpec((CTX_LEN, hb), lambda b, h, s: (ctx_blk + b, COL_NAV // hb + h)),
            pl.BlockSpec((hps, 3, ATT_QBLK, NA_UNION_ROWS * GRID_W), lambda b, h, s: (layer * n_groups + h, 0, 0, 0)),
        ],
        out_specs=pl.BlockSpec((ATT_QBLK, hb), lambda b, h, s: (_att_row_block(b, s), h)),
        compiler_params=_cparams(48),
        name="na_attn",
    )(proj, proj, proj, proj, proj, tb)


def _rope(x, cos, sin_signed):
    lane = lax.broadcasted_iota(jnp.int32, x.shape, 1)
    partner = jnp.where(lane % 64 < 32, pltpu.roll(x, 96, 1), pltpu.roll(x, 32, 1))
    return x * cos + partner * sin_signed


def _wg_kernel(sink_ref, q_ref, kl_ref, vl_ref, kc_ref, vc_ref, cos_ref, sin_ref, o_ref, kr_ref):
    hk = pl.program_id(1)
    step = pl.program_id(2)
    kc = kc_ref[...].astype(BF16)
    vc = vc_ref[...].astype(BF16)
    group = WG_Q_HEADS // WG_KV_HEADS

    @pl.when(step == 0)
    def _context_queries():
        for g in range(group):
            sink = sink_ref[hk * group + g]
            q = q_ref[:, g * HEAD_DIM:(g + 1) * HEAD_DIM].astype(BF16)
            s = _dot_nt(q, kc) * ATT_SCALE
            m = jnp.maximum(jnp.max(s, axis=-1, keepdims=True), sink)
            p = jnp.exp(s - m)
            den = jnp.sum(p, axis=-1, keepdims=True) + jnp.exp(sink - m)
            o_ref[:, g * HEAD_DIM:(g + 1) * HEAD_DIM] = (_dot(p.astype(BF16), vc) / den).astype(o_ref.dtype)

    @pl.when(step == 1)
    def _rope_keys():
        kr_ref[...] = _rope(kl_ref[...], cos_ref[...], sin_ref[...]).astype(BF16)

    @pl.when(step > 0)
    def _latent_queries():
        n_win = ATT_QBLK + 2 * WG_WINDOW
        base = pl.multiple_of((step - 1) * ATT_QBLK, ATT_QBLK)
        start = pl.multiple_of(jnp.clip(base - WG_WINDOW, 0, SEQ - n_win), WG_WINDOW)
        cos_q = cos_ref[pl.ds(base, ATT_QBLK), :]
        sin_q = sin_ref[pl.ds(base, ATT_QBLK), :]
        k_win = kr_ref[pl.ds(start, n_win), :]
        v_win = vl_ref[pl.ds(start, n_win), :].astype(BF16)
        qi = lax.broadcasted_iota(jnp.int32, (ATT_QBLK, n_win), 0)
        kj = lax.broadcasted_iota(jnp.int32, (ATT_QBLK, n_win), 1)
        in_window = jnp.abs(kj - qi + (start - base)) <= WG_WINDOW
        gs = range(group)
        sink = [sink_ref[hk * group + g] for g in gs]
        cols = [slice(g * HEAD_DIM, (g + 1) * HEAD_DIM) for g in gs]
        q = [_rope(q_ref[:, cols[g]], cos_q, sin_q).astype(BF16) for g in gs]
        s_loc = [jnp.where(in_window, _dot_nt(q[g], k_win) * ATT_SCALE, NEG_INF) for g in gs]
        s_ctx = [_dot_nt(q[g], kc) * ATT_SCALE for g in gs]
        m = [jnp.maximum(jnp.maximum(jnp.max(s_loc[g], axis=-1, keepdims=True),
                                     jnp.max(s_ctx[g], axis=-1, keepdims=True)), sink[g]) for g in gs]
        p_loc = [jnp.exp(s_loc[g] - m[g]) for g in gs]
        p_ctx = [jnp.exp(s_ctx[g] - m[g]) for g in gs]
        den = [jnp.sum(p_loc[g], axis=-1, keepdims=True) + jnp.sum(p_ctx[g], axis=-1, keepdims=True)
               + jnp.exp(sink[g] - m[g]) for g in gs]
        o = [(_dot(p_loc[g].astype(BF16), v_win) + _dot(p_ctx[g].astype(BF16), vc)) / den[g] for g in gs]
        for g in gs:
            o_ref[:, cols[g]] = o[g].astype(o_ref.dtype)


def _rope_tables():
    t = jnp.arange(SEQ)
    pos = jnp.stack([t // GRID_W, t % GRID_W], axis=-1).astype(F32)
    n_freq = HEAD_DIM // 4
    inv_freq = ROPE_BASE ** (-jnp.arange(n_freq, dtype=F32) / n_freq)
    ang = pos[:, :, None] * inv_freq
    cos, sin = jnp.cos(ang), jnp.sin(ang)
    cos_t = jnp.concatenate([cos[:, 0], cos[:, 0], cos[:, 1], cos[:, 1]], axis=-1)
    sin_t = jnp.concatenate([-sin[:, 0], sin[:, 0], -sin[:, 1], sin[:, 1]], axis=-1)
    return cos_t, sin_t


def _windowed_gqa(proj, sink, cos_t, sin_t):
    n_steps = 1 + SEQ // ATT_QBLK
    qw = (WG_Q_HEADS // WG_KV_HEADS) * HEAD_DIM
    hb = HEAD_DIM
    ctx_blk = T_LAT // CTX_LEN
    return pl.pallas_call(
        _wg_kernel,
        out_shape=jax.ShapeDtypeStruct((T_ALL, BRANCH_W), BF16),
        grid=(BATCH, WG_KV_HEADS, n_steps),
        in_specs=[
            pl.BlockSpec(memory_space=pltpu.SMEM),
            pl.BlockSpec((ATT_QBLK, qw), lambda b, h, s: (_att_row_block(b, s), COL_WGQ // qw + h)),
            pl.BlockSpec((SEQ, hb), lambda b, h, s: (b, COL_WGK // hb + h)),
            pl.BlockSpec((SEQ, hb), lambda b, h, s: (b, COL_WGV // hb + h)),
            pl.BlockSpec((CTX_LEN, hb), lambda b, h, s: (ctx_blk + b, COL_WGK // hb + h)),
            pl.BlockSpec((CTX_LEN, hb), lambda b, h, s: (ctx_blk + b, COL_WGV // hb + h)),
            pl.BlockSpec((SEQ, hb), lambda b, h, s: (0, 0)),
            pl.BlockSpec((SEQ, hb), lambda b, h, s: (0, 0)),
        ],
        out_specs=pl.BlockSpec((ATT_QBLK, qw), lambda b, h, s: (_att_row_block(b, s), h)),
        scratch_shapes=[pltpu.VMEM((SEQ, hb), BF16)],
        name="wg_attn",
    )(sink, proj, proj, proj, proj, proj, cos_t, sin_t)


def _s5_disc_kernel(lre_ref, lim_ref, lstep_ref, bre_ref, bim_ref, pre_ref, pim_ref, bbre_ref, bbim_ref):
    lre = jnp.minimum(lre_ref[...], -1e-4)
    lim = lim_ref[...]
    step = jnp.exp(lstep_ref[...])
    kk = (lax.broadcasted_iota(jnp.int32, (S5_SUB, 1), 0) + 1).astype(F32)
    mag = jnp.exp(kk * (lre * step))
    ang = kk * (lim * step)
    p_re = mag * jnp.cos(ang)
    p_im = mag * jnp.sin(ang)
    pre_ref[...] = p_re
    pim_ref[...] = p_im
    a_re = p_re[0:1, :]
    a_im = p_im[0:1, :]
    den = lre * lre + lim * lim
    f_re = ((a_re - 1.0) * lre + a_im * lim) / den
    f_im = (a_im * lre - (a_re - 1.0) * lim) / den
    br = bre_ref[...]
    bi = bim_ref[...]
    bbre_ref[...] = f_re * br - f_im * bi
    bbim_ref[...] = f_re * bi + f_im * br


def _s5_discretise(lam_re, lam_im, log_step, b_re, b_im):
    n = DEPTH * 2
    lre = lam_re.reshape(n, 1, S5_CH)
    lim = lam_im.reshape(n, 1, S5_CH)
    lstep = jnp.repeat(log_step.reshape(n, S5_GROUPS), S5_STATE, axis=-1).reshape(n, 1, S5_CH)
    br = jnp.transpose(b_re, (0, 1, 4, 2, 3)).reshape(n, S5_GROUP, S5_CH)
    bi = jnp.transpose(b_im, (0, 1, 4, 2, 3)).reshape(n, S5_GROUP, S5_CH)
    row = lambda r: pl.BlockSpec((None, r, S5_CH), lambda d: (d, 0, 0))
    outs = pl.pallas_call(
        _s5_disc_kernel,
        out_shape=(jax.ShapeDtypeStruct((n, S5_SUB, S5_CH), F32), jax.ShapeDtypeStruct((n, S5_SUB, S5_CH), F32),
                   jax.ShapeDtypeStruct((n, S5_GROUP, S5_CH), F32), jax.ShapeDtypeStruct((n, S5_GROUP, S5_CH), F32)),
        grid=(n,),
        in_specs=[row(1), row(1), row(1), row(S5_GROUP), row(S5_GROUP)],
        out_specs=(row(S5_SUB), row(S5_SUB), row(S5_GROUP), row(S5_GROUP)),
        name="s5_disc",
    )(lre, lim, lstep, br, bi)
    return [o.reshape(DEPTH, 2, *o.shape[1:]) for o in outs]


def _s5_scan_kernel(u_ref, bbre_ref, bbim_ref, cre_ref, cim_ref, tab_ref, y_ref, sre_ref, sim_ref, car_ref, *, rev):
    j = pl.program_id(1)

    @pl.when(j == 0)
    def _reset():
        car_ref[...] = jnp.zeros_like(car_ref)

    u = u_ref[...].astype(BF16)
    n_col_tiles = S5_CH // 256
    for c in range(n_col_tiles):
        ub = u[:, 128 * (c // 2):128 * (c // 2) + 128]
        sre_ref[:, 256 * c:256 * (c + 1)] = _dot(ub, bbre_ref[c])
        sim_ref[:, 256 * c:256 * (c + 1)] = _dot(ub, bbim_ref[c])

    n_groups = S5_CHUNK // S5_SUB
    last = 0 if rev else S5_SUB - 1

    def group(gi, carry):
        cr, ci = carry
        g = (n_groups - 1 - gi) if rev else gi
        r0 = pl.multiple_of(g * S5_SUB, S5_SUB)
        xr = sre_ref[pl.ds(r0, S5_SUB), :]
        xi = sim_ref[pl.ds(r0, S5_SUB), :]
        for t, k in enumerate((1, 2, 4)):
            shift = (S5_SUB - k) if rev else k
            ar = tab_ref[2 * t]
            ai = tab_ref[2 * t + 1]
            rr = pltpu.roll(xr, shift, 0)
            ri = pltpu.roll(xi, shift, 0)
            xr, xi = xr + ar * rr - ai * ri, xi + ar * ri + ai * rr
        apr = tab_ref[6]
        api = tab_ref[7]
        xr, xi = xr + apr * cr - api * ci, xi + apr * ci + api * cr
        sre_ref[pl.ds(r0, S5_SUB), :] = xr
        sim_ref[pl.ds(r0, S5_SUB), :] = xi
        return xr[last:last + 1, :], xi[last:last + 1, :]

    cr, ci = lax.fori_loop(0, n_groups, group, (car_ref[0:1, :], car_ref[1:2, :]))
    car_ref[0:1, :] = cr
    car_ref[1:2, :] = ci

    n_out_tiles = BRANCH_W // 128
    kw = S5_CH // n_out_tiles
    for oc in range(n_out_tiles):
        sr = sre_ref[:, kw * oc:kw * (oc + 1)].astype(BF16)
        si = sim_ref[:, kw * oc:kw * (oc + 1)].astype(BF16)
        y_ref[:, 128 * oc:128 * (oc + 1)] = _dot(sr, cre_ref[oc]) + _dot(si, cim_ref[oc])


def _s5_tables(p_re, p_im, bb_re, bb_im, c_re, c_im):
    row = jnp.arange(S5_SUB)
    ks = jnp.array([1, 2, 4])
    keep = jnp.stack([row[None, :] >= ks[:, None], row[None, :] <= S5_SUB - 1 - ks[:, None]]).astype(F32)

    def shift_tabs(p):
        return keep[None, :, :, :, None] * p[:, :, ks - 1][:, :, :, None, :]

    def carry_tabs(p):
        return jnp.stack([p[:, 0], p[:, 1, ::-1]], axis=1)

    s_re, s_im = shift_tabs(p_re), shift_tabs(p_im)
    tab = jnp.stack([s_re[:, :, 0], s_im[:, :, 0], s_re[:, :, 1], s_im[:, :, 1], s_re[:, :, 2], s_im[:, :, 2],
                     carry_tabs(p_re), carry_tabs(p_im)], axis=2)

    n_in = S5_CH // 256
    c_idx, gl, gj = jnp.arange(n_in)[:, None, None], jnp.arange(8)[None, :, None], jnp.arange(4)[None, None, :]
    in_mask = (gl == 4 * (c_idx % 2) + gj).astype(F32)

    def in_tiles(bb):
        x = bb.reshape(DEPTH, 2, S5_GROUP, n_in, 4, S5_STATE)
        x = jnp.transpose(x, (0, 1, 3, 2, 4, 5))
        t = in_mask[None, None, :, :, None, :, None] * x[:, :, :, None]
        return t.reshape(DEPTH, 2, n_in, 128, 256).astype(BF16)

    n_out = BRANCH_W // 128
    eye8 = jnp.eye(8, dtype=F32)

    def out_tiles(cc):
        x = cc.astype(F32).reshape(DEPTH, 2, n_out, 8, S5_GROUP, S5_STATE)
        x = jnp.transpose(x, (0, 1, 2, 5, 3, 4))
        t = eye8[None, None, None, :, None, :, None] * x[:, :, :, None]
        return t.reshape(DEPTH, 2, n_out, S5_CH // n_out, 128).astype(BF16)

    return tab, in_tiles(bb_re), in_tiles(bb_im), out_tiles(c_re), out_tiles(-c_im)


def _s5_scan(proj, tab, bbre_t, bbim_t, cre_t, cim_t, layer, rev):
    n_chunks = (CTX_LEN + SEQ) // S5_CHUNK
    blk = functools.partial(_seq_block, blk=S5_CHUNK, rev=rev)
    d = 1 if rev else 0

    def full(arr):
        shape = arr.shape[2:]
        return pl.BlockSpec((None, None) + shape, lambda b, j: (layer, d) + (0,) * len(shape))

    return pl.pallas_call(
        functools.partial(_s5_scan_kernel, rev=rev),
        out_shape=jax.ShapeDtypeStruct((T_ALL, BRANCH_W), F32),
        grid=(BATCH, n_chunks),
        in_specs=[
            pl.BlockSpec((S5_CHUNK, BRANCH_W), lambda b, j: (blk(b, j), COL_S5 // BRANCH_W)),
            full(bbre_t), full(bbim_t), full(cre_t), full(cim_t), full(tab),
        ],
        out_specs=pl.BlockSpec((S5_CHUNK, BRANCH_W), lambda b, j: (blk(b, j), 0)),
        scratch_shapes=[pltpu.VMEM((S5_CHUNK, S5_CH), F32), pltpu.VMEM((S5_CHUNK, S5_CH), F32),
                        pltpu.VMEM((8, S5_CH), F32)],
        name="s5_scan_bwd" if rev else "s5_scan_fwd",
    )(proj, bbre_t, bbim_t, cre_t, cim_t, tab)


def _s5_out_kernel(u_ref, yf_ref, yb_ref, d_ref, w_ref, b_ref, o_ref):
    y = d_ref[...] * u_ref[...] + yf_ref[...] + yb_ref[...]
    y = jax.nn.gelu(y)
    z = _dot(y.astype(BF16), w_ref[...]) + b_ref[...]
    o_ref[...] = (y * jax.nn.sigmoid(z)).astype(o_ref.dtype)


def _mlstm_kernel(qf_ref, kf_ref, vf_ref, gf_ref, qb_ref, kb_ref, vb_ref, gb_ref, bias_ref, hf_ref, hb_ref,
                  cf_ref, nf_ref, mf_ref, cb_ref, nb_ref, mb_ref):
    j = pl.program_id(1)

    @pl.when(j == 0)
    def _reset():
        for ref in (cf_ref, nf_ref, mf_ref, cb_ref, nb_ref, mb_ref):
            ref[...] = jnp.zeros_like(ref)

    L = ML_CHUNK
    q_refs, k_refs, v_refs, g_refs = (qf_ref, qb_ref), (kf_ref, kb_ref), (vf_ref, vb_ref), (gf_ref, gb_ref)
    h_refs, c_refs, n_refs, m_refs = (hf_ref, hb_ref), (cf_ref, cb_ref), (nf_ref, nb_ref), (mf_ref, mb_ref)
    dirs = range(2)
    chains = [(d, h) for d in dirs for h in range(ML_HEADS)]
    c_prev = [c_refs[d][h] for d, h in chains]
    n_prev = [n_refs[d][h][0:1, :] for d, h in chains]
    m_prev = [m_refs[d][h][0:1, 0:1] for d, h in chains]

    t_idx = lax.broadcasted_iota(jnp.int32, (L, L), 0)
    s_idx = lax.broadcasted_iota(jnp.int32, (L, L), 1)
    tri_dir = [s_idx <= t_idx, s_idx >= t_idx]
    tri_bf = [jnp.where(t, 1.0, 0.0).astype(BF16) for t in tri_dir]
    ones_bf = jnp.ones((L, HEAD_DIM), BF16)

    gates = [g_refs[d][...] + bias_ref[...] for d in dirs]
    log_f = [jax.nn.log_sigmoid(gates[d]) for d in dirs]
    hi = [log_f[d].astype(BF16) for d in dirs]
    r1 = [log_f[d] - hi[d].astype(F32) for d in dirs]
    mid = [r1[d].astype(BF16) for d in dirs]
    lo = [(r1[d] - mid[d].astype(F32)).astype(BF16) for d in dirs]
    bcum = [_dot(tri_bf[d], hi[d]) + _dot(tri_bf[d], mid[d]) + _dot(tri_bf[d], lo[d]) for d in dirs]
    gates_t = [gates[d].T for d in dirs]
    bcum_t = [bcum[d].T for d in dirs]
    last = [L - 1, 0]

    n_ch = range(len(chains))
    ii = [(2 * d) * ML_HEADS + h for d, h in chains]
    fi = [(2 * d + 1) * ML_HEADS + h for d, h in chains]
    cols = [slice(h * HEAD_DIM, (h + 1) * HEAD_DIM) for d, h in chains]
    tri = [tri_dir[d] for d, h in chains]
    li_r = [gates_t[d][ii[c]:ii[c] + 1, :] for c, (d, h) in enumerate(chains)]
    bc_c = [bcum[d][:, fi[c]:fi[c] + 1] for c, (d, h) in enumerate(chains)]
    bc_r = [bcum_t[d][fi[c]:fi[c] + 1, :] for c, (d, h) in enumerate(chains)]
    b_last = [bcum_t[d][fi[c]:fi[c] + 1, last[d]:last[d] + 1] for c, (d, h) in enumerate(chains)]
    k = [k_refs[d][:, cols[c]] * ATT_SCALE for c, (d, h) in enumerate(chains)]
    v = [v_refs[d][:, cols[c]] for c, (d, h) in enumerate(chains)]
    qb = [q_refs[d][:, cols[c]].astype(BF16) for c, (d, h) in enumerate(chains)]
    kb = [k[c].astype(BF16) for c in n_ch]
    vb = [v[c].astype(BF16) for c in n_ch]

    qk = [_dot_nt(qb[c], kb[c]) for c in n_ch]
    qc = [_dot_nt(qb[c], c_prev[c].astype(BF16)) for c in n_ch]
    qn = [_dot_nt(qb[c], jnp.broadcast_to(n_prev[c], (L, HEAD_DIM)).astype(BF16)) for c in n_ch]
    v_t = [v[c].T for c in n_ch]

    log_end = [b_last[c] - bc_r[c] + li_r[c] for c in n_ch]
    m_new = [jnp.maximum(b_last[c] + m_prev[c], jnp.max(log_end[c], axis=-1, keepdims=True)) for c in n_ch]
    w_end = [jnp.exp(log_end[c] - m_new[c]) for c in n_ch]
    decay = [jnp.exp(b_last[c] + m_prev[c] - m_new[c]) for c in n_ch]

    bc_full = [jnp.broadcast_to(bc_c[c], (L, L)) for c in n_ch]
    log_w = [jnp.where(tri[c], bc_full[c] - bc_r[c] + li_r[c], NEG_INF) for c in n_ch]
    log_inter = [bc_full[c] + m_prev[c] for c in n_ch]
    m_t = [jnp.maximum(log_inter[c], jnp.broadcast_to(jnp.max(log_w[c], axis=-1, keepdims=True), (L, L)))
           for c in n_ch]
    w = [jnp.exp(log_w[c] - m_t[c]) for c in n_ch]
    inter = [jnp.exp(log_inter[c] - m_t[c]) for c in n_ch]
    sb = [(qk[c] * w[c]).astype(BF16) for c in n_ch]
    num = [inter[c] * qc[c] + _dot(sb[c], vb[c]) for c in n_ch]
    den = [inter[c] * qn[c] + _dot(sb[c], ones_bf) for c in n_ch]
    for c, (d, h) in enumerate(chains):
        h_refs[d][:, cols[c]] = num[c] / jnp.maximum(jnp.abs(den[c]), jnp.exp(-m_t[c]))

    vw_t = [(v_t[c] * w_end[c]).astype(BF16) for c in n_ch]
    c_new = [decay[c] * c_prev[c] + _dot(vw_t[c], kb[c]) for c in n_ch]
    n_new = [decay[c] * n_prev[c] + _dot(jnp.broadcast_to(w_end[c], (8, L)).astype(BF16), kb[c])[0:1, :]
             for c in n_ch]
    for c, (d, h) in enumerate(chains):
        c_refs[d][h] = c_new[c]
        n_refs[d][h] = jnp.broadcast_to(n_new[c], (8, HEAD_DIM))
        m_refs[d][h] = jnp.broadcast_to(m_new[c], (8, HEAD_DIM))


def _mlstm(proj, mlg, gate_bias):
    n_chunks = (CTX_LEN + SEQ) // ML_CHUNK
    w = ML_HEADS * HEAD_DIM

    def specs(rev):
        blk = functools.partial(_seq_block, blk=ML_CHUNK, rev=rev)
        tok = lambda cb: pl.BlockSpec((ML_CHUNK, w), lambda b, j: (blk(b, j), cb))
        return [tok(COL_MLQ // w), tok(COL_MLK // w), tok(COL_MLV // w),
                pl.BlockSpec((ML_CHUNK, 128), lambda b, j: (blk(b, j), 0))], tok(0)

    in_f, out_f = specs(False)
    in_b, out_b = specs(True)
    out = jax.ShapeDtypeStruct((T_ALL, w), F32)
    return pl.pallas_call(
        _mlstm_kernel,
        out_shape=(out, out),
        grid=(BATCH, n_chunks),
        in_specs=in_f + in_b + [pl.BlockSpec((1, 128), lambda b, j: (0, 0))],
        out_specs=(out_f, out_b),
        scratch_shapes=[pltpu.VMEM((ML_HEADS, HEAD_DIM, HEAD_DIM), F32), pltpu.VMEM((ML_HEADS, 8, HEAD_DIM), F32),
                        pltpu.VMEM((ML_HEADS, 8, HEAD_DIM), F32)] * 2,
        name="mlstm",
    )(proj, proj, proj, mlg, proj, proj, proj, mlg, gate_bias)


def _ml_out_kernel(hf_ref, hb_ref, o_ref, nrm_ref, out_ref):
    for h in range(ML_HEADS):
        cols = slice(h * HEAD_DIM, (h + 1) * HEAD_DIM)
        x = hf_ref[:, cols] + hb_ref[:, cols]
        y = x * lax.rsqrt(jnp.mean(x * x, axis=-1, keepdims=True) + EPS) * nrm_ref[:, cols]
        out_ref[:, cols] = (y * jax.nn.sigmoid(o_ref[:, cols])).astype(out_ref.dtype)


def _merge_kernel(na_ref, wgq_ref, u_ref, yf_ref, yb_ref, d_ref, wglu_ref, bglu_ref, hf_ref, hb_ref, opre_ref,
                  nrm_ref, w_ref, g0_ref, g1_ref, g2_ref, g3_ref, z_ref, s5_ref, ml_ref):
    @pl.when(pl.program_id(1) == 0)
    def _finish_branches():
        _s5_out_kernel(u_ref, yf_ref, yb_ref, d_ref, wglu_ref, bglu_ref, s5_ref)
        _ml_out_kernel(hf_ref, hb_ref, opre_ref, nrm_ref, ml_ref)

    acc = g0_ref[...].astype(F32) * _dot(na_ref[...], w_ref[0])
    acc += g1_ref[...].astype(F32) * _dot(s5_ref[...], w_ref[1])
    acc += g2_ref[...].astype(F32) * _dot(wgq_ref[...], w_ref[2])
    acc += g3_ref[...].astype(F32) * _dot(ml_ref[...], w_ref[3])
    z_ref[...] = acc.astype(z_ref.dtype)


def _merge(br_na, br_wg, proj, y_f, y_b, d_skip, w_glu, b_glu, h_f, h_b, ml_norm, w_branch, layer, gates, rows):
    tm, tn = TOK_TM, 512
    nj = D_MODEL // tn
    tok = lambda cb: pl.BlockSpec((tm, BRANCH_W), lambda i, j: (i, cb))
    vec = pl.BlockSpec((1, BRANCH_W), lambda i, j: (0, 0))
    gate = lambda g: pl.BlockSpec((tm, tn), lambda i, j: (i, g * nj + j))
    return pl.pallas_call(
        _merge_kernel,
        out_shape=jax.ShapeDtypeStruct((rows, D_MODEL), BF16),
        grid=(rows // tm, nj),
        in_specs=[tok(0), tok(0),
                  tok(COL_S5 // BRANCH_W), tok(0), tok(0), vec,
                  pl.BlockSpec((None, BRANCH_W, BRANCH_W), lambda i, j: (layer, 0, 0)), vec,
                  tok(0), tok(0), tok(COL_MLO // BRANCH_W), vec,
                  pl.BlockSpec((None, N_BRANCH, BRANCH_W, tn), lambda i, j: (layer, 0, 0, j)),
                  gate(0), gate(1), gate(2), gate(3)],
        out_specs=pl.BlockSpec((tm, tn), lambda i, j: (i, j)),
        scratch_shapes=[pltpu.VMEM((tm, BRANCH_W), BF16), pltpu.VMEM((tm, BRANCH_W), BF16)],
        compiler_params=_cparams(40),
        name="merge",
    )(br_na, br_wg, proj, y_f, y_b, d_skip.reshape(1, BRANCH_W), w_glu, b_glu.reshape(1, BRANCH_W),
      h_f, h_b, proj, ml_norm.reshape(1, BRANCH_W), w_branch, gates, gates, gates, gates)


def _mm_norm_res_kernel(*refs, nj, tn, n_x, with_next):
    a_ref, w_ref = refs[:2]
    x_refs = refs[2:2 + n_x]
    g_ref, gt_ref = refs[2 + n_x:4 + n_x]
    rest = refs[4 + n_x:]
    if with_next:
        g2_ref, sh_ref, sc_ref, o_ref, h_ref, y_ref = rest
    else:
        o_ref, y_ref = rest
    j = pl.program_id(1)
    y_ref[j] = _dot(a_ref[...], w_ref[...])

    @pl.when(j == nj - 1)
    def _finish():
        ss = jnp.sum(y_ref[0] * y_ref[0], axis=-1, keepdims=True)
        for jj in range(1, nj):
            ss += jnp.sum(y_ref[jj] * y_ref[jj], axis=-1, keepdims=True)
        rs = lax.rsqrt(ss * (1.0 / D_MODEL) + EPS)

        def emit(x_ref):
            ss2 = jnp.zeros((TOK_TM, 1), F32)
            for jj in range(nj):
                cols = slice(jj * tn, (jj + 1) * tn)
                x_new = x_ref[:, cols] + gt_ref[:, cols] * (y_ref[jj] * rs * g_ref[:, cols])
                o_ref[:, cols] = x_new
                ss2 += jnp.sum(x_new * x_new, axis=-1, keepdims=True)
            if with_next:
                rs2 = lax.rsqrt(ss2 * (1.0 / D_MODEL) + EPS)
                for jj in range(nj):
                    cols = slice(jj * tn, (jj + 1) * tn)
                    hn = o_ref[:, cols] * rs2 * g2_ref[:, cols]
                    h_ref[:, cols] = (hn * (1.0 + sc_ref[:, cols]) + sh_ref[:, cols]).astype(BF16)

        _for_tile_source(x_refs, emit)


def _matmul_norm_residual(a, w, x_pair, g, mod4, layer, gate_chunk, n_tiles, tn, next_norm=None):
    k_dim = a.shape[1]
    nj = D_MODEL // tn
    rows = n_tiles * TOK_TM
    x_specs, x_args = _x_specs(x_pair, n_tiles)
    vec = pl.BlockSpec((1, D_MODEL), lambda i, j: (0, 0))
    tile = pl.BlockSpec((TOK_TM, D_MODEL), lambda i, j: (i, 0))
    in_specs = ([pl.BlockSpec((TOK_TM, k_dim), lambda i, j: (i, 0)),
                 pl.BlockSpec((None, k_dim, tn), lambda i, j: (layer, 0, j))]
                + x_specs + [vec, _mod_spec(layer, gate_chunk)])
    args = [a, w, *x_args, g.reshape(1, D_MODEL), mod4]
    out_shape = jax.ShapeDtypeStruct((rows, D_MODEL), F32)
    out_specs = tile
    if next_norm is not None:
        g2, layer2, shift_chunk, scale_chunk = next_norm
        in_specs += [vec, _mod_spec(layer2, shift_chunk), _mod_spec(layer2, scale_chunk)]
        args += [g2.reshape(1, D_MODEL), mod4, mod4]
        out_shape = (out_shape, jax.ShapeDtypeStruct((rows, D_MODEL), BF16))
        out_specs = (tile, tile)
    return pl.pallas_call(
        functools.partial(_mm_norm_res_kernel, nj=nj, tn=tn, n_x=len(x_args), with_next=next_norm is not None),
        out_shape=out_shape,
        grid=(n_tiles, nj),
        in_specs=in_specs,
        out_specs=out_specs,
        scratch_shapes=[pltpu.VMEM((nj, TOK_TM, tn), F32)],
        compiler_params=_cparams(54),
        name="matmul_norm_res",
    )(*args)


FFN_TF = 512
FFN_HALO = 16
FFN_ROW_BLOCKS = 4
FFN_ALIGN = 16
FFN_LAG = 2 * FFN_ALIGN


def _ffn_up_kernel(hp_ref, hm_ref, hn_ref, wa_ref, wg_ref, cwa_ref, cwg_ref, cba_ref, cbg_ref, o_ref,
                   hext_ref, ua_ref, ug_ref, *, tm):
    i = pl.program_id(0)
    j = pl.program_id(1)
    halo = FFN_HALO
    ext = tm + 2 * halo
    rb = -(-ext // (FFN_ROW_BLOCKS * FFN_ALIGN)) * FFN_ALIGN

    @pl.when(j == 0)
    def _assemble_rows():
        hext_ref[0:halo, :] = hp_ref[...]
        hext_ref[halo:halo + tm, :] = hm_ref[...]
        hext_ref[halo + tm:ext, :] = hn_ref[...]

    wa = wa_ref[...].astype(BF16)
    wg = wg_ref[...].astype(BF16)

    def conv(u_ref, cw_ref, cb_ref, lo, n, has_prev, has_next):
        prev = u_ref[pl.ds(lo + halo - 1, n), :] * has_prev
        mid = u_ref[pl.ds(lo + halo, n), :]
        nxt = u_ref[pl.ds(lo + halo + 1, n), :] * has_next
        return prev * cw_ref[0:1, :] + mid * cw_ref[1:2, :] + nxt * cw_ref[2:3, :] + cb_ref[...]

    lo = 0
    for blk in range(FFN_ROW_BLOCKS):
        rows = slice(blk * rb, min((blk + 1) * rb, ext))
        hx = hext_ref[rows, :]
        ua_ref[rows, :] = _dot(hx, wa)
        ug_ref[rows, :] = _dot(hx, wg)
        hi = tm if blk == FFN_ROW_BLOCKS - 1 else (blk + 1) * rb - FFN_LAG
        n = hi - lo
        row = i * tm + lo + lax.broadcasted_iota(jnp.int32, (n, 1), 0)
        first = (row == 0) | (row == SEQ) | (row == T_LAT) | (row == T_LAT + CTX_LEN)
        final = (row == SEQ - 1) | (row == T_LAT - 1) | (row == T_LAT + CTX_LEN - 1) | (row == T_ALL - 1)
        has_prev = jnp.where(first, 0.0, 1.0)
        has_next = jnp.where(final, 0.0, 1.0)
        a = conv(ua_ref, cwa_ref, cba_ref, lo, n, has_prev, has_next)
        g = conv(ug_ref, cwg_ref, cbg_ref, lo, n, has_prev, has_next)
        o_ref[lo:hi, :] = (a * (g * jax.nn.sigmoid(g))).astype(o_ref.dtype)
        lo = hi


def _ffn_up(h, w_up, layer, conv_w, conv_b, rows, tm):
    tf, halo = FFN_TF, FFN_HALO
    nf = FFN_DIM // tf
    hb = tm // halo
    n_halo_blocks = rows // halo
    ext = tm + 2 * halo
    assert tm % halo == 0 and rows % tm == 0
    conv_b = conv_b.reshape(1, 2 * FFN_DIM)
    return pl.pallas_call(
        functools.partial(_ffn_up_kernel, tm=tm),
        out_shape=jax.ShapeDtypeStruct((rows, FFN_DIM), BF16),
        grid=(rows // tm, nf),
        in_specs=[
            pl.BlockSpec((halo, D_MODEL), lambda i, j: (jnp.maximum(i * hb - 1, 0), 0)),
            pl.BlockSpec((tm, D_MODEL), lambda i, j: (i, 0)),
            pl.BlockSpec((halo, D_MODEL), lambda i, j: (jnp.minimum((i + 1) * hb, n_halo_blocks - 1), 0)),
            pl.BlockSpec((None, D_MODEL, tf), lambda i, j: (layer, 0, j)),
            pl.BlockSpec((None, D_MODEL, tf), lambda i, j: (layer, 0, nf + j)),
            pl.BlockSpec((3, tf), lambda i, j: (0, j)),
            pl.BlockSpec((3, tf), lambda i, j: (0, nf + j)),
            pl.BlockSpec((1, tf), lambda i, j: (0, j)),
            pl.BlockSpec((1, tf), lambda i, j: (0, nf + j)),
        ],
        out_specs=pl.BlockSpec((tm, tf), lambda i, j: (i, j)),
        scratch_shapes=[pltpu.VMEM((ext, D_MODEL), BF16), pltpu.VMEM((ext, tf), F32), pltpu.VMEM((ext, tf), F32)],
        compiler_params=_cparams(52),
        name="ffn_up",
    )(h, h, h, w_up, w_up, conv_w, conv_w, conv_b, conv_b)


def kernel(x, c, ctx, c_ctx, w_ada, b_ada, g_mix_pre, g_mix_post, g_ffn_pre, g_ffn_post, w_in, na_rpb, wg_sink,
           s5_lam_re, s5_lam_im, s5_log_step, s5_b_re, s5_b_im, s5_c_re, s5_c_im, s5_d, s5_w_glu, s5_b_glu,
           ml_gate_bias, ml_norm, w_branch, w_out, w_up, ffn_conv_w, ffn_conv_b, w_down):
    assert x.shape == (BATCH, SEQ, D_MODEL) and ctx.shape == (BATCH, CTX_LEN, D_MODEL)
    xs = (x.reshape(T_LAT, D_MODEL), ctx.reshape(T_CTX, D_MODEL))
    cs = jnp.concatenate([c, c_ctx[None, :], jnp.zeros((8 - BATCH - 1, D_MODEL), F32)], axis=0)
    mod4 = _ada(cs, w_ada, b_ada).reshape(DEPTH, 8, 1, 6 * D_MODEL)
    cos_t, sin_t = _rope_tables()

    na_tabs = _na_bias_table(na_rpb.reshape(DEPTH * NA_HEADS, 2 * NA_WIN_ROWS - 1, 2 * NA_WIN_COLS - 1))
    s5_tabs = _s5_tables(*_s5_discretise(s5_lam_re, s5_lam_im, s5_log_step, s5_b_re, s5_b_im), s5_c_re, s5_c_im)

    w_mlg = jnp.pad(w_in[:, :, COL_MLG:COL_GATE], ((0, 0), (0, 0), (0, 128 - 4 * ML_HEADS))).astype(BF16)
    w_branch_bf = w_branch.astype(BF16)
    w_out_bf = w_out.astype(BF16)
    w_down_bf = w_down.astype(BF16)
    w_glu_bf = s5_w_glu.astype(BF16)

    for l in range(DEPTH):
        ctx_out = l < DEPTH - 1
        n_tiles = N_ALL_TILES if ctx_out else N_LAT_TILES
        rows = n_tiles * TOK_TM
        big_tm = rows // 4

        w_main = w_in[l, :, :PROJ_W].astype(BF16)
        w_gate = w_in[l, :, COL_GATE:].astype(BF16)

        if l == 0:
            h = _norm_mod(xs, g_mix_pre[l], mod4, l, 0, 1, N_ALL_TILES)
        proj = _matmul(h, w_main, PROJ_W, 512, F32, tm=T_ALL // 4, rows=T_ALL)
        mlg = _matmul(h, w_mlg, 128, 128, F32, tm=T_ALL // 4, rows=T_ALL, layer=l)
        gates = _matmul(h, w_gate, GATE_W, 1024, BF16, tm=big_tm, rows=rows, act="sigmoid")

        br_na = _neighbourhood_attention(proj, na_tabs, l)
        br_wg = _windowed_gqa(proj, wg_sink[l], cos_t, sin_t)

        ys = [_s5_scan(proj, *s5_tabs, l, rev=dr == 1) for dr in range(2)]

        gate_bias = jnp.pad(ml_gate_bias[l].reshape(1, 4 * ML_HEADS), ((0, 0), (0, 128 - 4 * ML_HEADS)))
        h_f, h_b = _mlstm(proj, mlg, gate_bias)

        z = _merge(br_na, br_wg, proj, ys[0], ys[1], s5_d[l], w_glu_bf, s5_b_glu[l], h_f, h_b, ml_norm[l],
                   w_branch_bf, l, gates, rows)
        xs, h2 = _matmul_norm_residual(z, w_out_bf, xs, g_mix_post[l], mod4, l, 2, n_tiles, 1024,
                                       next_norm=(g_ffn_pre[l], l, 3, 4))
        act = _ffn_up(h2, w_up, l, ffn_conv_w[l], ffn_conv_b[l], rows, rows // 8)
        if ctx_out:
            xs, h = _matmul_norm_residual(act, w_down_bf, xs, g_ffn_post[l], mod4, l, 5, n_tiles, 512,
                                          next_norm=(g_mix_pre[l + 1], l + 1, 0, 1))
        else:
            xs = _matmul_norm_residual(act, w_down_bf, xs, g_ffn_post[l], mod4, l, 5, n_tiles, 512)

    return xs.reshape(BATCH, SEQ, D_MODEL)
```

```python
import functools

import jax
import jax.numpy as jnp
from jax import lax
from jax.experimental import pallas as pl
from jax.experimental.pallas import tpu as pltpu

F32 = jnp.float32
BF16 = jnp.bfloat16

D_MODEL = 2048
BATCH = 2
SEQ = 4096
DEPTH = 2
GRID_W = 64
CTX_LEN = 256
HEAD_DIM = 128
BRANCH_W = 512
N_BRANCH = 4
NA_HEADS = 4
NA_WIN_ROWS = 8
NA_WIN_COLS = 16
S5_GROUP = 16
S5_GROUPS = BRANCH_W // S5_GROUP
S5_STATE = 64
S5_CH = S5_GROUPS * S5_STATE
WG_Q_HEADS = 4
WG_KV_HEADS = 2
WG_WINDOW = 128
WG_BLOCK = 128
ML_HEADS = 4
ML_CHUNK = 128
FFN_DIM = 5632
ROPE_BASE = 10000.0
EPS = 1e-6
NEG_INF = -1e30

T_LAT = BATCH * SEQ
T_CTX = BATCH * CTX_LEN
T_ALL = T_LAT + T_CTX
GRID_ROWS = SEQ // GRID_W
ATT_SCALE = HEAD_DIM ** -0.5

COL_NAQ, COL_NAK, COL_NAV, COL_S5 = 0, 512, 1024, 1536
COL_WGQ, COL_WGK, COL_WGV = 2048, 2560, 2816
COL_MLQ, COL_MLK, COL_MLV, COL_MLO = 3072, 3584, 4096, 4608
PROJ_W = 5120
COL_MLG = PROJ_W
COL_GATE = PROJ_W + 4 * ML_HEADS
GATE_W = N_BRANCH * D_MODEL

MIB = 1024 * 1024
TOK_TM = 512
N_LAT_TILES = T_LAT // TOK_TM
N_ALL_TILES = T_ALL // TOK_TM
ATT_QBLK = 256
NA_ROWS_PER_STEP = ATT_QBLK // GRID_W
S5_CHUNK = 256
S5_SUB = 8


def _cparams(vmem_mib=None):
    if vmem_mib is None:
        return None
    return pltpu.CompilerParams(vmem_limit_bytes=vmem_mib * MIB)


def _dot(a, b):
    return jnp.dot(a, b, preferred_element_type=F32)


def _dot_nt(a, b):
    return lax.dot_general(a, b, (((1,), (1,)), ((), ())), preferred_element_type=F32)


def _mod_row(i):
    return jnp.where(i >= N_LAT_TILES, BATCH, i // (N_LAT_TILES // BATCH))


def _mod_spec(layer, chunk):
    return pl.BlockSpec((None, None, 1, D_MODEL), lambda i, *_: (layer, _mod_row(i), 0, chunk))


def _seq_block(b, j, blk, rev):
    n_ctx = CTX_LEN // blk
    n_lat = SEQ // blk
    if rev:
        ctx = T_LAT // blk + b * n_ctx + (n_ctx - 1 - j)
        lat = b * n_lat + (n_lat - 1 - (j - n_ctx))
    else:
        ctx = T_LAT // blk + b * n_ctx + j
        lat = b * n_lat + (j - n_ctx)
    return jnp.where(j < n_ctx, ctx, lat)


def _ada_kernel(c_ref, w_ref, b_ref, o_ref):
    c = c_ref[...]
    s = (c * jax.nn.sigmoid(c)).astype(BF16)
    o_ref[...] = _dot(s, w_ref[...].astype(BF16)) + b_ref[...]


def _ada(cs, w_ada, b_ada):
    tn = 1024
    n_out = 6 * D_MODEL
    return pl.pallas_call(
        _ada_kernel,
        out_shape=jax.ShapeDtypeStruct((DEPTH, 8, n_out), F32),
        grid=(DEPTH, n_out // tn),
        in_specs=[
            pl.BlockSpec((8, D_MODEL), lambda l, j: (0, 0)),
            pl.BlockSpec((None, D_MODEL, tn), lambda l, j: (l, 0, j)),
            pl.BlockSpec((None, 1, tn), lambda l, j: (l, 0, j)),
        ],
        out_specs=pl.BlockSpec((None, 8, tn), lambda l, j: (l, 0, j)),
        compiler_params=_cparams(40),
        name="ada",
    )(cs, w_ada, b_ada.reshape(DEPTH, 1, n_out))


def _x_specs(x_pair, n_tiles):
    if isinstance(x_pair, tuple):
        lat, ctx = x_pair
        last = N_LAT_TILES - 1
        specs = [pl.BlockSpec((TOK_TM, D_MODEL), lambda i, *_: (jnp.minimum(i, last), 0)),
                 pl.BlockSpec((TOK_TM, D_MODEL), lambda i, *_: (0, 0))]
        return specs, [lat, ctx]
    return [pl.BlockSpec((TOK_TM, D_MODEL), lambda i, *_: (i, 0))], [x_pair]


def _for_tile_source(x_refs, fn):
    if len(x_refs) == 1:
        fn(x_refs[0])
        return
    i = pl.program_id(0)
    pl.when(i < N_LAT_TILES)(lambda: fn(x_refs[0]))
    pl.when(i >= N_LAT_TILES)(lambda: fn(x_refs[1]))


def _norm_mod_kernel(*refs):
    *x_refs, g_ref, sh_ref, sc_ref, o_ref = refs

    def body(x_ref):
        x = x_ref[...]
        y = x * lax.rsqrt(jnp.mean(x * x, axis=-1, keepdims=True) + EPS) * g_ref[...]
        o_ref[...] = (y * (1.0 + sc_ref[...]) + sh_ref[...]).astype(BF16)

    _for_tile_source(x_refs, body)


def _norm_mod(x_pair, g, mod4, layer, shift_chunk, scale_chunk, n_tiles):
    x_specs, x_args = _x_specs(x_pair, n_tiles)
    return pl.pallas_call(
        _norm_mod_kernel,
        out_shape=jax.ShapeDtypeStruct((n_tiles * TOK_TM, D_MODEL), BF16),
        grid=(n_tiles,),
        in_specs=x_specs + [pl.BlockSpec((1, D_MODEL), lambda i: (0, 0)),
                            _mod_spec(layer, shift_chunk), _mod_spec(layer, scale_chunk)],
        out_specs=pl.BlockSpec((TOK_TM, D_MODEL), lambda i: (i, 0)),
        name="norm_mod",
    )(*x_args, g.reshape(1, D_MODEL), mod4, mod4)


def _mm_kernel(a_ref, w_ref, o_ref, *, act):
    r = _dot(a_ref[...], w_ref[...].astype(BF16))
    if act == "sigmoid":
        r = jax.nn.sigmoid(r)
    o_ref[...] = r.astype(o_ref.dtype)


def _matmul(a, w, n_cols, tn, out_dtype, *, tm, rows, layer=None, act=None):
    k = a.shape[1]
    if layer is None:
        w_spec = pl.BlockSpec((k, tn), lambda i, j: (0, j))
    else:
        w_spec = pl.BlockSpec((None, k, tn), lambda i, j: (layer, 0, j))
    return pl.pallas_call(
        functools.partial(_mm_kernel, act=act),
        out_shape=jax.ShapeDtypeStruct((rows, n_cols), out_dtype),
        grid=(rows // tm, n_cols // tn),
        in_specs=[pl.BlockSpec((tm, k), lambda i, j: (i, 0)), w_spec],
        out_specs=pl.BlockSpec((tm, tn), lambda i, j: (i, j)),
        compiler_params=_cparams(52),
        name="matmul",
    )(a, w)


def _att_row_block(b, s):
    n_lat = SEQ // ATT_QBLK
    return jnp.where(s == 0, T_LAT // ATT_QBLK + b, b * n_lat + s - 1)


NA_UNION_ROWS = NA_ROWS_PER_STEP + NA_WIN_ROWS - 1


NA_HEADS_PER_STEP = 2


def _na_kernel(q_ref, kl_ref, vl_ref, kc_ref, vc_ref, tb_ref, o_ref):
    step = pl.program_id(2)
    heads = range(NA_HEADS_PER_STEP)
    cols = [slice(h * HEAD_DIM, (h + 1) * HEAD_DIM) for h in heads]
    kc = [kc_ref[:, cols[h]].astype(BF16) for h in heads]
    vc = [vc_ref[:, cols[h]].astype(BF16) for h in heads]
    q = [q_ref[:, cols[h]].astype(BF16) for h in heads]
    s_ctx = [_dot_nt(q[h], kc[h]) * ATT_SCALE for h in heads]

    @pl.when(step == 0)
    def _context_queries():
        p = [jnp.exp(s_ctx[h] - jnp.max(s_ctx[h], axis=-1, keepdims=True)) for h in heads]
        o = [_dot(p[h].astype(BF16), vc[h]) / jnp.sum(p[h], axis=-1, keepdims=True) for h in heads]
        for h in heads:
            o_ref[:, cols[h]] = o[h].astype(o_ref.dtype)

    @pl.when(step > 0)
    def _latent_queries():
        rq0 = (step - 1) * NA_ROWS_PER_STEP
        u0 = jnp.clip(rq0 - NA_WIN_ROWS // 2, 0, GRID_ROWS - NA_UNION_ROWS)
        pattern = jnp.where(rq0 == 0, 0, jnp.where(rq0 == GRID_ROWS - NA_ROWS_PER_STEP, 2, 1))
        start = pl.multiple_of(u0 * GRID_W, GRID_W)
        n_win = NA_UNION_ROWS * GRID_W
        kw = [kl_ref[pl.ds(start, n_win), cols[h]].astype(BF16) for h in heads]
        vw = [vl_ref[pl.ds(start, n_win), cols[h]].astype(BF16) for h in heads]
        bias = [tb_ref[h, pattern] for h in heads]
        s_loc = [jnp.where(bias[h] > 0.5 * NEG_INF, _dot_nt(q[h], kw[h]) * ATT_SCALE + bias[h], NEG_INF)
                 for h in heads]
        m = [jnp.maximum(jnp.max(s_loc[h], axis=-1, keepdims=True), jnp.max(s_ctx[h], axis=-1, keepdims=True))
             for h in heads]
        p_loc = [jnp.exp(s_loc[h] - m[h]) for h in heads]
        p_ctx = [jnp.exp(s_ctx[h] - m[h]) for h in heads]
        den = [jnp.sum(p_loc[h], axis=-1, keepdims=True) + jnp.sum(p_ctx[h], axis=-1, keepdims=True) for h in heads]
        o = [(_dot(p_loc[h].astype(BF16), vw[h]) + _dot(p_ctx[h].astype(BF16), vc[h])) / den[h] for h in heads]
        for h in heads:
            o_ref[:, cols[h]] = o[h].astype(o_ref.dtype)


def _na_bias_table(rpb):
    assert NA_ROWS_PER_STEP == NA_WIN_ROWS // 2 and GRID_ROWS % NA_ROWS_PER_STEP == 0
    n_r, n_u = NA_ROWS_PER_STEP, NA_UNION_ROWS
    col = jnp.arange(GRID_W)
    c0 = jnp.clip(col - NA_WIN_COLS // 2, 0, GRID_W - NA_WIN_COLS)
    col_ok = (col[None, :] >= c0[:, None]) & (col[None, :] < c0[:, None] + NA_WIN_COLS)
    dc = jnp.clip(col[None, :] - col[:, None] + NA_WIN_COLS - 1, 0, 2 * NA_WIN_COLS - 2)
    per_dr = jnp.where(col_ok[None, None], rpb[:, :, dc].astype(F32), NEG_INF)
    d = jnp.arange(n_r)[:, None]
    i = jnp.arange(n_u)[None, :]
    zero = jnp.zeros_like(d)
    tabs = []
    for u_off, i0 in ((0, zero), (-(NA_WIN_ROWS // 2), d), (n_r - n_u, zero + (n_u - NA_WIN_ROWS))):
        dr = u_off + i - d + NA_WIN_ROWS - 1
        visible = (i >= i0) & (i < i0 + NA_WIN_ROWS)
        t = per_dr[:, jnp.clip(dr, 0, 2 * NA_WIN_ROWS - 2)]
        t = jnp.where(visible[None, :, :, None, None], t, NEG_INF)
        tabs.append(jnp.transpose(t, (0, 1, 3, 2, 4)).reshape(rpb.shape[0], n_r * GRID_W, n_u * GRID_W))
    return jnp.stack(tabs, axis=1)


def _neighbourhood_attention(proj, tb, layer):
    n_steps = 1 + SEQ // ATT_QBLK
    hps = NA_HEADS_PER_STEP
    hb = hps * HEAD_DIM
    n_groups = NA_HEADS // hps
    ctx_blk = T_LAT // CTX_LEN
    return pl.pallas_call(
        _na_kernel,
        out_shape=jax.ShapeDtypeStruct((T_ALL, BRANCH_W), BF16),
        grid=(BATCH, n_groups, n_steps),
        in_specs=[
            pl.BlockSpec((ATT_QBLK, hb), lambda b, h, s: (_att_row_block(b, s), COL_NAQ // hb + h)),
            pl.BlockSpec((SEQ, hb), lambda b, h, s: (b, COL_NAK // hb + h)),
            pl.BlockSpec((SEQ, hb), lambda b, h, s: (b, COL_NAV // hb + h)),
            pl.BlockSpec((CTX_LEN, hb), lambda b, h, s: (ctx_blk + b, COL_NAK // hb + h)),
            pl.BlockSpec((CTX_LEN, hb), lambda b, h, s: (ctx_blk + b, COL_NAV // hb + h)),
            pl.BlockSpec((hps, 3, ATT_QBLK, NA_UNION_ROWS * GRID_W), lambda b, h, s: (layer * n_groups + h, 0, 0, 0)),
        ],
        out_specs=pl.BlockSpec((ATT_QBLK, hb), lambda b, h, s: (_att_row_block(b, s), h)),
        compiler_params=_cparams(48),
        name="na_attn",
    )(proj, proj, proj, proj, proj, tb)


def _rope(x, cos, sin_signed):
    lane = lax.broadcasted_iota(jnp.int32, x.shape, 1)
    partner = jnp.where(lane % 64 < 32, pltpu.roll(x, 96, 1), pltpu.roll(x, 32, 1))
    return x * cos + partner * sin_signed


def _wg_kernel(sink_ref, q_ref, kl_ref, vl_ref, kc_ref, vc_ref, cos_ref, sin_ref, o_ref, kr_ref):
    hk = pl.program_id(1)
    step = pl.program_id(2)
    kc = kc_ref[...].astype(BF16)
    vc = vc_ref[...].astype(BF16)
    group = WG_Q_HEADS // WG_KV_HEADS

    @pl.when(step == 0)
    def _context_queries():
        for g in range(group):
            sink = sink_ref[hk * group + g]
            q = q_ref[:, g * HEAD_DIM:(g + 1) * HEAD_DIM].astype(BF16)
            s = _dot_nt(q, kc) * ATT_SCALE
            m = jnp.maximum(jnp.max(s, axis=-1, keepdims=True), sink)
            p = jnp.exp(s - m)
            den = jnp.sum(p, axis=-1, keepdims=True) + jnp.exp(sink - m)
            o_ref[:, g * HEAD_DIM:(g + 1) * HEAD_DIM] = (_dot(p.astype(BF16), vc) / den).astype(o_ref.dtype)

    @pl.when(step == 1)
    def _rope_keys():
        kr_ref[...] = _rope(kl_ref[...], cos_ref[...], sin_ref[...]).astype(BF16)

    @pl.when(step > 0)
    def _latent_queries():
        n_win = ATT_QBLK + 2 * WG_WINDOW
        base = pl.multiple_of((step - 1) * ATT_QBLK, ATT_QBLK)
        start = pl.multiple_of(jnp.clip(base - WG_WINDOW, 0, SEQ - n_win), WG_WINDOW)
        cos_q = cos_ref[pl.ds(base, ATT_QBLK), :]
        sin_q = sin_ref[pl.ds(base, ATT_QBLK), :]
        k_win = kr_ref[pl.ds(start, n_win), :]
        v_win = vl_ref[pl.ds(start, n_win), :].astype(BF16)
        qi = lax.broadcasted_iota(jnp.int32, (ATT_QBLK, n_win), 0)
        kj = lax.broadcasted_iota(jnp.int32, (ATT_QBLK, n_win), 1)
        in_window = jnp.abs(kj - qi + (start - base)) <= WG_WINDOW
        gs = range(group)
        sink = [sink_ref[hk * group + g] for g in gs]
        cols = [slice(g * HEAD_DIM, (g + 1) * HEAD_DIM) for g in gs]
        q = [_rope(q_ref[:, cols[g]], cos_q, sin_q).astype(BF16) for g in gs]
        s_loc = [jnp.where(in_window, _dot_nt(q[g], k_win) * ATT_SCALE, NEG_INF) for g in gs]
        s_ctx = [_dot_nt(q[g], kc) * ATT_SCALE for g in gs]
        m = [jnp.maximum(jnp.maximum(jnp.max(s_loc[g], axis=-1, keepdims=True),
                                     jnp.max(s_ctx[g], axis=-1, keepdims=True)), sink[g]) for g in gs]
        p_loc = [jnp.exp(s_loc[g] - m[g]) for g in gs]
        p_ctx = [jnp.exp(s_ctx[g] - m[g]) for g in gs]
        den = [jnp.sum(p_loc[g], axis=-1, keepdims=True) + jnp.sum(p_ctx[g], axis=-1, keepdims=True)
               + jnp.exp(sink[g] - m[g]) for g in gs]
        o = [(_dot(p_loc[g].astype(BF16), v_win) + _dot(p_ctx[g].astype(BF16), vc)) / den[g] for g in gs]
        for g in gs:
            o_ref[:, cols[g]] = o[g].astype(o_ref.dtype)


def _rope_tables():
    t = jnp.arange(SEQ)
    pos = jnp.stack([t // GRID_W, t % GRID_W], axis=-1).astype(F32)
    n_freq = HEAD_DIM // 4
    inv_freq = ROPE_BASE ** (-jnp.arange(n_freq, dtype=F32) / n_freq)
    ang = pos[:, :, None] * inv_freq
    cos, sin = jnp.cos(ang), jnp.sin(ang)
    cos_t = jnp.concatenate([cos[:, 0], cos[:, 0], cos[:, 1], cos[:, 1]], axis=-1)
    sin_t = jnp.concatenate([-sin[:, 0], sin[:, 0], -sin[:, 1], sin[:, 1]], axis=-1)
    return cos_t, sin_t


def _windowed_gqa(proj, sink, cos_t, sin_t):
    n_steps = 1 + SEQ // ATT_QBLK
    qw = (WG_Q_HEADS // WG_KV_HEADS) * HEAD_DIM
    hb = HEAD_DIM
    ctx_blk = T_LAT // CTX_LEN
    return pl.pallas_call(
        _wg_kernel,
        out_shape=jax.ShapeDtypeStruct((T_ALL, BRANCH_W), BF16),
        grid=(BATCH, WG_KV_HEADS, n_steps),
        in_specs=[
            pl.BlockSpec(memory_space=pltpu.SMEM),
            pl.BlockSpec((ATT_QBLK, qw), lambda b, h, s: (_att_row_block(b, s), COL_WGQ // qw + h)),
            pl.BlockSpec((SEQ, hb), lambda b, h, s: (b, COL_WGK // hb + h)),
            pl.BlockSpec((SEQ, hb), lambda b, h, s: (b, COL_WGV // hb + h)),
            pl.BlockSpec((CTX_LEN, hb), lambda b, h, s: (ctx_blk + b, COL_WGK // hb + h)),
            pl.BlockSpec((CTX_LEN, hb), lambda b, h, s: (ctx_blk + b, COL_WGV // hb + h)),
            pl.BlockSpec((SEQ, hb), lambda b, h, s: (0, 0)),
            pl.BlockSpec((SEQ, hb), lambda b, h, s: (0, 0)),
        ],
        out_specs=pl.BlockSpec((ATT_QBLK, qw), lambda b, h, s: (_att_row_block(b, s), h)),
        scratch_shapes=[pltpu.VMEM((SEQ, hb), BF16)],
        name="wg_attn",
    )(sink, proj, proj, proj, proj, proj, cos_t, sin_t)


def _s5_disc_kernel(lre_ref, lim_ref, lstep_ref, bre_ref, bim_ref, pre_ref, pim_ref, bbre_ref, bbim_ref):
    lre = jnp.minimum(lre_ref[...], -1e-4)
    lim = lim_ref[...]
    step = jnp.exp(lstep_ref[...])
    kk = (lax.broadcasted_iota(jnp.int32, (S5_SUB, 1), 0) + 1).astype(F32)
    mag = jnp.exp(kk * (lre * step))
    ang = kk * (lim * step)
    p_re = mag * jnp.cos(ang)
    p_im = mag * jnp.sin(ang)
    pre_ref[...] = p_re
    pim_ref[...] = p_im
    a_re = p_re[0:1, :]
    a_im = p_im[0:1, :]
    den = lre * lre + lim * lim
    f_re = ((a_re - 1.0) * lre + a_im * lim) / den
    f_im = (a_im * lre - (a_re - 1.0) * lim) / den
    br = bre_ref[...]
    bi = bim_ref[...]
    bbre_ref[...] = f_re * br - f_im * bi
    bbim_ref[...] = f_re * bi + f_im * br


def _s5_discretise(lam_re, lam_im, log_step, b_re, b_im):
    n = DEPTH * 2
    lre = lam_re.reshape(n, 1, S5_CH)
    lim = lam_im.reshape(n, 1, S5_CH)
    lstep = jnp.repeat(log_step.reshape(n, S5_GROUPS), S5_STATE, axis=-1).reshape(n, 1, S5_CH)
    br = jnp.transpose(b_re, (0, 1, 4, 2, 3)).reshape(n, S5_GROUP, S5_CH)
    bi = jnp.transpose(b_im, (0, 1, 4, 2, 3)).reshape(n, S5_GROUP, S5_CH)
    row = lambda r: pl.BlockSpec((None, r, S5_CH), lambda d: (d, 0, 0))
    outs = pl.pallas_call(
        _s5_disc_kernel,
        out_shape=(jax.ShapeDtypeStruct((n, S5_SUB, S5_CH), F32), jax.ShapeDtypeStruct((n, S5_SUB, S5_CH), F32),
                   jax.ShapeDtypeStruct((n, S5_GROUP, S5_CH), F32), jax.ShapeDtypeStruct((n, S5_GROUP, S5_CH), F32)),
        grid=(n,),
        in_specs=[row(1), row(1), row(1), row(S5_GROUP), row(S5_GROUP)],
        out_specs=(row(S5_SUB), row(S5_SUB), row(S5_GROUP), row(S5_GROUP)),
        name="s5_disc",
    )(lre, lim, lstep, br, bi)
    return [o.reshape(DEPTH, 2, *o.shape[1:]) for o in outs]


def _s5_scan_kernel(u_ref, bbre_ref, bbim_ref, cre_ref, cim_ref, tab_ref, y_ref, sre_ref, sim_ref, car_ref, *, rev):
    j = pl.program_id(1)

    @pl.when(j == 0)
    def _reset():
        car_ref[...] = jnp.zeros_like(car_ref)

    u = u_ref[...].astype(BF16)
    n_col_tiles = S5_CH // 256
    for c in range(n_col_tiles):
        ub = u[:, 128 * (c // 2):128 * (c // 2) + 128]
        sre_ref[:, 256 * c:256 * (c + 1)] = _dot(ub, bbre_ref[c])
        sim_ref[:, 256 * c:256 * (c + 1)] = _dot(ub, bbim_ref[c])

    n_groups = S5_CHUNK // S5_SUB
    last = 0 if rev else S5_SUB - 1

    def group(gi, carry):
        cr, ci = carry
        g = (n_groups - 1 - gi) if rev else gi
        r0 = pl.multiple_of(g * S5_SUB, S5_SUB)
        xr = sre_ref[pl.ds(r0, S5_SUB), :]
        xi = sim_ref[pl.ds(r0, S5_SUB), :]
        for t, k in enumerate((1, 2, 4)):
            shift = (S5_SUB - k) if rev else k
            ar = tab_ref[2 * t]
            ai = tab_ref[2 * t + 1]
            rr = pltpu.roll(xr, shift, 0)
            ri = pltpu.roll(xi, shift, 0)
            xr, xi = xr + ar * rr - ai * ri, xi + ar * ri + ai * rr
        apr = tab_ref[6]
        api = tab_ref[7]
        xr, xi = xr + apr * cr - api * ci, xi + apr * ci + api * cr
        sre_ref[pl.ds(r0, S5_SUB), :] = xr
        sim_ref[pl.ds(r0, S5_SUB), :] = xi
        return xr[last:last + 1, :], xi[last:last + 1, :]

    cr, ci = lax.fori_loop(0, n_groups, group, (car_ref[0:1, :], car_ref[1:2, :]))
    car_ref[0:1, :] = cr
    car_ref[1:2, :] = ci

    n_out_tiles = BRANCH_W // 128
    kw = S5_CH // n_out_tiles
    for oc in range(n_out_tiles):
        sr = sre_ref[:, kw * oc:kw * (oc + 1)].astype(BF16)
        si = sim_ref[:, kw * oc:kw * (oc + 1)].astype(BF16)
        y_ref[:, 128 * oc:128 * (oc + 1)] = _dot(sr, cre_ref[oc]) + _dot(si, cim_ref[oc])


def _s5_tables(p_re, p_im, bb_re, bb_im, c_re, c_im):
    row = jnp.arange(S5_SUB)
    ks = jnp.array([1, 2, 4])
    keep = jnp.stack([row[None, :] >= ks[:, None], row[None, :] <= S5_SUB - 1 - ks[:, None]]).astype(F32)

    def shift_tabs(p):
        return keep[None, :, :, :, None] * p[:, :, ks - 1][:, :, :, None, :]

    def carry_tabs(p):
        return jnp.stack([p[:, 0], p[:, 1, ::-1]], axis=1)

    s_re, s_im = shift_tabs(p_re), shift_tabs(p_im)
    tab = jnp.stack([s_re[:, :, 0], s_im[:, :, 0], s_re[:, :, 1], s_im[:, :, 1], s_re[:, :, 2], s_im[:, :, 2],
                     carry_tabs(p_re), carry_tabs(p_im)], axis=2)

    n_in = S5_CH // 256
    c_idx, gl, gj = jnp.arange(n_in)[:, None, None], jnp.arange(8)[None, :, None], jnp.arange(4)[None, None, :]
    in_mask = (gl == 4 * (c_idx % 2) + gj).astype(F32)

    def in_tiles(bb):
        x = bb.reshape(DEPTH, 2, S5_GROUP, n_in, 4, S5_STATE)
        x = jnp.transpose(x, (0, 1, 3, 2, 4, 5))
        t = in_mask[None, None, :, :, None, :, None] * x[:, :, :, None]
        return t.reshape(DEPTH, 2, n_in, 128, 256).astype(BF16)

    n_out = BRANCH_W // 128
    eye8 = jnp.eye(8, dtype=F32)

    def out_tiles(cc):
        x = cc.astype(F32).reshape(DEPTH, 2, n_out, 8, S5_GROUP, S5_STATE)
        x = jnp.transpose(x, (0, 1, 2, 5, 3, 4))
        t = eye8[None, None, None, :, None, :, None] * x[:, :, :, None]
        return t.reshape(DEPTH, 2, n_out, S5_CH // n_out, 128).astype(BF16)

    return tab, in_tiles(bb_re), in_tiles(bb_im), out_tiles(c_re), out_tiles(-c_im)


def _s5_scan(proj, tab, bbre_t, bbim_t, cre_t, cim_t, layer, rev):
    n_chunks = (CTX_LEN + SEQ) // S5_CHUNK
    blk = functools.partial(_seq_block, blk=S5_CHUNK, rev=rev)
    d = 1 if rev else 0

    def full(arr):
        shape = arr.shape[2:]
        return pl.BlockSpec((None, None) + shape, lambda b, j: (layer, d) + (0,) * len(shape))

    return pl.pallas_call(
        functools.partial(_s5_scan_kernel, rev=rev),
        out_shape=jax.ShapeDtypeStruct((T_ALL, BRANCH_W), F32),
        grid=(BATCH, n_chunks),
        in_specs=[
            pl.BlockSpec((S5_CHUNK, BRANCH_W), lambda b, j: (blk(b, j), COL_S5 // BRANCH_W)),
            full(bbre_t), full(bbim_t), full(cre_t), full(cim_t), full(tab),
        ],
        out_specs=pl.BlockSpec((S5_CHUNK, BRANCH_W), lambda b, j: (blk(b, j), 0)),
        scratch_shapes=[pltpu.VMEM((S5_CHUNK, S5_CH), F32), pltpu.VMEM((S5_CHUNK, S5_CH), F32),
                        pltpu.VMEM((8, S5_CH), F32)],
        name="s5_scan_bwd" if rev else "s5_scan_fwd",
    )(proj, bbre_t, bbim_t, cre_t, cim_t, tab)


def _s5_out_kernel(u_ref, yf_ref, yb_ref, d_ref, w_ref, b_ref, o_ref):
    y = d_ref[...] * u_ref[...] + yf_ref[...] + yb_ref[...]
    y = jax.nn.gelu(y)
    z = _dot(y.astype(BF16), w_ref[...]) + b_ref[...]
    o_ref[...] = (y * jax.nn.sigmoid(z)).astype(o_ref.dtype)


def _mlstm_kernel(qf_ref, kf_ref, vf_ref, gf_ref, qb_ref, kb_ref, vb_ref, gb_ref, bias_ref, hf_ref, hb_ref,
                  cf_ref, nf_ref, mf_ref, cb_ref, nb_ref, mb_ref):
    j = pl.program_id(1)

    @pl.when(j == 0)
    def _reset():
        for ref in (cf_ref, nf_ref, mf_ref, cb_ref, nb_ref, mb_ref):
            ref[...] = jnp.zeros_like(ref)

    L = ML_CHUNK
    q_refs, k_refs, v_refs, g_refs = (qf_ref, qb_ref), (kf_ref, kb_ref), (vf_ref, vb_ref), (gf_ref, gb_ref)
    h_refs, c_refs, n_refs, m_refs = (hf_ref, hb_ref), (cf_ref, cb_ref), (nf_ref, nb_ref), (mf_ref, mb_ref)
    dirs = range(2)
    chains = [(d, h) for d in dirs for h in range(ML_HEADS)]
    c_prev = [c_refs[d][h] for d, h in chains]
    n_prev = [n_refs[d][h][0:1, :] for d, h in chains]
    m_prev = [m_refs[d][h][0:1, 0:1] for d, h in chains]

    t_idx = lax.broadcasted_iota(jnp.int32, (L, L), 0)
    s_idx = lax.broadcasted_iota(jnp.int32, (L, L), 1)
    tri_dir = [s_idx <= t_idx, s_idx >= t_idx]
    tri_bf = [jnp.where(t, 1.0, 0.0).astype(BF16) for t in tri_dir]
    ones_bf = jnp.ones((L, HEAD_DIM), BF16)

    gates = [g_refs[d][...] + bias_ref[...] for d in dirs]
    log_f = [jax.nn.log_sigmoid(gates[d]) for d in dirs]
    hi = [log_f[d].astype(BF16) for d in dirs]
    r1 = [log_f[d] - hi[d].astype(F32) for d in dirs]
    mid = [r1[d].astype(BF16) for d in dirs]
    lo = [(r1[d] - mid[d].astype(F32)).astype(BF16) for d in dirs]
    bcum = [_dot(tri_bf[d], hi[d]) + _dot(tri_bf[d], mid[d]) + _dot(tri_bf[d], lo[d]) for d in dirs]
    gates_t = [gates[d].T for d in dirs]
    bcum_t = [bcum[d].T for d in dirs]
    last = [L - 1, 0]

    n_ch = range(len(chains))
    ii = [(2 * d) * ML_HEADS + h for d, h in chains]
    fi = [(2 * d + 1) * ML_HEADS + h for d, h in chains]
    cols = [slice(h * HEAD_DIM, (h + 1) * HEAD_DIM) for d, h in chains]
    tri = [tri_dir[d] for d, h in chains]
    li_r = [gates_t[d][ii[c]:ii[c] + 1, :] for c, (d, h) in enumerate(chains)]
    bc_c = [bcum[d][:, fi[c]:fi[c] + 1] for c, (d, h) in enumerate(chains)]
    bc_r = [bcum_t[d][fi[c]:fi[c] + 1, :] for c, (d, h) in enumerate(chains)]
    b_last = [bcum_t[d][fi[c]:fi[c] + 1, last[d]:last[d] + 1] for c, (d, h) in enumerate(chains)]
    k = [k_refs[d][:, cols[c]] * ATT_SCALE for c, (d, h) in enumerate(chains)]
    v = [v_refs[d][:, cols[c]] for c, (d, h) in enumerate(chains)]
    qb = [q_refs[d][:, cols[c]].astype(BF16) for c, (d, h) in enumerate(chains)]
    kb = [k[c].astype(BF16) for c in n_ch]
    vb = [v[c].astype(BF16) for c in n_ch]

    qk = [_dot_nt(qb[c], kb[c]) for c in n_ch]
    qc = [_dot_nt(qb[c], c_prev[c].astype(BF16)) for c in n_ch]
    qn = [_dot_nt(qb[c], jnp.broadcast_to(n_prev[c], (L, HEAD_DIM)).astype(BF16)) for c in n_ch]
    v_t = [v[c].T for c in n_ch]

    log_end = [b_last[c] - bc_r[c] + li_r[c] for c in n_ch]
    m_new = [jnp.maximum(b_last[c] + m_prev[c], jnp.max(log_end[c], axis=-1, keepdims=True)) for c in n_ch]
    w_end = [jnp.exp(log_end[c] - m_new[c]) for c in n_ch]
    decay = [jnp.exp(b_last[c] + m_prev[c] - m_new[c]) for c in n_ch]

    bc_full = [jnp.broadcast_to(bc_c[c], (L, L)) for c in n_ch]
    log_w = [jnp.where(tri[c], bc_full[c] - bc_r[c] + li_r[c], NEG_INF) for c in n_ch]
    log_inter = [bc_full[c] + m_prev[c] for c in n_ch]
    m_t = [jnp.maximum(log_inter[c], jnp.broadcast_to(jnp.max(log_w[c], axis=-1, keepdims=True), (L, L)))
           for c in n_ch]
    w = [jnp.exp(log_w[c] - m_t[c]) for c in n_ch]
    inter = [jnp.exp(log_inter[c] - m_t[c]) for c in n_ch]
    sb = [(qk[c] * w[c]).astype(BF16) for c in n_ch]
    num = [inter[c] * qc[c] + _dot(sb[c], vb[c]) for c in n_ch]
    den = [inter[c] * qn[c] + _dot(sb[c], ones_bf) for c in n_ch]
    for c, (d, h) in enumerate(chains):
        h_refs[d][:, cols[c]] = num[c] / jnp.maximum(jnp.abs(den[c]), jnp.exp(-m_t[c]))

    vw_t = [(v_t[c] * w_end[c]).astype(BF16) for c in n_ch]
    c_new = [decay[c] * c_prev[c] + _dot(vw_t[c], kb[c]) for c in n_ch]
    n_new = [decay[c] * n_prev[c] + _dot(jnp.broadcast_to(w_end[c], (8, L)).astype(BF16), kb[c])[0:1, :]
             for c in n_ch]
    for c, (d, h) in enumerate(chains):
        c_refs[d][h] = c_new[c]
        n_refs[d][h] = jnp.broadcast_to(n_new[c], (8, HEAD_DIM))
        m_refs[d][h] = jnp.broadcast_to(m_new[c], (8, HEAD_DIM))


def _mlstm(proj, mlg, gate_bias):
    n_chunks = (CTX_LEN + SEQ) // ML_CHUNK
    w = ML_HEADS * HEAD_DIM

    def specs(rev):
        blk = functools.partial(_seq_block, blk=ML_CHUNK, rev=rev)
        tok = lambda cb: pl.BlockSpec((ML_CHUNK, w), lambda b, j: (blk(b, j), cb))
        return [tok(COL_MLQ // w), tok(COL_MLK // w), tok(COL_MLV // w),
                pl.BlockSpec((ML_CHUNK, 128), lambda b, j: (blk(b, j), 0))], tok(0)

    in_f, out_f = specs(False)
    in_b, out_b = specs(True)
    out = jax.ShapeDtypeStruct((T_ALL, w), F32)
    return pl.pallas_call(
        _mlstm_kernel,
        out_shape=(out, out),
        grid=(BATCH, n_chunks),
        in_specs=in_f + in_b + [pl.BlockSpec((1, 128), lambda b, j: (0, 0))],
        out_specs=(out_f, out_b),
        scratch_shapes=[pltpu.VMEM((ML_HEADS, HEAD_DIM, HEAD_DIM), F32), pltpu.VMEM((ML_HEADS, 8, HEAD_DIM), F32),
                        pltpu.VMEM((ML_HEADS, 8, HEAD_DIM), F32)] * 2,
        name="mlstm",
    )(proj, proj, proj, mlg, proj, proj, proj, mlg, gate_bias)


def _ml_out_kernel(hf_ref, hb_ref, o_ref, nrm_ref, out_ref):
    for h in range(ML_HEADS):
        cols = slice(h * HEAD_DIM, (h + 1) * HEAD_DIM)
        x = hf_ref[:, cols] + hb_ref[:, cols]
        y = x * lax.rsqrt(jnp.mean(x * x, axis=-1, keepdims=True) + EPS) * nrm_ref[:, cols]
        out_ref[:, cols] = (y * jax.nn.sigmoid(o_ref[:, cols])).astype(out_ref.dtype)


def _s5_out(proj, y_f, y_b, d_skip, w_glu, layer, b_glu):
    tm = TOK_TM
    tok = lambda cb: pl.BlockSpec((tm, BRANCH_W), lambda i: (i, cb))
    vec = pl.BlockSpec((1, BRANCH_W), lambda i: (0, 0))
    return pl.pallas_call(
        _s5_out_kernel,
        out_shape=jax.ShapeDtypeStruct((T_ALL, BRANCH_W), BF16),
        grid=(T_ALL // tm,),
        in_specs=[tok(COL_S5 // BRANCH_W), tok(0), tok(0), vec,
                  pl.BlockSpec((None, BRANCH_W, BRANCH_W), lambda i: (layer, 0, 0)), vec],
        out_specs=tok(0),
        name="s5_out",
    )(proj, y_f, y_b, d_skip.reshape(1, BRANCH_W), w_glu, b_glu.reshape(1, BRANCH_W))


def _ml_out(h_f, h_b, proj, ml_norm):
    tm = TOK_TM
    w = ML_HEADS * HEAD_DIM
    tok = lambda cb: pl.BlockSpec((tm, w), lambda i: (i, cb))
    return pl.pallas_call(
        _ml_out_kernel,
        out_shape=jax.ShapeDtypeStruct((T_ALL, w), BF16),
        grid=(T_ALL // tm,),
        in_specs=[tok(0), tok(0), tok(COL_MLO // w), pl.BlockSpec((1, w), lambda i: (0, 0))],
        out_specs=tok(0),
        name="ml_out",
    )(h_f, h_b, proj, ml_norm.reshape(1, w))


def _merge_kernel(b0_ref, b1_ref, b2_ref, b3_ref, w_ref, g0_ref, g1_ref, g2_ref, g3_ref, z_ref):
    acc = g0_ref[...].astype(F32) * _dot(b0_ref[...], w_ref[0])
    acc += g1_ref[...].astype(F32) * _dot(b1_ref[...], w_ref[1])
    acc += g2_ref[...].astype(F32) * _dot(b2_ref[...], w_ref[2])
    acc += g3_ref[...].astype(F32) * _dot(b3_ref[...], w_ref[3])
    z_ref[...] = acc.astype(z_ref.dtype)


def _merge(branches, w_branch, layer, gates, rows, tm):
    tn = 512
    nj = D_MODEL // tn
    br = pl.BlockSpec((tm, BRANCH_W), lambda i, j: (i, 0))
    gate = lambda g: pl.BlockSpec((tm, tn), lambda i, j: (i, g * nj + j))
    return pl.pallas_call(
        _merge_kernel,
        out_shape=jax.ShapeDtypeStruct((rows, D_MODEL), BF16),
        grid=(rows // tm, nj),
        in_specs=[br, br, br, br, pl.BlockSpec((None, N_BRANCH, BRANCH_W, tn), lambda i, j: (layer, 0, 0, j)),
                  gate(0), gate(1), gate(2), gate(3)],
        out_specs=pl.BlockSpec((tm, tn), lambda i, j: (i, j)),
        compiler_params=_cparams(40),
        name="merge",
    )(*branches, w_branch, gates, gates, gates, gates)


def _mm_norm_res_kernel(*refs, nj, tn, n_x, with_next):
    a_ref, w_ref = refs[:2]
    x_refs = refs[2:2 + n_x]
    g_ref, gt_ref = refs[2 + n_x:4 + n_x]
    rest = refs[4 + n_x:]
    if with_next:
        g2_ref, sh_ref, sc_ref, o_ref, h_ref, y_ref = rest
    else:
        o_ref, y_ref = rest
    j = pl.program_id(1)
    y_ref[j] = _dot(a_ref[...], w_ref[...])

    @pl.when(j == nj - 1)
    def _finish():
        ss = jnp.sum(y_ref[0] * y_ref[0], axis=-1, keepdims=True)
        for jj in range(1, nj):
            ss += jnp.sum(y_ref[jj] * y_ref[jj], axis=-1, keepdims=True)
        rs = lax.rsqrt(ss * (1.0 / D_MODEL) + EPS)

        def emit(x_ref):
            ss2 = jnp.zeros((TOK_TM, 1), F32)
            for jj in range(nj):
                cols = slice(jj * tn, (jj + 1) * tn)
                x_new = x_ref[:, cols] + gt_ref[:, cols] * (y_ref[jj] * rs * g_ref[:, cols])
                o_ref[:, cols] = x_new
                ss2 += jnp.sum(x_new * x_new, axis=-1, keepdims=True)
            if with_next:
                rs2 = lax.rsqrt(ss2 * (1.0 / D_MODEL) + EPS)
                for jj in range(nj):
                    cols = slice(jj * tn, (jj + 1) * tn)
                    hn = o_ref[:, cols] * rs2 * g2_ref[:, cols]
                    h_ref[:, cols] = (hn * (1.0 + sc_ref[:, cols]) + sh_ref[:, cols]).astype(BF16)

        _for_tile_source(x_refs, emit)


def _matmul_norm_residual(a, w, x_pair, g, mod4, layer, gate_chunk, n_tiles, tn, next_norm=None):
    k_dim = a.shape[1]
    nj = D_MODEL // tn
    rows = n_tiles * TOK_TM
    x_specs, x_args = _x_specs(x_pair, n_tiles)
    vec = pl.BlockSpec((1, D_MODEL), lambda i, j: (0, 0))
    tile = pl.BlockSpec((TOK_TM, D_MODEL), lambda i, j: (i, 0))
    in_specs = ([pl.BlockSpec((TOK_TM, k_dim), lambda i, j: (i, 0)),
                 pl.BlockSpec((None, k_dim, tn), lambda i, j: (layer, 0, j))]
                + x_specs + [vec, _mod_spec(layer, gate_chunk)])
    args = [a, w, *x_args, g.reshape(1, D_MODEL), mod4]
    out_shape = jax.ShapeDtypeStruct((rows, D_MODEL), F32)
    out_specs = tile
    if next_norm is not None:
        g2, layer2, shift_chunk, scale_chunk = next_norm
        in_specs += [vec, _mod_spec(layer2, shift_chunk), _mod_spec(layer2, scale_chunk)]
        args += [g2.reshape(1, D_MODEL), mod4, mod4]
        out_shape = (out_shape, jax.ShapeDtypeStruct((rows, D_MODEL), BF16))
        out_specs = (tile, tile)
    return pl.pallas_call(
        functools.partial(_mm_norm_res_kernel, nj=nj, tn=tn, n_x=len(x_args), with_next=next_norm is not None),
        out_shape=out_shape,
        grid=(n_tiles, nj),
        in_specs=in_specs,
        out_specs=out_specs,
        scratch_shapes=[pltpu.VMEM((nj, TOK_TM, tn), F32)],
        compiler_params=_cparams(54),
        name="matmul_norm_res",
    )(*args)


FFN_TF = 512
FFN_HALO = 16
FFN_ROW_BLOCKS = 4
FFN_ALIGN = 16
FFN_LAG = 2 * FFN_ALIGN


def _ffn_up_kernel(hp_ref, hm_ref, hn_ref, wa_ref, wg_ref, cwa_ref, cwg_ref, cba_ref, cbg_ref, o_ref,
                   hext_ref, ua_ref, ug_ref, *, tm):
    i = pl.program_id(0)
    j = pl.program_id(1)
    halo = FFN_HALO
    ext = tm + 2 * halo
    rb = -(-ext // (FFN_ROW_BLOCKS * FFN_ALIGN)) * FFN_ALIGN

    @pl.when(j == 0)
    def _assemble_rows():
        hext_ref[0:halo, :] = hp_ref[...]
        hext_ref[halo:halo + tm, :] = hm_ref[...]
        hext_ref[halo + tm:ext, :] = hn_ref[...]

    wa = wa_ref[...].astype(BF16)
    wg = wg_ref[...].astype(BF16)

    def conv(u_ref, cw_ref, cb_ref, lo, n, has_prev, has_next):
        prev = u_ref[pl.ds(lo + halo - 1, n), :] * has_prev
        mid = u_ref[pl.ds(lo + halo, n), :]
        nxt = u_ref[pl.ds(lo + halo + 1, n), :] * has_next
        return prev * cw_ref[0:1, :] + mid * cw_ref[1:2, :] + nxt * cw_ref[2:3, :] + cb_ref[...]

    lo = 0
    for blk in range(FFN_ROW_BLOCKS):
        rows = slice(blk * rb, min((blk + 1) * rb, ext))
        hx = hext_ref[rows, :]
        ua_ref[rows, :] = _dot(hx, wa)
        ug_ref[rows, :] = _dot(hx, wg)
        hi = tm if blk == FFN_ROW_BLOCKS - 1 else (blk + 1) * rb - FFN_LAG
        n = hi - lo
        row = i * tm + lo + lax.broadcasted_iota(jnp.int32, (n, 1), 0)
        first = (row == 0) | (row == SEQ) | (row == T_LAT) | (row == T_LAT + CTX_LEN)
        final = (row == SEQ - 1) | (row == T_LAT - 1) | (row == T_LAT + CTX_LEN - 1) | (row == T_ALL - 1)
        has_prev = jnp.where(first, 0.0, 1.0)
        has_next = jnp.where(final, 0.0, 1.0)
        a = conv(ua_ref, cwa_ref, cba_ref, lo, n, has_prev, has_next)
        g = conv(ug_ref, cwg_ref, cbg_ref, lo, n, has_prev, has_next)
        o_ref[lo:hi, :] = (a * (g * jax.nn.sigmoid(g))).astype(o_ref.dtype)
        lo = hi


def _ffn_up(h, w_up, layer, conv_w, conv_b, rows, tm):
    tf, halo = FFN_TF, FFN_HALO
    nf = FFN_DIM // tf
    hb = tm // halo
    n_halo_blocks = rows // halo
    ext = tm + 2 * halo
    assert tm % halo == 0 and rows % tm == 0
    conv_b = conv_b.reshape(1, 2 * FFN_DIM)
    return pl.pallas_call(
        functools.partial(_ffn_up_kernel, tm=tm),
        out_shape=jax.ShapeDtypeStruct((rows, FFN_DIM), BF16),
        grid=(rows // tm, nf),
        in_specs=[
            pl.BlockSpec((halo, D_MODEL), lambda i, j: (jnp.maximum(i * hb - 1, 0), 0)),
            pl.BlockSpec((tm, D_MODEL), lambda i, j: (i, 0)),
            pl.BlockSpec((halo, D_MODEL), lambda i, j: (jnp.minimum((i + 1) * hb, n_halo_blocks - 1), 0)),
            pl.BlockSpec((None, D_MODEL, tf), lambda i, j: (layer, 0, j)),
            pl.BlockSpec((None, D_MODEL, tf), lambda i, j: (layer, 0, nf + j)),
            pl.BlockSpec((3, tf), lambda i, j: (0, j)),
            pl.BlockSpec((3, tf), lambda i, j: (0, nf + j)),
            pl.BlockSpec((1, tf), lambda i, j: (0, j)),
            pl.BlockSpec((1, tf), lambda i, j: (0, nf + j)),
        ],
        out_specs=pl.BlockSpec((tm, tf), lambda i, j: (i, j)),
        scratch_shapes=[pltpu.VMEM((ext, D_MODEL), BF16), pltpu.VMEM((ext, tf), F32), pltpu.VMEM((ext, tf), F32)],
        compiler_params=_cparams(52),
        name="ffn_up",
    )(h, h, h, w_up, w_up, conv_w, conv_w, conv_b, conv_b)


def kernel(x, c, ctx, c_ctx, w_ada, b_ada, g_mix_pre, g_mix_post, g_ffn_pre, g_ffn_post, w_in, na_rpb, wg_sink,
           s5_lam_re, s5_lam_im, s5_log_step, s5_b_re, s5_b_im, s5_c_re, s5_c_im, s5_d, s5_w_glu, s5_b_glu,
           ml_gate_bias, ml_norm, w_branch, w_out, w_up, ffn_conv_w, ffn_conv_b, w_down):
    assert x.shape == (BATCH, SEQ, D_MODEL) and ctx.shape == (BATCH, CTX_LEN, D_MODEL)
    xs = (x.reshape(T_LAT, D_MODEL), ctx.reshape(T_CTX, D_MODEL))
    cs = jnp.concatenate([c, c_ctx[None, :], jnp.zeros((8 - BATCH - 1, D_MODEL), F32)], axis=0)
    mod4 = _ada(cs, w_ada, b_ada).reshape(DEPTH, 8, 1, 6 * D_MODEL)
    cos_t, sin_t = _rope_tables()

    na_tabs = _na_bias_table(na_rpb.reshape(DEPTH * NA_HEADS, 2 * NA_WIN_ROWS - 1, 2 * NA_WIN_COLS - 1))
    s5_tabs = _s5_tables(*_s5_discretise(s5_lam_re, s5_lam_im, s5_log_step, s5_b_re, s5_b_im), s5_c_re, s5_c_im)

    w_mlg = jnp.pad(w_in[:, :, COL_MLG:COL_GATE], ((0, 0), (0, 0), (0, 128 - 4 * ML_HEADS))).astype(BF16)
    w_branch_bf = w_branch.astype(BF16)
    w_out_bf = w_out.astype(BF16)
    w_down_bf = w_down.astype(BF16)
    w_glu_bf = s5_w_glu.astype(BF16)

    for l in range(DEPTH):
        ctx_out = l < DEPTH - 1
        n_tiles = N_ALL_TILES if ctx_out else N_LAT_TILES
        rows = n_tiles * TOK_TM
        big_tm = rows // 4

        w_main = w_in[l, :, :PROJ_W].astype(BF16)
        w_gate = w_in[l, :, COL_GATE:].astype(BF16)

        if l == 0:
            h = _norm_mod(xs, g_mix_pre[l], mod4, l, 0, 1, N_ALL_TILES)
        proj = _matmul(h, w_main, PROJ_W, 512, F32, tm=T_ALL // 4, rows=T_ALL)
        mlg = _matmul(h, w_mlg, 128, 128, F32, tm=T_ALL // 4, rows=T_ALL, layer=l)
        gates = _matmul(h, w_gate, GATE_W, 1024, BF16, tm=big_tm, rows=rows, act="sigmoid")

        br_na = _neighbourhood_attention(proj, na_tabs, l)
        br_wg = _windowed_gqa(proj, wg_sink[l], cos_t, sin_t)

        ys = [_s5_scan(proj, *s5_tabs, l, rev=dr == 1) for dr in range(2)]
        br_s5 = _s5_out(proj, ys[0], ys[1], s5_d[l], w_glu_bf, l, s5_b_glu[l])

        gate_bias = jnp.pad(ml_gate_bias[l].reshape(1, 4 * ML_HEADS), ((0, 0), (0, 128 - 4 * ML_HEADS)))
        h_f, h_b = _mlstm(proj, mlg, gate_bias)
        br_ml = _ml_out(h_f, h_b, proj, ml_norm[l])

        z = _merge((br_na, br_s5, br_wg, br_ml), w_branch_bf, l, gates, rows, rows // 8)
        xs, h2 = _matmul_norm_residual(z, w_out_bf, xs, g_mix_post[l], mod4, l, 2, n_tiles, 1024,
                                       next_norm=(g_ffn_pre[l], l, 3, 4))
        act = _ffn_up(h2, w_up, l, ffn_conv_w[l], ffn_conv_b[l], rows, rows // 8)
        if ctx_out:
            xs, h = _matmul_norm_residual(act, w_down_bf, xs, g_ffn_post[l], mod4, l, 5, n_tiles, 512,
                                          next_norm=(g_mix_pre[l + 1], l + 1, 0, 1))
        else:
            xs = _matmul_norm_residual(act, w_down_bf, xs, g_ffn_post[l], mod4, l, 5, n_tiles, 512)

    return xs.reshape(BATCH, SEQ, D_MODEL)
```
